```python
import jax, jax.numpy as jnp
from jax import lax
import numpy as np

D_MODEL = 2048
BATCH = 1
SEQ = 8192
DEPTH = 2
DEC_BATCH = 32
DEC_SEQ = 64
PAST_LEN = 4096

CHUNK = 64
HEAD_DIM = 128
H_SB = D_MODEL // (2 * HEAD_DIM)
H_RET = D_MODEL // (2 * HEAD_DIM)
D_SB = H_SB * HEAD_DIM
D_RET = H_RET * HEAD_DIM
D_AB_IN = 3 * D_SB + 4 * D_RET
D_AB_OUT = D_SB + D_RET
H_BAND = D_MODEL // HEAD_DIM
D_BAND = H_BAND * HEAD_DIM
N_BAND_CHUNKS = 8
BAND_WINDOW = N_BAND_CHUNKS * CHUNK
MAX_REL = 128
H_X = 4
D_X = H_X * HEAD_DIM
N_MEM = 256
D_FF = 5632
N_EXPERTS = 8
TOP_K = 2
D_FF_EXPERT = 2816
SB_BLOCK = 128
RMS_EPS = 1e-6
ROPE_BASE = 10000.0
NEG_INF = -1e30

kernel_name = "hybrid_stickbreak_retention_band_step"


def rmsnorm(x, g):
    xf = x.astype(jnp.float32)
    y = xf * lax.rsqrt(jnp.mean(xf * xf, axis=-1, keepdims=True) + RMS_EPS)
    return (y * g.astype(jnp.float32)).astype(x.dtype)


def heads(x, n):
    return x.reshape(x.shape[0], x.shape[1], n, HEAD_DIM)


def rope(x, pos):
    half = HEAD_DIM // 2
    inv = ROPE_BASE ** (-jnp.arange(half, dtype=jnp.float32) / half)
    ang = pos.astype(jnp.float32)[:, None] * inv[None, :]
    cos = jnp.cos(ang)[None, :, None, :]
    sin = jnp.sin(ang)[None, :, None, :]
    xf = x.astype(jnp.float32)
    x1, x2 = xf[..., :half], xf[..., half:]
    return jnp.concatenate([x1 * cos - x2 * sin, x1 * sin + x2 * cos], axis=-1).astype(x.dtype)


def stick_breaking_block(q, k, v, q_pos, k_pos):
    z = jnp.einsum("bqhd,bkhd->bhqk", q, k).astype(jnp.float32) * (HEAD_DIM ** -0.5)
    before = (k_pos[None, :] < q_pos[:, None])[None, None]
    log_1m_beta = jnp.where(before, jax.nn.log_sigmoid(-z), 0.0)
    gap = lax.cumsum(log_1m_beta, axis=3, reverse=True) - log_1m_beta
    w = jnp.where(before, jnp.exp(jax.nn.log_sigmoid(z) + gap), 0.0)
    return jnp.einsum("bhqk,bkhd->bqhd", w.astype(v.dtype), v)


def stick_breaking_prompt(q, k, v):
    b, s, h, d = q.shape
    nb = s // SB_BLOCK
    pos = jnp.arange(s, dtype=jnp.int32)
    qb = q.reshape(b, nb, SB_BLOCK, h, d).swapaxes(0, 1)
    pb = pos.reshape(nb, SB_BLOCK)
    out = lax.map(lambda a: stick_breaking_block(a[0], k, v, a[1], pos), (qb, pb))
    return out.swapaxes(0, 1).reshape(b, s, h, d)


def retention_log_decay():
    return jnp.log1p(-(2.0 ** (-5.0 - jnp.arange(H_RET, dtype=jnp.float32))))


def retention_chunk(q, k, v, state, log_g):
    t = q.shape[1]
    idx = jnp.arange(t, dtype=jnp.float32)
    rel = idx[:, None] - idx[None, :]
    decay = jnp.where(rel >= 0, jnp.exp(log_g[:, None, None] * jnp.maximum(rel, 0.0)), 0.0)
    qf, kf, vf = q.astype(jnp.float32), k.astype(jnp.float32), v.astype(jnp.float32)
    scores = jnp.einsum("bthd,bshd->bhts", qf, kf) * decay[None]
    inner = jnp.einsum("bhts,bshe->bthe", scores, vf)
    q_dec = jnp.exp((idx[:, None] + 1.0) * log_g[None, :])
    cross = jnp.einsum("bthd,bhde->bthe", qf, state) * q_dec[None, :, :, None]
    k_dec = jnp.exp((t - 1.0 - idx[:, None]) * log_g[None, :])
    new_state = (jnp.exp(t * log_g)[None, :, None, None] * state
                 + jnp.einsum("bthd,bthe->bhde", kf * k_dec[None, :, :, None], vf))
    return inner + cross, new_state


def retention_prompt(q, k, v, log_g):
    b, s, h, _ = q.shape
    nc = s // CHUNK

    def split(a):
        return a.reshape(b, nc, CHUNK, h, HEAD_DIM).swapaxes(0, 1)

    def step(state, qkv):
        o, state = retention_chunk(qkv[0], qkv[1], qkv[2], state, log_g)
        return state, o

    state0 = jnp.zeros((b, h, HEAD_DIM, HEAD_DIM), jnp.float32)
    state, o = lax.scan(step, state0, (split(q), split(k), split(v)))
    return o.swapaxes(0, 1).reshape(b, s, h, HEAD_DIM), state


def ab_project(h, pos, w_in):
    p = h @ w_in
    cuts = [D_SB, 2 * D_SB, 3 * D_SB, 3 * D_SB + D_RET, 3 * D_SB + 2 * D_RET, 3 * D_SB + 3 * D_RET]
    q_sb, k_sb, v_sb, q_r, k_r, v_r, g_r = jnp.split(p, cuts, axis=-1)
    q_r = rope(heads(q_r, H_RET), pos)
    k_r = rope(heads(k_r, H_RET), pos) * (HEAD_DIM ** -0.5)
    return (heads(q_sb, H_SB), heads(k_sb, H_SB), heads(v_sb, H_SB),
            q_r, k_r, heads(v_r, H_RET), g_r)


def ab_merge(o_sb, o_ret, g_r, ret_norm_g, w_out):
    b, t = o_sb.shape[0], o_sb.shape[1]
    of = o_ret.astype(jnp.float32)
    of = (of * lax.rsqrt(jnp.mean(of * of, axis=-1, keepdims=True) + RMS_EPS)
          * ret_norm_g.astype(jnp.float32).reshape(H_RET, HEAD_DIM))
    o_r = (of.reshape(b, t, D_RET) * jax.nn.silu(g_r.astype(jnp.float32))).astype(o_sb.dtype)
    return jnp.concatenate([o_sb.reshape(b, t, D_SB), o_r], axis=-1) @ w_out


def band_block(q, k, v, q_pos, k_pos, valid, rel_bias):
    s = jnp.einsum("bqhd,bkhd->bhqk", q, k).astype(jnp.float32) * (HEAD_DIM ** -0.5)
    rel = jnp.clip(q_pos[:, None] - k_pos[None, :], -MAX_REL, MAX_REL) + MAX_REL
    s = s + rel_bias.astype(jnp.float32)[:, rel][None]
    s = jnp.where(valid[None, None, None, :], s, NEG_INF)
    p = jax.nn.softmax(s, axis=-1)
    return jnp.einsum("bhqk,bkhd->bqhd", p.astype(v.dtype), v)


def band_prompt(q, k, v, rel_bias):
    b, s, h, d = q.shape
    nc = s // CHUNK
    pad = ((0, 0), (BAND_WINDOW, 0), (0, 0), (0, 0))
    kp, vp = jnp.pad(k, pad), jnp.pad(v, pad)
    qc = q.reshape(b, nc, CHUNK, h, d).swapaxes(0, 1)

    def one(args):
        c, qb = args
        start = c * CHUNK
        kb = lax.dynamic_slice_in_dim(kp, start, BAND_WINDOW + CHUNK, axis=1)
        vb = lax.dynamic_slice_in_dim(vp, start, BAND_WINDOW + CHUNK, axis=1)
        q_pos = start + jnp.arange(CHUNK, dtype=jnp.int32)
        k_pos = start - BAND_WINDOW + jnp.arange(BAND_WINDOW + CHUNK, dtype=jnp.int32)
        return band_block(qb, kb, vb, q_pos, k_pos, k_pos >= 0, rel_bias)

    out = lax.map(one, (jnp.arange(nc, dtype=jnp.int32), qc))
    return out.swapaxes(0, 1).reshape(b, s, h, d), kp, vp


def cross_attn(h, mk, mv, wq, wo):
    q = heads(h @ wq, H_X)
    s = jnp.einsum("bthd,bmhd->bhtm", q, mk).astype(jnp.float32) * (HEAD_DIM ** -0.5)
    p = jax.nn.softmax(s, axis=-1)
    o = jnp.einsum("bhtm,bmhd->bthd", p.astype(mv.dtype), mv)
    return o.reshape(h.shape[0], h.shape[1], D_X) @ wo


def swiglu(h, wg, wu, wd):
    return (jax.nn.silu(h @ wg) * (h @ wu)) @ wd


def moe_swiglu(h, router, wg, wu, wd):
    b, t, d = h.shape
    hf = h.reshape(b * t, d)
    logits = (hf @ router).astype(jnp.float32)
    top_val, top_idx = lax.top_k(logits, TOP_K)
    gate = jax.nn.softmax(top_val, axis=-1)
    combine = jnp.sum(jax.nn.one_hot(top_idx, N_EXPERTS, dtype=jnp.float32) * gate[..., None], axis=1)
    out = jnp.zeros_like(hf)
    for e in range(N_EXPERTS):
        out = out + swiglu(hf, wg[e], wu[e], wd[e]) * combine[:, e:e + 1].astype(hf.dtype)
    return out.reshape(b, t, d)


def setup_inputs(seed: int = 0) -> dict:
    key = jax.random.key(seed)
    keys = jax.random.split(key, 30)
    d = D_MODEL
    band_past = min(BAND_WINDOW, PAST_LEN)

    def nrm(i, shape, scale=1.0):
        return jax.random.normal(keys[i], shape, jnp.float32) * scale

    return {
        "x_prompt": nrm(0, (BATCH, SEQ, d)),
        "x_sample": nrm(1, (DEC_BATCH, DEC_SEQ, d)),
        "cache_sb_k": nrm(2, (DEC_BATCH, PAST_LEN, H_SB, HEAD_DIM)),
        "cache_sb_v": nrm(3, (DEC_BATCH, PAST_LEN, H_SB, HEAD_DIM)),
        "state_ret": nrm(4, (DEC_BATCH, H_RET, HEAD_DIM, HEAD_DIM)),
        "cache_band_k": nrm(5, (DEC_BATCH, band_past, H_BAND, HEAD_DIM)),
        "cache_band_v": nrm(6, (DEC_BATCH, band_past, H_BAND, HEAD_DIM)),
        "cache_mem_k": nrm(7, (DEPTH, DEC_BATCH, N_MEM, H_X, HEAD_DIM)),
        "cache_mem_v": nrm(8, (DEPTH, DEC_BATCH, N_MEM, H_X, HEAD_DIM)),
        "mem_prompt": nrm(9, (BATCH, N_MEM, d)),
        "w_in_ab": nrm(10, (d, D_AB_IN), d ** -0.5),
        "w_out_ab": nrm(11, (D_AB_OUT, d), D_AB_OUT ** -0.5),
        "ret_norm_g": 1.0 + nrm(12, (D_RET,), 0.02),
        "w_qkv_band": nrm(13, (d, 3 * D_BAND), d ** -0.5),
        "w_out_band": nrm(14, (D_BAND, d), D_BAND ** -0.5),
        "rel_bias_band": nrm(15, (H_BAND, 2 * MAX_REL + 1), 0.1),
        "norm_g": 1.0 + nrm(16, (DEPTH, 3, d), 0.02),
        "mem_norm_g": 1.0 + nrm(17, (DEPTH, d), 0.02),
        "w_xq": nrm(18, (DEPTH, d, D_X), d ** -0.5),
        "w_xk": nrm(19, (DEPTH, d, D_X), d ** -0.5),
        "w_xv": nrm(20, (DEPTH, d, D_X), d ** -0.5),
        "w_xo": nrm(21, (DEPTH, D_X, d), D_X ** -0.5),
        "ffn_w_gate": nrm(22, (d, D_FF), d ** -0.5),
        "ffn_w_up": nrm(23, (d, D_FF), d ** -0.5),
        "ffn_w_down": nrm(24, (D_FF, d), D_FF ** -0.5),
        "moe_router": nrm(25, (d, N_EXPERTS), d ** -0.5),
        "moe_w_gate": nrm(26, (N_EXPERTS, d, D_FF_EXPERT), d ** -0.5),
        "moe_w_up": nrm(27, (N_EXPERTS, d, D_FF_EXPERT), d ** -0.5),
        "moe_w_down": nrm(28, (N_EXPERTS, D_FF_EXPERT, d), D_FF_EXPERT ** -0.5),
        "final_norm_g": 1.0 + nrm(29, (d,), 0.02),
    }


def reference(x_prompt, x_sample, cache_sb_k, cache_sb_v, state_ret, cache_band_k, cache_band_v,
              cache_mem_k, cache_mem_v, mem_prompt, w_in_ab, w_out_ab, ret_norm_g, w_qkv_band,
              w_out_band, rel_bias_band, norm_g, mem_norm_g, w_xq, w_xk, w_xv, w_xo,
              ffn_w_gate, ffn_w_up, ffn_w_down, moe_router, moe_w_gate, moe_w_up, moe_w_down,
              final_norm_g):
    b, s, _ = x_prompt.shape
    nb, t, _ = x_sample.shape
    past = cache_sb_k.shape[1]
    band_past = cache_band_k.shape[1]
    log_g = retention_log_decay()
    pos_p = jnp.arange(s, dtype=jnp.int32)
    pos_s = past + jnp.arange(t, dtype=jnp.int32)

    mem_k_list, mem_v_list = [], []
    xp, xs = x_prompt, x_sample
    for l in range(DEPTH):
        mem_n = rmsnorm(mem_prompt, mem_norm_g[l])
        mk_p = heads(mem_n @ w_xk[l], H_X)
        mv_p = heads(mem_n @ w_xv[l], H_X)
        mem_k_list.append(mk_p)
        mem_v_list.append(mv_p)

        hp = rmsnorm(xp, norm_g[l, 0])
        hs = rmsnorm(xs, norm_g[l, 0])
        if l % 2 == 0:
            q, k, v, qr, kr, vr, gr = ab_project(hp, pos_p, w_in_ab)
            o_sb = stick_breaking_prompt(q, k, v)
            o_r, ret_state_prompt = retention_prompt(qr, kr, vr, log_g)
            xp = xp + ab_merge(o_sb, o_r, gr, ret_norm_g, w_out_ab)
            sb_k_prompt, sb_v_prompt = k, v
            q, k, v, qr, kr, vr, gr = ab_project(hs, pos_s, w_in_ab)
            k_all = jnp.concatenate([cache_sb_k, k], axis=1)
            v_all = jnp.concatenate([cache_sb_v, v], axis=1)
            o_sb = stick_breaking_block(q, k_all, v_all, pos_s, jnp.arange(past + t, dtype=jnp.int32))
            o_r, ret_state_sample = retention_chunk(qr, kr, vr, state_ret.astype(jnp.float32), log_g)
            xs = xs + ab_merge(o_sb, o_r, gr, ret_norm_g, w_out_ab)
            sb_k_sample, sb_v_sample = k, v
        else:
            q, k, v = jnp.split(hp @ w_qkv_band, 3, axis=-1)
            o, kpad, vpad = band_prompt(heads(q, H_BAND), heads(k, H_BAND), heads(v, H_BAND), rel_bias_band)
            xp = xp + o.reshape(b, s, D_BAND) @ w_out_band
            band_k_prompt = kpad[:, kpad.shape[1] - band_past:]
            band_v_prompt = vpad[:, vpad.shape[1] - band_past:]
            q, k, v = jnp.split(hs @ w_qkv_band, 3, axis=-1)
            k_all = jnp.concatenate([cache_band_k, heads(k, H_BAND)], axis=1)
            v_all = jnp.concatenate([cache_band_v, heads(v, H_BAND)], axis=1)
            k_pos = past - band_past + jnp.arange(band_past + t, dtype=jnp.int32)
            o = band_block(heads(q, H_BAND), k_all, v_all, pos_s, k_pos,
                           jnp.ones((band_past + t,), dtype=bool), rel_bias_band)
            xs = xs + o.reshape(nb, t, D_BAND) @ w_out_band
            band_k_sample = k_all[:, t:]
            band_v_sample = v_all[:, t:]

        xp = xp + cross_attn(rmsnorm(xp, norm_g[l, 1]), mk_p, mv_p, w_xq[l], w_xo[l])
        xs = xs + cross_attn(rmsnorm(xs, norm_g[l, 1]), cache_mem_k[l], cache_mem_v[l], w_xq[l], w_xo[l])

        hp = rmsnorm(xp, norm_g[l, 2])
        hs = rmsnorm(xs, norm_g[l, 2])
        if l % 2 == 0:
            xp = xp + swiglu(hp, ffn_w_gate, ffn_w_up, ffn_w_down)
            xs = xs + swiglu(hs, ffn_w_gate, ffn_w_up, ffn_w_down)
        else:
            xp = xp + moe_swiglu(hp, moe_router, moe_w_gate, moe_w_up, moe_w_down)
            xs = xs + moe_swiglu(hs, moe_router, moe_w_gate, moe_w_up, moe_w_down)

    y_prompt = rmsnorm(xp, final_norm_g)
    y_sample = rmsnorm(xs, final_norm_g)
    mem_k_prompt = jnp.stack(mem_k_list, axis=0)
    mem_v_prompt = jnp.stack(mem_v_list, axis=0)
    return (y_prompt, y_sample, sb_k_prompt, sb_v_prompt, sb_k_sample, sb_v_sample,
            ret_state_prompt, ret_state_sample, band_k_prompt, band_v_prompt,
            band_k_sample, band_v_sample, mem_k_prompt, mem_v_prompt)
```

```python
import functools

import numpy as np
import jax
import jax.numpy as jnp
from jax import lax
from jax.experimental import pallas as pl
from jax.experimental.pallas import tpu as pltpu

F32 = jnp.float32
BF16 = jnp.bfloat16

HEAD_DIM = 128
CHUNK = 64
N_BAND_CHUNKS = 8
BAND_WINDOW = N_BAND_CHUNKS * CHUNK
MAX_REL = 128
H_X = 4
TOP_K = 2
RMS_EPS = 1e-6
ROPE_BASE = 10000.0
NEG_INF = -1e30
SCALE = HEAD_DIM ** -0.5

VMEM_LIMIT_BYTES = 56 * 1024 * 1024


def _params(*sem):
    return pltpu.CompilerParams(dimension_semantics=sem, vmem_limit_bytes=VMEM_LIMIT_BYTES)


def _tile(n, pref, mult=8):
    t = min(n, pref)
    while t > mult and (n % t or t % mult):
        t -= mult
    assert n % t == 0, (n, pref)
    return t


def _dot(a, b):
    return jnp.dot(a, b, preferred_element_type=F32)


def _dot_nt(a, b):
    return lax.dot_general(a, b, (((1,), (1,)), ((), ())), preferred_element_type=F32)


def _dot_tn(a, b):
    return lax.dot_general(a, b, (((0,), (0,)), ((), ())), preferred_element_type=F32)


def _silu(a):
    return a * (1.0 / (1.0 + jnp.exp(-a)))


def _rmsnorm_kernel(x_ref, g_ref, o_ref):
    x = x_ref[...]
    y = x * lax.rsqrt(jnp.mean(x * x, axis=-1, keepdims=True) + RMS_EPS) * g_ref[...]
    o_ref[...] = y.astype(o_ref.dtype)


def rmsnorm(x, g, out_dtype):
    m, d = x.shape
    tm = _tile(m, 512)
    return pl.pallas_call(
        _rmsnorm_kernel,
        grid=(m // tm,),
        in_specs=[pl.BlockSpec((tm, d), lambda i: (i, 0)),
                  pl.BlockSpec((1, d), lambda i: (0, 0))],
        out_specs=pl.BlockSpec((tm, d), lambda i: (i, 0)),
        out_shape=jax.ShapeDtypeStruct((m, d), out_dtype),
        compiler_params=_params("parallel"),
        name="rmsnorm",
    )(x, g.reshape(1, d).astype(F32))


def _mm_kernel(*refs, n_parts, has_res, nk):
    xs, ws = refs[:n_parts], refs[n_parts:2 * n_parts]
    pos = 2 * n_parts
    res_ref = refs[pos] if has_res else None
    pos += int(has_res)
    o_ref = refs[pos]
    part = _dot(xs[0][...], ws[0][...])
    for x_ref, w_ref in zip(xs[1:], ws[1:]):
        part = part + _dot(x_ref[...], w_ref[...])
    if nk == 1:
        if has_res:
            part = res_ref[...] + part
        o_ref[...] = part.astype(o_ref.dtype)
        return
    acc_ref = refs[pos + 1]
    k = pl.program_id(2)

    @pl.when(k == 0)
    def _():
        acc_ref[...] = part

    @pl.when(k > 0)
    def _():
        acc_ref[...] += part

    @pl.when(k == nk - 1)
    def _():
        out = acc_ref[...]
        if has_res:
            out = res_ref[...] + out
        o_ref[...] = out.astype(o_ref.dtype)


def matmul(parts, w, k_part, *, col_off=0, n_cols=None, out_dtype=F32, residual=None,
           tm_pref=1024, tn_pref=512, tk_pref=2048):
    m = parts[0][0].shape[0]
    n_cols = w.shape[1] - col_off if n_cols is None else n_cols
    tm = _tile(m, tm_pref)
    tn = _tile(n_cols, tn_pref, 128)
    tk = _tile(k_part, tk_pref, 128)
    nk = k_part // tk
    assert col_off % tn == 0
    in_specs, args = [], []
    for x, xo, _ in parts:
        assert xo % tk == 0
        in_specs.append(pl.BlockSpec((tm, tk), lambda i, j, k, xo=xo: (i, xo // tk + k)))
        args.append(x)
    for _, _, wo in parts:
        assert wo % tk == 0
        in_specs.append(pl.BlockSpec((tk, tn), lambda i, j, k, wo=wo: (wo // tk + k, col_off // tn + j)))
        args.append(w)
    if residual is not None:
        in_specs.append(pl.BlockSpec((tm, tn), lambda i, j, k: (i, j)))
        args.append(residual)
    return pl.pallas_call(
        functools.partial(_mm_kernel, n_parts=len(parts), has_res=residual is not None, nk=nk),
        grid=(m // tm, n_cols // tn, nk),
        in_specs=in_specs,
        out_specs=pl.BlockSpec((tm, tn), lambda i, j, k: (i, j)),
        out_shape=jax.ShapeDtypeStruct((m, n_cols), out_dtype),
        scratch_shapes=[pltpu.VMEM((tm, tn), F32)] if nk > 1 else [],
        compiler_params=_params("parallel", "parallel", "arbitrary"),
        name="matmul",
    )(*args)


def _swiglu_up_kernel(*refs, has_comb):
    if has_comb:
        x_ref, wg_ref, wu_ref, c_ref, o_ref = refs
    else:
        x_ref, wg_ref, wu_ref, o_ref = refs
    x = x_ref[...]
    act = _silu(_dot(x, wg_ref[...])) * _dot(x, wu_ref[...])
    if has_comb:
        c = c_ref[...]
        lane = lax.broadcasted_iota(jnp.int32, c.shape, 1)
        act = act * jnp.sum(jnp.where(lane == pl.program_id(1), c, 0.0), axis=1, keepdims=True)
    o_ref[...] = act.astype(o_ref.dtype)


def swiglu_up(x, wg, wu, combine=None, *, tn_pref):
    m, d = x.shape
    n_e, _, f = wg.shape
    tm = _tile(m, 1024)
    tn = _tile(f, tn_pref, 128)
    nj = f // tn
    in_specs = [pl.BlockSpec((tm, d), lambda i, e, j: (i, 0)),
                pl.BlockSpec((None, d, tn), lambda i, e, j: (e, 0, j)),
                pl.BlockSpec((None, d, tn), lambda i, e, j: (e, 0, j))]
    args = [x, wg, wu]
    if combine is not None:
        in_specs.append(pl.BlockSpec((tm, combine.shape[1]), lambda i, e, j: (i, 0)))
        args.append(combine)
    return pl.pallas_call(
        functools.partial(_swiglu_up_kernel, has_comb=combine is not None),
        grid=(m // tm, n_e, nj),
        in_specs=in_specs,
        out_specs=pl.BlockSpec((tm, tn), lambda i, e, j: (i, e * nj + j)),
        out_shape=jax.ShapeDtypeStruct((m, n_e * f), BF16),
        compiler_params=_params("parallel", "parallel", "parallel"),
        name="swiglu_up",
    )(*args)


def _router_kernel(x_ref, g_ref, r_ref, o_ref, *, n_experts):
    x = x_ref[...]
    hn = x * lax.rsqrt(jnp.mean(x * x, axis=-1, keepdims=True) + RMS_EPS) * g_ref[...]
    logits = jnp.dot(hn, r_ref[...], preferred_element_type=F32, precision=lax.Precision.HIGHEST)
    lane = lax.broadcasted_iota(jnp.int32, logits.shape, 1)
    n_lanes = logits.shape[1]
    lg = jnp.where(lane < n_experts, logits, -jnp.inf)
    m1 = jnp.max(lg, axis=1, keepdims=True)
    i1 = jnp.min(jnp.where(lg == m1, lane, n_lanes), axis=1, keepdims=True)
    lg2 = jnp.where(lane == i1, -jnp.inf, lg)
    m2 = jnp.max(lg2, axis=1, keepdims=True)
    i2 = jnp.min(jnp.where(lg2 == m2, lane, n_lanes), axis=1, keepdims=True)
    e2 = jnp.exp(m2 - m1)
    inv = 1.0 / (1.0 + e2)
    o_ref[...] = jnp.where(lane == i1, inv, 0.0) + jnp.where(lane == i2, e2 * inv, 0.0)


def moe_combine(x, g, router):
    m, d = x.shape
    n_e = router.shape[1]
    tm = _tile(m, 512)
    r_pad = jnp.zeros((d, 128), F32).at[:, :n_e].set(router.astype(F32))
    out = pl.pallas_call(
        functools.partial(_router_kernel, n_experts=n_e),
        grid=(m // tm,),
        in_specs=[pl.BlockSpec((tm, d), lambda i: (i, 0)),
                  pl.BlockSpec((1, d), lambda i: (0, 0)),
                  pl.BlockSpec((d, 128), lambda i: (0, 0))],
        out_specs=pl.BlockSpec((tm, 128), lambda i: (i, 0)),
        out_shape=jax.ShapeDtypeStruct((m, 128), F32),
        compiler_params=_params("parallel"),
        name="moe_router",
    )(x, g.reshape(1, d).astype(F32), r_pad)
    return out[:, :n_e]


def _cumsum_matrix(tk):
    r = np.arange(2 * tk)[:, None] % tk
    c = np.arange(2 * tk)[None, :]
    return jnp.asarray(np.where(c < tk, r > c, True).astype(np.float32), dtype=BF16)


def _sb_block(q, kblk, vblk, before, u, acc_ref, carry_ref):
    tk = kblk.shape[0]
    z = _dot_nt(q, kblk)
    l1p = jnp.log(1.0 + jnp.exp(-jnp.abs(z)))
    log_1m_beta = -(jnp.maximum(z, 0.0) + l1p)
    if before is not None:
        log_1m_beta = jnp.where(before, log_1m_beta, 0.0)
    log_beta = jnp.minimum(z, 0.0) - l1p
    hi = log_1m_beta.astype(BF16)
    lo = (log_1m_beta - hi.astype(F32)).astype(BF16)
    sums = _dot(jnp.concatenate([hi, lo], axis=1), u)
    w = jnp.exp(log_beta + sums[:, :tk] + carry_ref[...])
    if before is not None:
        w = jnp.where(before, w, 0.0)
    acc_ref[...] += _dot(w.astype(BF16), vblk)
    carry_ref[...] += sums[:, tk:]


def _sb_prompt_kernel(q_ref, k_ref, v_ref, u_ref, o_ref, kb_ref, vb_ref, acc_ref, carry_ref, *, tq, tk):
    i = pl.program_id(1)

    @pl.when(i == 0)
    def _():
        kb_ref[...] = k_ref[...].astype(BF16)
        vb_ref[...] = v_ref[...].astype(BF16)

    q = (q_ref[...] * SCALE).astype(BF16)
    acc_ref[...] = jnp.zeros_like(acc_ref)
    carry_ref[...] = jnp.zeros_like(carry_ref)
    u = u_ref[...]
    row = lax.broadcasted_iota(jnp.int32, (tq, tk), 0) + i * tq
    lane = lax.broadcasted_iota(jnp.int32, (tq, tk), 1)
    n_blocks = (i + 1) * (tq // tk)

    def body(s, _):
        off = pl.multiple_of((n_blocks - 1 - s) * tk, tk)
        _sb_block(q, kb_ref[pl.ds(off, tk), :], vb_ref[pl.ds(off, tk), :], lane + off < row, u,
                  acc_ref, carry_ref)
        return 0

    lax.fori_loop(0, n_blocks, body, 0)
    o_ref[...] = acc_ref[...].astype(o_ref.dtype)


def sb_prompt(p, s, n_heads, q_col, k_col, v_col):
    tq = _tile(s, 256)
    tk = HEAD_DIM
    return pl.pallas_call(
        functools.partial(_sb_prompt_kernel, tq=tq, tk=tk),
        grid=(n_heads, s // tq),
        in_specs=[pl.BlockSpec((tq, HEAD_DIM), lambda h, i: (i, q_col + h)),
                  pl.BlockSpec((s, HEAD_DIM), lambda h, i: (0, k_col + h)),
                  pl.BlockSpec((s, HEAD_DIM), lambda h, i: (0, v_col + h)),
                  pl.BlockSpec((2 * tk, 2 * tk), lambda h, i: (0, 0))],
        out_specs=pl.BlockSpec((tq, HEAD_DIM), lambda h, i: (i, h)),
        out_shape=jax.ShapeDtypeStruct((s, n_heads * HEAD_DIM), BF16),
        scratch_shapes=[pltpu.VMEM((s, HEAD_DIM), BF16), pltpu.VMEM((s, HEAD_DIM), BF16),
                        pltpu.VMEM((tq, HEAD_DIM), F32), pltpu.VMEM((tq, tk), F32)],
        compiler_params=_params("parallel", "arbitrary"),
        name="sb_prompt",
    )(p, p, p, _cumsum_matrix(tk))


def _sb_sample_kernel(q_ref, kn_ref, vn_ref, kc_ref, vc_ref, u_ref, o_ref, acc_ref, carry_ref, *, t, tk, past):
    q = (q_ref[...] * SCALE).astype(BF16)
    acc_ref[...] = jnp.zeros_like(acc_ref)
    carry_ref[...] = jnp.zeros_like(carry_ref)
    u = u_ref[...]
    reps = tk // t
    kn = jnp.concatenate([kn_ref[...].astype(BF16)] * reps, axis=0)
    vn = jnp.concatenate([vn_ref[...].astype(BF16)] * reps, axis=0)
    row = lax.broadcasted_iota(jnp.int32, (t, tk), 0)
    lane = lax.broadcasted_iota(jnp.int32, (t, tk), 1)
    _sb_block(q, kn, vn, lane < row, u, acc_ref, carry_ref)
    n_blocks = past // tk

    def body(s, _):
        off = pl.multiple_of((n_blocks - 1 - s) * tk, tk)
        _sb_block(q, kc_ref[pl.ds(off, tk), :].astype(BF16), vc_ref[pl.ds(off, tk), :].astype(BF16), None, u,
                  acc_ref, carry_ref)
        return 0

    lax.fori_loop(0, n_blocks, body, 0)
    o_ref[...] = acc_ref[...].astype(o_ref.dtype)


def sb_sample(p, cache_k, cache_v, row0, nb, t, n_heads, q_col, k_col, v_col):
    past = cache_k.shape[1]
    tk = HEAD_DIM
    assert tk % t == 0 and past % tk == 0 and row0 % t == 0
    r0 = row0 // t
    ck = cache_k.reshape(nb, past, n_heads * HEAD_DIM)
    cv = cache_v.reshape(nb, past, n_heads * HEAD_DIM)
    return pl.pallas_call(
        functools.partial(_sb_sample_kernel, t=t, tk=tk, past=past),
        grid=(nb, n_heads),
        in_specs=[pl.BlockSpec((t, HEAD_DIM), lambda b, h: (r0 + b, q_col + h)),
                  pl.BlockSpec((t, HEAD_DIM), lambda b, h: (r0 + b, k_col + h)),
                  pl.BlockSpec((t, HEAD_DIM), lambda b, h: (r0 + b, v_col + h)),
                  pl.BlockSpec((None, past, HEAD_DIM), lambda b, h: (b, 0, h)),
                  pl.BlockSpec((None, past, HEAD_DIM), lambda b, h: (b, 0, h)),
                  pl.BlockSpec((2 * tk, 2 * tk), lambda b, h: (0, 0))],
        out_specs=pl.BlockSpec((t, HEAD_DIM), lambda b, h: (b, h)),
        out_shape=jax.ShapeDtypeStruct((nb * t, n_heads * HEAD_DIM), BF16),
        scratch_shapes=[pltpu.VMEM((t, HEAD_DIM), F32), pltpu.VMEM((t, tk), F32)],
        compiler_params=_params("parallel", "parallel"),
        name="sb_sample",
    )(p, p, p, ck, cv, _cumsum_matrix(tk))


def _retention_log_decay(n_heads):
    return [float(np.log1p(-np.float32(2.0 ** (-5.0 - h)))) for h in range(n_heads)]


def _retention_kernel(q_ref, k_ref, v_ref, g_ref, cos_ref, sin_ref, gn_ref, s0_ref, o_ref, sout_ref,
                      state_ref, dec_ref, *, t, n_chunks, log_g):
    c = pl.program_id(1)

    @pl.when(c == 0)
    def _():
        state_ref[...] = s0_ref[...]
        rel = (lax.broadcasted_iota(jnp.int32, (t, t), 0) - lax.broadcasted_iota(jnp.int32, (t, t), 1)).astype(F32)
        for h, lg in enumerate(log_g):
            dec_ref[h] = jnp.where(rel >= 0, jnp.exp(lg * jnp.maximum(rel, 0.0)), 0.0)

    cos, sin = cos_ref[...], sin_ref[...]
    idx = lax.broadcasted_iota(jnp.int32, (t, HEAD_DIM), 0).astype(F32)
    for h, lg in enumerate(log_g):
        sl = slice(h * HEAD_DIM, (h + 1) * HEAD_DIM)
        qh, kh = q_ref[:, sl], k_ref[:, sl]
        qh = qh * cos + pltpu.roll(qh, HEAD_DIM // 2, 1) * sin
        kh = (kh * cos + pltpu.roll(kh, HEAD_DIM // 2, 1) * sin) * SCALE
        qb, vb = qh.astype(BF16), v_ref[:, sl].astype(BF16)
        scores = _dot_nt(qb, kh.astype(BF16)) * dec_ref[h]
        state = state_ref[h]
        o = _dot(scores.astype(BF16), vb) + _dot(qb, state.astype(BF16)) * jnp.exp((idx + 1.0) * lg)
        k_dec = (kh * jnp.exp((t - 1.0 - idx) * lg)).astype(BF16)
        state_ref[h] = float(np.exp(np.float32(t * lg))) * state + _dot_tn(k_dec, vb)
        o = o * lax.rsqrt(jnp.mean(o * o, axis=-1, keepdims=True) + RMS_EPS) * gn_ref[:, sl]
        o_ref[:, sl] = (o * _silu(g_ref[:, sl])).astype(o_ref.dtype)

    @pl.when(c == n_chunks - 1)
    def _():
        sout_ref[...] = state_ref[...]


def retention(p, cos2, sin2, ret_norm_g, state0, row0, nb, seq, t, n_heads, q_col, k_col, v_col, g_col):
    w = n_heads * HEAD_DIM
    n_chunks = seq // t
    assert row0 % t == 0 and seq % t == 0
    r0 = row0 // t

    def rows(col):
        return pl.BlockSpec((t, w), lambda b, c: (r0 + b * n_chunks + c, col))

    return pl.pallas_call(
        functools.partial(_retention_kernel, t=t, n_chunks=n_chunks, log_g=_retention_log_decay(n_heads)),
        grid=(nb, n_chunks),
        in_specs=[rows(q_col), rows(k_col), rows(v_col), rows(g_col),
                  pl.BlockSpec((t, HEAD_DIM), lambda b, c: (c, 0)),
                  pl.BlockSpec((t, HEAD_DIM), lambda b, c: (c, 0)),
                  pl.BlockSpec((1, w), lambda b, c: (0, 0)),
                  pl.BlockSpec((None, n_heads, HEAD_DIM, HEAD_DIM), lambda b, c: (b, 0, 0, 0))],
        out_specs=[pl.BlockSpec((t, w), lambda b, c: (b * n_chunks + c, 0)),
                   pl.BlockSpec((None, n_heads, HEAD_DIM, HEAD_DIM), lambda b, c: (b, 0, 0, 0))],
        out_shape=[jax.ShapeDtypeStruct((nb * seq, w), BF16),
                   jax.ShapeDtypeStruct((nb, n_heads, HEAD_DIM, HEAD_DIM), F32)],
        scratch_shapes=[pltpu.VMEM((n_heads, HEAD_DIM, HEAD_DIM), F32), pltpu.VMEM((n_heads, t, t), F32)],
        compiler_params=_params("parallel", "arbitrary"),
        name="retention",
    )(p, p, p, p, cos2, sin2, ret_norm_g.reshape(1, w).astype(F32), state0)


def _rope_tables(pos):
    half = HEAD_DIM // 2
    inv = ROPE_BASE ** (-jnp.arange(half, dtype=F32) / half)
    ang = pos.astype(F32)[:, None] * inv[None, :]
    cos, sin = jnp.cos(ang), jnp.sin(ang)
    return jnp.concatenate([cos, cos], axis=1), jnp.concatenate([-sin, sin], axis=1)


def _band_bias_table(rel_bias, tq):
    t = np.arange(tq)[:, None]
    j = np.arange(BAND_WINDOW + tq)[None, :]
    rel = np.clip(t + BAND_WINDOW - j, -MAX_REL, MAX_REL) + MAX_REL
    lo = (t // CHUNK) * CHUNK
    in_band = (j >= lo) & (j < lo + BAND_WINDOW + CHUNK)
    return jnp.where(jnp.asarray(in_band)[None], rel_bias.astype(F32)[:, rel], NEG_INF)


def _softmax_pv(scores, values):
    m = scores[0].max(axis=1, keepdims=True)
    for s in scores[1:]:
        m = jnp.maximum(m, s.max(axis=1, keepdims=True))
    ps = [jnp.exp(s - m) for s in scores]
    denom = ps[0].sum(axis=1, keepdims=True)
    for p in ps[1:]:
        denom = denom + p.sum(axis=1, keepdims=True)
    o = _dot(ps[0].astype(BF16), values[0])
    for p, v in zip(ps[1:], values[1:]):
        o = o + _dot(p.astype(BF16), v)
    return o * (1.0 / denom)


def _band_prompt_kernel(*refs, tq, n_kb):
    q_ref = refs[0]
    k_refs, v_refs = refs[1:1 + n_kb], refs[1 + n_kb:1 + 2 * n_kb]
    b_ref, o_ref = refs[1 + 2 * n_kb], refs[2 + 2 * n_kb]
    i = pl.program_id(1)
    q = (q_ref[...] * SCALE).astype(BF16)
    scores, values = [], []
    for kb in range(n_kb):
        s = _dot_nt(q, k_refs[kb][...].astype(BF16)) + b_ref[:, kb * tq:(kb + 1) * tq]
        scores.append(s + jnp.where(i - (n_kb - 1) + kb >= 0, 0.0, NEG_INF))
        values.append(v_refs[kb][...].astype(BF16))
    o_ref[...] = _softmax_pv(scores, values).astype(o_ref.dtype)


def band_prompt(p, rel_bias, s, n_heads, q_col, k_col, v_col):
    tq = _tile(s, 256)
    assert BAND_WINDOW % tq == 0 and tq % CHUNK == 0
    n_kb = BAND_WINDOW // tq + 1

    def kv_specs(col):
        return [pl.BlockSpec((tq, HEAD_DIM),
                             lambda h, i, kb=kb: (jnp.maximum(i - (n_kb - 1) + kb, 0), col + h))
                for kb in range(n_kb)]

    return pl.pallas_call(
        functools.partial(_band_prompt_kernel, tq=tq, n_kb=n_kb),
        grid=(n_heads, s // tq),
        in_specs=([pl.BlockSpec((tq, HEAD_DIM), lambda h, i: (i, q_col + h))] + kv_specs(k_col) + kv_specs(v_col)
                  + [pl.BlockSpec((None, tq, BAND_WINDOW + tq), lambda h, i: (h, 0, 0))]),
        out_specs=pl.BlockSpec((tq, HEAD_DIM), lambda h, i: (i, h)),
        out_shape=jax.ShapeDtypeStruct((s, n_heads * HEAD_DIM), BF16),
        compiler_params=_params("parallel", "parallel"),
        name="band_prompt",
    )(*([p] * (1 + 2 * n_kb)), _band_bias_table(rel_bias, tq))


def _band_sample_kernel(q_ref, kn_ref, vn_ref, kc_ref, vc_ref, b_ref, o_ref, *, band_past):
    q = (q_ref[...] * SCALE).astype(BF16)
    s_c = _dot_nt(q, kc_ref[...].astype(BF16)) + b_ref[:, :band_past]
    s_n = _dot_nt(q, kn_ref[...].astype(BF16)) + b_ref[:, band_past:]
    o = _softmax_pv([s_c, s_n], [vc_ref[...].astype(BF16), vn_ref[...].astype(BF16)])
    o_ref[...] = o.astype(o_ref.dtype)


def band_sample(p, cache_k, cache_v, rel_bias, row0, nb, t, n_heads, q_col, k_col, v_col):
    band_past = cache_k.shape[1]
    assert t == CHUNK and band_past == BAND_WINDOW and row0 % t == 0
    r0 = row0 // t
    ck = cache_k.reshape(nb, band_past, n_heads * HEAD_DIM)
    cv = cache_v.reshape(nb, band_past, n_heads * HEAD_DIM)
    return pl.pallas_call(
        functools.partial(_band_sample_kernel, band_past=band_past),
        grid=(nb, n_heads),
        in_specs=[pl.BlockSpec((t, HEAD_DIM), lambda b, h: (r0 + b, q_col + h)),
                  pl.BlockSpec((t, HEAD_DIM), lambda b, h: (r0 + b, k_col + h)),
                  pl.BlockSpec((t, HEAD_DIM), lambda b, h: (r0 + b, v_col + h)),
                  pl.BlockSpec((None, band_past, HEAD_DIM), lambda b, h: (b, 0, h)),
                  pl.BlockSpec((None, band_past, HEAD_DIM), lambda b, h: (b, 0, h)),
                  pl.BlockSpec((None, t, band_past + t), lambda b, h: (h, 0, 0))],
        out_specs=pl.BlockSpec((t, HEAD_DIM), lambda b, h: (b, h)),
        out_shape=jax.ShapeDtypeStruct((nb * t, n_heads * HEAD_DIM), BF16),
        compiler_params=_params("parallel", "parallel"),
        name="band_sample",
    )(p, p, p, ck, cv, _band_bias_table(rel_bias, t))


def _cross_attn_kernel(q_ref, mk_ref, mv_ref, o_ref, *, n_heads):
    for h in range(n_heads):
        sl = slice(h * HEAD_DIM, (h + 1) * HEAD_DIM)
        q = (q_ref[:, sl] * SCALE).astype(BF16)
        s = _dot_nt(q, mk_ref[:, sl].astype(BF16))
        o_ref[:, sl] = _softmax_pv([s], [mv_ref[:, sl].astype(BF16)]).astype(o_ref.dtype)


def cross_attn(q, mk, mv, row0, nb, seq):
    n_mem, w = mk.shape[1], mk.shape[2]
    tq = _tile(seq, 512)
    n_t = seq // tq
    assert row0 % tq == 0
    r0 = row0 // tq
    return pl.pallas_call(
        functools.partial(_cross_attn_kernel, n_heads=w // HEAD_DIM),
        grid=(nb, n_t),
        in_specs=[pl.BlockSpec((tq, w), lambda b, i: (r0 + b * n_t + i, 0)),
                  pl.BlockSpec((None, n_mem, w), lambda b, i: (b, 0, 0)),
                  pl.BlockSpec((None, n_mem, w), lambda b, i: (b, 0, 0))],
        out_specs=pl.BlockSpec((tq, w), lambda b, i: (b * n_t + i, 0)),
        out_shape=jax.ShapeDtypeStruct((nb * seq, w), BF16),
        compiler_params=_params("parallel", "parallel"),
        name="cross_attn",
    )(q, mk, mv)


def kernel(x_prompt, x_sample, cache_sb_k, cache_sb_v, state_ret, cache_band_k, cache_band_v, cache_mem_k, cache_mem_v, mem_prompt, w_in_ab, w_out_ab, ret_norm_g, w_qkv_band, w_out_band, rel_bias_band, norm_g, mem_norm_g, w_xq, w_xk, w_xv, w_xo, ffn_w_gate, ffn_w_up, ffn_w_down, moe_router, moe_w_gate, moe_w_up, moe_w_down, final_norm_g):
    bp, s, d = x_prompt.shape
    nb, t, _ = x_sample.shape
    assert bp == 1
    past = cache_sb_k.shape[1]
    band_past = cache_band_k.shape[1]
    h_sb = cache_sb_k.shape[2]
    h_ret = state_ret.shape[1]
    h_band = cache_band_k.shape[2]
    d_sb, d_ret = h_sb * HEAD_DIM, h_ret * HEAD_DIM
    n_mem = mem_prompt.shape[1]
    d_x = w_xq.shape[2]
    depth = w_xq.shape[0]
    n_exp, _, d_ffe = moe_w_gate.shape
    d_ff = ffn_w_gate.shape[1]
    ms = nb * t
    assert d_sb == d_ret and 3 * d_sb + 4 * d_ret == w_in_ab.shape[1]

    x = jnp.concatenate([x_prompt.reshape(s, d), x_sample.reshape(ms, d)], axis=0)
    mem = mem_prompt.reshape(n_mem, d)
    bf = lambda a: a.astype(BF16)

    cos_p, sin_p = _rope_tables(jnp.arange(s, dtype=jnp.int32))
    cos_s, sin_s = _rope_tables(past + jnp.arange(t, dtype=jnp.int32))

    mem_k_list, mem_v_list = [], []
    for l in range(depth):
        mem_n = rmsnorm(mem, mem_norm_g[l], BF16)
        mk_p = matmul([(mem_n, 0, 0)], bf(w_xk[l]), d)
        mv_p = matmul([(mem_n, 0, 0)], bf(w_xv[l]), d)
        mem_k_list.append(mk_p.reshape(1, n_mem, H_X, HEAD_DIM))
        mem_v_list.append(mv_p.reshape(1, n_mem, H_X, HEAD_DIM))

        h = rmsnorm(x, norm_g[l, 0], BF16)
        if l % 2 == 0:
            p = matmul([(h, 0, 0)], bf(w_in_ab), d)
            o_sb_p = sb_prompt(p, s, h_sb, 0, h_sb, 2 * h_sb)
            o_sb_s = sb_sample(p, cache_sb_k, cache_sb_v, s, nb, t, h_sb, 0, h_sb, 2 * h_sb)
            t_ret = _tile(s, 256, CHUNK)
            o_r_p, ret_state_prompt = retention(p, cos_p, sin_p, ret_norm_g, jnp.zeros((1,) + state_ret.shape[1:], F32),
                                                0, 1, s, t_ret, h_ret, 3, 4, 5, 6)
            o_r_s, ret_state_sample = retention(p, cos_s, sin_s, ret_norm_g, state_ret.astype(F32),
                                                s, nb, t, t, h_ret, 3, 4, 5, 6)
            o_sb = jnp.concatenate([o_sb_p, o_sb_s], axis=0)
            o_r = jnp.concatenate([o_r_p, o_r_s], axis=0)
            x = matmul([(o_sb, 0, 0), (o_r, 0, d_sb)], bf(w_out_ab), d_sb, residual=x)
            sb_k_prompt = p[:s, d_sb:2 * d_sb].reshape(1, s, h_sb, HEAD_DIM)
            sb_v_prompt = p[:s, 2 * d_sb:3 * d_sb].reshape(1, s, h_sb, HEAD_DIM)
            sb_k_sample = p[s:, d_sb:2 * d_sb].reshape(nb, t, h_sb, HEAD_DIM)
            sb_v_sample = p[s:, 2 * d_sb:3 * d_sb].reshape(nb, t, h_sb, HEAD_DIM)
        else:
            d_band = h_band * HEAD_DIM
            p = matmul([(h, 0, 0)], bf(w_qkv_band), d)
            o_p = band_prompt(p, rel_bias_band, s, h_band, 0, h_band, 2 * h_band)
            o_s = band_sample(p, cache_band_k, cache_band_v, rel_bias_band, s, nb, t, h_band, 0, h_band, 2 * h_band)
            o = jnp.concatenate([o_p, o_s], axis=0)
            x = matmul([(o, 0, 0)], bf(w_out_band), d_band, residual=x)
            k_new = p[:, d_band:2 * d_band]
            v_new = p[:, 2 * d_band:]
            band_k_prompt = k_new[s - band_past:s].reshape(1, band_past, h_band, HEAD_DIM)
            band_v_prompt = v_new[s - band_past:s].reshape(1, band_past, h_band, HEAD_DIM)
            band_k_sample = jnp.concatenate([cache_band_k[:, t:], k_new[s:].reshape(nb, t, h_band, HEAD_DIM)], axis=1)
            band_v_sample = jnp.concatenate([cache_band_v[:, t:], v_new[s:].reshape(nb, t, h_band, HEAD_DIM)], axis=1)

        h = rmsnorm(x, norm_g[l, 1], BF16)
        q = matmul([(h, 0, 0)], bf(w_xq[l]), d)
        o_p = cross_attn(q, mk_p.reshape(1, n_mem, d_x), mv_p.reshape(1, n_mem, d_x), 0, 1, s)
        o_s = cross_attn(q, cache_mem_k[l].reshape(nb, n_mem, d_x), cache_mem_v[l].reshape(nb, n_mem, d_x), s, nb, t)
        o = jnp.concatenate([o_p, o_s], axis=0)
        x = matmul([(o, 0, 0)], bf(w_xo[l]), d_x, residual=x)

        h = rmsnorm(x, norm_g[l, 2], BF16)
        if l % 2 == 0:
            act = swiglu_up(h, bf(ffn_w_gate)[None], bf(ffn_w_up)[None], tn_pref=512)
            x = matmul([(act, 0, 0)], bf(ffn_w_down), d_ff, residual=x, tk_pref=2816)
        else:
            combine = moe_combine(x, norm_g[l, 2], moe_router)
            act = swiglu_up(h, bf(moe_w_gate), bf(moe_w_up), combine, tn_pref=256)
            x = matmul([(act, 0, 0)], bf(moe_w_down).reshape(n_exp * d_ffe, d), n_exp * d_ffe, residual=x,
                       tk_pref=2816)

    y = rmsnorm(x, final_norm_g, F32)
    y_prompt = y[:s].reshape(1, s, d)
    y_sample = y[s:].reshape(nb, t, d)
    mem_k_prompt = jnp.stack(mem_k_list, axis=0)
    mem_v_prompt = jnp.stack(mem_v_list, axis=0)
    return (y_prompt, y_sample, sb_k_prompt, sb_v_prompt, sb_k_sample, sb_v_sample,
            ret_state_prompt, ret_state_sample, band_k_prompt, band_v_prompt,
            band_k_sample, band_v_sample, mem_k_prompt, mem_v_prompt)
```

```python
import functools

import numpy as np
import jax
import jax.numpy as jnp
from jax import lax
from jax.experimental import pallas as pl
from jax.experimental.pallas import tpu as pltpu

F32 = jnp.float32
BF16 = jnp.bfloat16

HEAD_DIM = 128
CHUNK = 64
N_BAND_CHUNKS = 8
BAND_WINDOW = N_BAND_CHUNKS * CHUNK
MAX_REL = 128
H_X = 4
TOP_K = 2
RMS_EPS = 1e-6
ROPE_BASE = 10000.0
NEG_INF = -1e30
SCALE = HEAD_DIM ** -0.5
LOG2E = 1.4426950408889634

VMEM_LIMIT_BYTES = 56 * 1024 * 1024


def _params(*sem):
    return pltpu.CompilerParams(dimension_semantics=sem, vmem_limit_bytes=VMEM_LIMIT_BYTES)


def _tile(n, pref, mult=8):
    t = min(n, pref)
    while t > mult and (n % t or t % mult):
        t -= mult
    assert n % t == 0, (n, pref)
    return t


def _dot(a, b):
    return jnp.dot(a, b, preferred_element_type=F32)


def _dot_nt(a, b):
    return lax.dot_general(a, b, (((1,), (1,)), ((), ())), preferred_element_type=F32)


def _dot_tn(a, b):
    return lax.dot_general(a, b, (((0,), (0,)), ((), ())), preferred_element_type=F32)


def _silu(a):
    return a * (1.0 / (1.0 + jnp.exp(-a)))


def _rows(parts):
    return parts[0] if len(parts) == 1 else jnp.concatenate(parts, axis=0)


def _rmsnorm_kernel(x_ref, g_ref, o_ref):
    x = x_ref[...]
    y = x * lax.rsqrt(jnp.mean(x * x, axis=-1, keepdims=True) + RMS_EPS) * g_ref[...]
    o_ref[...] = y.astype(o_ref.dtype)


def rmsnorm(x, g, out_dtype):
    m, d = x.shape
    tm = _tile(m, 512)
    return pl.pallas_call(
        _rmsnorm_kernel,
        grid=(m // tm,),
        in_specs=[pl.BlockSpec((tm, d), lambda i: (i, 0)),
                  pl.BlockSpec((1, d), lambda i: (0, 0))],
        out_specs=pl.BlockSpec((tm, d), lambda i: (i, 0)),
        out_shape=jax.ShapeDtypeStruct((m, d), out_dtype),
        compiler_params=_params("parallel"),
        name="rmsnorm",
    )(x, g.reshape(1, d).astype(F32))


def _mm_kernel(*refs, n_parts, has_res, nk):
    xs, ws = refs[:n_parts], refs[n_parts:2 * n_parts]
    pos = 2 * n_parts
    res_ref = refs[pos] if has_res else None
    pos += int(has_res)
    o_ref = refs[pos]
    part = _dot(xs[0][...], ws[0][...])
    for x_ref, w_ref in zip(xs[1:], ws[1:]):
        part = part + _dot(x_ref[...], w_ref[...])
    if nk == 1:
        if has_res:
            part = res_ref[...] + part
        o_ref[...] = part.astype(o_ref.dtype)
        return
    acc_ref = refs[pos + 1]
    k = pl.program_id(2)

    @pl.when(k == 0)
    def _():
        acc_ref[...] = part

    @pl.when(k > 0)
    def _():
        acc_ref[...] += part

    @pl.when(k == nk - 1)
    def _():
        out = acc_ref[...]
        if has_res:
            out = res_ref[...] + out
        o_ref[...] = out.astype(o_ref.dtype)


def matmul(parts, w, k_part, *, col_off=0, n_cols=None, out_dtype=F32, residual=None,
           tm_pref=1024, tn_pref=512, tk_pref=2048):
    m = parts[0][0].shape[0]
    n_cols = w.shape[1] - col_off if n_cols is None else n_cols
    tm = _tile(m, tm_pref)
    tn = _tile(n_cols, tn_pref, 128)
    tk = _tile(k_part, tk_pref, 128)
    nk = k_part // tk
    assert col_off % tn == 0
    in_specs, args = [], []
    for x, xo, _ in parts:
        assert xo % tk == 0
        in_specs.append(pl.BlockSpec((tm, tk), lambda i, j, k, xo=xo: (i, xo // tk + k)))
        args.append(x)
    for _, _, wo in parts:
        assert wo % tk == 0
        in_specs.append(pl.BlockSpec((tk, tn), lambda i, j, k, wo=wo: (wo // tk + k, col_off // tn + j)))
        args.append(w)
    if residual is not None:
        in_specs.append(pl.BlockSpec((tm, tn), lambda i, j, k: (i, j)))
        args.append(residual)
    return pl.pallas_call(
        functools.partial(_mm_kernel, n_parts=len(parts), has_res=residual is not None, nk=nk),
        grid=(m // tm, n_cols // tn, nk),
        in_specs=in_specs,
        out_specs=pl.BlockSpec((tm, tn), lambda i, j, k: (i, j)),
        out_shape=jax.ShapeDtypeStruct((m, n_cols), out_dtype),
        scratch_shapes=[pltpu.VMEM((tm, tn), F32)] if nk > 1 else [],
        compiler_params=_params("parallel", "parallel", "arbitrary"),
        name="matmul",
    )(*args)


def _swiglu_up_kernel(*refs, has_comb):
    if has_comb:
        x_ref, wg_ref, wu_ref, c_ref, o_ref = refs
    else:
        x_ref, wg_ref, wu_ref, o_ref = refs
    x = x_ref[...]
    act = _silu(_dot(x, wg_ref[...])) * _dot(x, wu_ref[...])
    if has_comb:
        c = c_ref[...]
        lane = lax.broadcasted_iota(jnp.int32, c.shape, 1)
        act = act * jnp.sum(jnp.where(lane == pl.program_id(1), c, 0.0), axis=1, keepdims=True)
    o_ref[...] = act.astype(o_ref.dtype)


def swiglu_up(x, wg, wu, combine=None, *, tn_pref):
    m, d = x.shape
    n_e, _, f = wg.shape
    tm = _tile(m, 1024)
    tn = _tile(f, tn_pref, 128)
    nj = f // tn
    in_specs = [pl.BlockSpec((tm, d), lambda i, e, j: (i, 0)),
                pl.BlockSpec((None, d, tn), lambda i, e, j: (e, 0, j)),
                pl.BlockSpec((None, d, tn), lambda i, e, j: (e, 0, j))]
    args = [x, wg, wu]
    if combine is not None:
        in_specs.append(pl.BlockSpec((tm, combine.shape[1]), lambda i, e, j: (i, 0)))
        args.append(combine)
    return pl.pallas_call(
        functools.partial(_swiglu_up_kernel, has_comb=combine is not None),
        grid=(m // tm, n_e, nj),
        in_specs=in_specs,
        out_specs=pl.BlockSpec((tm, tn), lambda i, e, j: (i, e * nj + j)),
        out_shape=jax.ShapeDtypeStruct((m, n_e * f), BF16),
        compiler_params=_params("parallel", "parallel", "parallel"),
        name="swiglu_up",
    )(*args)


def _router_kernel(x_ref, g_ref, r_ref, o_ref, *, n_experts):
    x = x_ref[...]
    hn = x * lax.rsqrt(jnp.mean(x * x, axis=-1, keepdims=True) + RMS_EPS) * g_ref[...]
    logits = jnp.dot(hn, r_ref[...], preferred_element_type=F32, precision=lax.Precision.HIGHEST)
    lane = lax.broadcasted_iota(jnp.int32, logits.shape, 1)
    n_lanes = logits.shape[1]
    lg = jnp.where(lane < n_experts, logits, -jnp.inf)
    m1 = jnp.max(lg, axis=1, keepdims=True)
    i1 = jnp.min(jnp.where(lg == m1, lane, n_lanes), axis=1, keepdims=True)
    lg2 = jnp.where(lane == i1, -jnp.inf, lg)
    m2 = jnp.max(lg2, axis=1, keepdims=True)
    i2 = jnp.min(jnp.where(lg2 == m2, lane, n_lanes), axis=1, keepdims=True)
    e2 = jnp.exp(m2 - m1)
    inv = 1.0 / (1.0 + e2)
    o_ref[...] = jnp.where(lane == i1, inv, 0.0) + jnp.where(lane == i2, e2 * inv, 0.0)


def moe_combine(x, g, router):
    m, d = x.shape
    n_e = router.shape[1]
    tm = _tile(m, 512)
    r_pad = jnp.zeros((d, 128), F32).at[:, :n_e].set(router.astype(F32))
    out = pl.pallas_call(
        functools.partial(_router_kernel, n_experts=n_e),
        grid=(m // tm,),
        in_specs=[pl.BlockSpec((tm, d), lambda i: (i, 0)),
                  pl.BlockSpec((1, d), lambda i: (0, 0)),
                  pl.BlockSpec((d, 128), lambda i: (0, 0))],
        out_specs=pl.BlockSpec((tm, 128), lambda i: (i, 0)),
        out_shape=jax.ShapeDtypeStruct((m, 128), F32),
        compiler_params=_params("parallel"),
        name="moe_router",
    )(x, g.reshape(1, d).astype(F32), r_pad)
    return out[:, :n_e]


def _cumsum_matrix(tk):
    r = np.arange(2 * tk)[:, None] % tk
    c = np.arange(2 * tk)[None, :]
    return jnp.asarray(np.where(c < tk, r > c, True).astype(np.float32), dtype=BF16)


def _sb_sweep_step(qs, k2s, v2s, before, u, acc, carry):
    tq = qs[0].shape[0]
    tk = k2s[0].shape[0] // 2
    z = _rows([_dot_nt(q, k2) for q, k2 in zip(qs, k2s)])
    neg_l = jnp.maximum(z, 0.0) + jnp.log2(1.0 + jnp.exp2(-jnp.abs(z)))
    log_b = z - neg_l
    if before is not None:
        neg_l = jnp.where(before, neg_l, 0.0)
    hi = neg_l.astype(BF16)
    lo = (neg_l - hi.astype(F32)).astype(BF16)
    s_new = _dot(jnp.concatenate([hi[:, tk:], lo[:, tk:]], axis=1), u)
    s_old = _dot(jnp.concatenate([hi[:, :tk], lo[:, :tk]], axis=1), u)
    c_old = carry + s_new[:, tk:]
    gap = jnp.concatenate([s_old[:, :tk] + c_old, s_new[:, :tk] + carry], axis=1)
    w = jnp.exp2(log_b - gap)
    if before is not None:
        w = jnp.where(before, w, 0.0)
    w = w.astype(BF16)
    acc = acc + _rows([_dot(w[n * tq:(n + 1) * tq], v2) for n, v2 in enumerate(v2s)])
    return acc, c_old + s_old[:, tk:]


def _sb_prompt_kernel(q_ref, k_ref, v_ref, u_ref, o_ref, kb_ref, vb_ref, *, tq, tk):
    i = pl.program_id(1)

    @pl.when(i == 0)
    def _():
        kb_ref[...] = k_ref[...].astype(BF16)
        vb_ref[...] = v_ref[...].astype(BF16)

    q = (q_ref[...] * (SCALE * LOG2E)).astype(BF16)
    u = u_ref[...]
    steps_per_tile = tq // (2 * tk)
    row = lax.broadcasted_iota(jnp.int32, (tq, 2 * tk), 0)
    lane = lax.broadcasted_iota(jnp.int32, (tq, 2 * tk), 1)
    state = (jnp.zeros((tq, HEAD_DIM), F32), jnp.zeros((tq, tk), F32))

    def step(off, before, state):
        return _sb_sweep_step([q], [kb_ref[pl.ds(off, 2 * tk), :]], [vb_ref[pl.ds(off, 2 * tk), :]],
                              before, u, *state)

    for d in reversed(range(steps_per_tile)):
        state = step(pl.multiple_of(i * tq + d * 2 * tk, 2 * tk), lane + d * 2 * tk < row, state)
    n_steps = i * steps_per_tile
    acc, _ = lax.fori_loop(
        0, n_steps, lambda s, st: step(pl.multiple_of((n_steps - 1 - s) * 2 * tk, 2 * tk), None, st), state)
    o_ref[...] = acc.astype(o_ref.dtype)


def sb_prompt(p, s, n_heads, q_col, k_col, v_col):
    tk = HEAD_DIM
    tq = _tile(s, 1024, 2 * tk)
    return pl.pallas_call(
        functools.partial(_sb_prompt_kernel, tq=tq, tk=tk),
        grid=(n_heads, s // tq),
        in_specs=[pl.BlockSpec((tq, HEAD_DIM), lambda h, i: (i, q_col + h)),
                  pl.BlockSpec((s, HEAD_DIM), lambda h, i: (0, k_col + h)),
                  pl.BlockSpec((s, HEAD_DIM), lambda h, i: (0, v_col + h)),
                  pl.BlockSpec((2 * tk, 2 * tk), lambda h, i: (0, 0))],
        out_specs=pl.BlockSpec((tq, HEAD_DIM), lambda h, i: (i, h)),
        out_shape=jax.ShapeDtypeStruct((s, n_heads * HEAD_DIM), BF16),
        scratch_shapes=[pltpu.VMEM((s, HEAD_DIM), BF16), pltpu.VMEM((s, HEAD_DIM), BF16)],
        compiler_params=_params("parallel", "arbitrary"),
        name="sb_prompt",
    )(p, p, p, _cumsum_matrix(tk))


def _sb_sample_kernel(q_ref, kn_ref, vn_ref, kc_ref, vc_ref, u_ref, o_ref, acc_ref, carry_ref, kb_ref, vb_ref, *,
                      t, tk, ck, n_heads):
    c = pl.program_id(1)
    heads = [slice(h * HEAD_DIM, (h + 1) * HEAD_DIM) for h in range(n_heads)]
    qs = [(q_ref[:, sl] * (SCALE * LOG2E)).astype(BF16) for sl in heads]
    u = u_ref[...]

    @pl.when(c == 0)
    def _():
        reps = 2 * tk // t
        k2s = [jnp.concatenate([kn_ref[:, sl].astype(BF16)] * reps, axis=0) for sl in heads]
        v2s = [jnp.concatenate([vn_ref[:, sl].astype(BF16)] * reps, axis=0) for sl in heads]
        before = (lax.broadcasted_iota(jnp.int32, (t, 2 * tk), 1) < lax.broadcasted_iota(jnp.int32, (t, 2 * tk), 0))
        acc, carry = _sb_sweep_step(qs, k2s, v2s, _rows([before] * n_heads), u,
                                    jnp.zeros(acc_ref.shape, F32), jnp.zeros(carry_ref.shape, F32))
        acc_ref[...] = acc
        carry_ref[...] = carry

    for h in range(n_heads):
        kb_ref[h] = kc_ref[pl.ds(h, ck, stride=n_heads), :].astype(BF16)
        vb_ref[h] = vc_ref[pl.ds(h, ck, stride=n_heads), :].astype(BF16)
    n_steps = ck // (2 * tk)

    def body(s, state):
        off = pl.multiple_of((n_steps - 1 - s) * 2 * tk, 2 * tk)
        k2s = [kb_ref[h, pl.ds(off, 2 * tk), :] for h in range(n_heads)]
        v2s = [vb_ref[h, pl.ds(off, 2 * tk), :] for h in range(n_heads)]
        return _sb_sweep_step(qs, k2s, v2s, None, u, *state)

    acc, carry = lax.fori_loop(0, n_steps, body, (acc_ref[...], carry_ref[...]), unroll=2)
    acc_ref[...] = acc
    carry_ref[...] = carry

    @pl.when(c == pl.num_programs(1) - 1)
    def _():
        for h, sl in enumerate(heads):
            o_ref[:, sl] = acc_ref[h * t:(h + 1) * t, :].astype(o_ref.dtype)


def sb_sample(p, cache_k, cache_v, row0, nb, t, q_col, k_col, v_col):
    _, past, n_heads, _ = cache_k.shape
    tk = HEAD_DIM
    w = n_heads * HEAD_DIM
    ck = _tile(past, 1024, 2 * tk)
    n_c = past // ck
    assert (2 * tk) % t == 0 and row0 % t == 0
    r0 = row0 // t
    cache_spec = pl.BlockSpec((None, ck * n_heads, HEAD_DIM), lambda b, c: (b, n_c - 1 - c, 0))
    cache_k = cache_k.reshape(nb, past * n_heads, HEAD_DIM)
    cache_v = cache_v.reshape(nb, past * n_heads, HEAD_DIM)
    return pl.pallas_call(
        functools.partial(_sb_sample_kernel, t=t, tk=tk, ck=ck, n_heads=n_heads),
        grid=(nb, n_c),
        in_specs=[pl.BlockSpec((t, w), lambda b, c: (r0 + b, q_col)),
                  pl.BlockSpec((t, w), lambda b, c: (r0 + b, k_col)),
                  pl.BlockSpec((t, w), lambda b, c: (r0 + b, v_col)),
                  cache_spec, cache_spec,
                  pl.BlockSpec((2 * tk, 2 * tk), lambda b, c: (0, 0))],
        out_specs=pl.BlockSpec((t, w), lambda b, c: (b, 0)),
        out_shape=jax.ShapeDtypeStruct((nb * t, w), BF16),
        scratch_shapes=[pltpu.VMEM((n_heads * t, HEAD_DIM), F32), pltpu.VMEM((n_heads * t, tk), F32),
                        pltpu.VMEM((n_heads, ck, HEAD_DIM), BF16), pltpu.VMEM((n_heads, ck, HEAD_DIM), BF16)],
        compiler_params=_params("parallel", "arbitrary"),
        name="sb_sample",
    )(p, p, p, cache_k, cache_v, _cumsum_matrix(tk))


def _retention_log_decay(n_heads):
    return [float(np.log1p(-np.float32(2.0 ** (-5.0 - h)))) for h in range(n_heads)]


def _retention_kernel(q_ref, k_ref, v_ref, g_ref, cos_ref, sin_ref, gn_ref, s0_ref, o_ref, sout_ref,
                      state_ref, dec_ref, *, t, n_chunks, log_g):
    c = pl.program_id(1)

    @pl.when(c == 0)
    def _():
        state_ref[...] = s0_ref[...]
        rel = (lax.broadcasted_iota(jnp.int32, (t, t), 0) - lax.broadcasted_iota(jnp.int32, (t, t), 1)).astype(F32)
        for h, lg in enumerate(log_g):
            dec_ref[h] = jnp.where(rel >= 0, jnp.exp(lg * jnp.maximum(rel, 0.0)), 0.0)

    cos, sin = cos_ref[...], sin_ref[...]
    idx = lax.broadcasted_iota(jnp.int32, (t, HEAD_DIM), 0).astype(F32)
    for h, lg in enumerate(log_g):
        sl = slice(h * HEAD_DIM, (h + 1) * HEAD_DIM)
        qh, kh = q_ref[:, sl], k_ref[:, sl]
        qh = qh * cos + pltpu.roll(qh, HEAD_DIM // 2, 1) * sin
        kh = (kh * cos + pltpu.roll(kh, HEAD_DIM // 2, 1) * sin) * SCALE
        qb, vb = qh.astype(BF16), v_ref[:, sl].astype(BF16)
        scores = _dot_nt(qb, kh.astype(BF16)) * dec_ref[h]
        state = state_ref[h]
        o = _dot(scores.astype(BF16), vb) + _dot(qb, state.astype(BF16)) * jnp.exp((idx + 1.0) * lg)
        k_dec = (kh * jnp.exp((t - 1.0 - idx) * lg)).astype(BF16)
        state_ref[h] = float(np.exp(np.float32(t * lg))) * state + _dot_tn(k_dec, vb)
        o = o * lax.rsqrt(jnp.mean(o * o, axis=-1, keepdims=True) + RMS_EPS) * gn_ref[:, sl]
        o_ref[:, sl] = (o * _silu(g_ref[:, sl])).astype(o_ref.dtype)

    @pl.when(c == n_chunks - 1)
    def _():
        sout_ref[...] = state_ref[...]


def retention(p, cos2, sin2, ret_norm_g, state0, row0, nb, seq, t, n_heads, q_col, k_col, v_col, g_col):
    w = n_heads * HEAD_DIM
    n_chunks = seq // t
    assert row0 % t == 0 and seq % t == 0
    r0 = row0 // t

    def rows(col):
        return pl.BlockSpec((t, w), lambda b, c: (r0 + b * n_chunks + c, col))

    return pl.pallas_call(
        functools.partial(_retention_kernel, t=t, n_chunks=n_chunks, log_g=_retention_log_decay(n_heads)),
        grid=(nb, n_chunks),
        in_specs=[rows(q_col), rows(k_col), rows(v_col), rows(g_col),
                  pl.BlockSpec((t, HEAD_DIM), lambda b, c: (c, 0)),
                  pl.BlockSpec((t, HEAD_DIM), lambda b, c: (c, 0)),
                  pl.BlockSpec((1, w), lambda b, c: (0, 0)),
                  pl.BlockSpec((None, n_heads, HEAD_DIM, HEAD_DIM), lambda b, c: (b, 0, 0, 0))],
        out_specs=[pl.BlockSpec((t, w), lambda b, c: (b * n_chunks + c, 0)),
                   pl.BlockSpec((None, n_heads, HEAD_DIM, HEAD_DIM), lambda b, c: (b, 0, 0, 0))],
        out_shape=[jax.ShapeDtypeStruct((nb * seq, w), BF16),
                   jax.ShapeDtypeStruct((nb, n_heads, HEAD_DIM, HEAD_DIM), F32)],
        scratch_shapes=[pltpu.VMEM((n_heads, HEAD_DIM, HEAD_DIM), F32), pltpu.VMEM((n_heads, t, t), F32)],
        compiler_params=_params("parallel", "arbitrary"),
        name="retention",
    )(p, p, p, p, cos2, sin2, ret_norm_g.reshape(1, w).astype(F32), state0)


def _rope_tables(pos):
    half = HEAD_DIM // 2
    inv = ROPE_BASE ** (-jnp.arange(half, dtype=F32) / half)
    ang = pos.astype(F32)[:, None] * inv[None, :]
    cos, sin = jnp.cos(ang), jnp.sin(ang)
    return jnp.concatenate([cos, cos], axis=1), jnp.concatenate([-sin, sin], axis=1)


def _band_bias_table(rel_bias, tq):
    n_heads = rel_bias.shape[0]
    w = BAND_WINDOW + tq
    n = tq + w - 1
    dist = BAND_WINDOW + tq - 1 - np.arange(n)
    vec = rel_bias.astype(F32)[:, np.clip(dist, -MAX_REL, MAX_REL) + MAX_REL]
    vec = jnp.roll(vec, -(tq - 1), axis=1)
    bias = jnp.tile(vec, (1, tq))[:, :tq * (n - 1)].reshape(n_heads, tq, n - 1)[:, :, :w]
    t = np.arange(tq)[:, None]
    j = np.arange(w)[None, :]
    lo = (t // CHUNK) * CHUNK
    in_band = (j >= lo) & (j < lo + BAND_WINDOW + CHUNK)
    return jnp.where(jnp.asarray(in_band)[None], bias, NEG_INF)


def _softmax_pv(scores, values):
    m = scores[0].max(axis=1, keepdims=True)
    for s in scores[1:]:
        m = jnp.maximum(m, s.max(axis=1, keepdims=True))
    ps = [jnp.exp(s - m) for s in scores]
    denom = ps[0].sum(axis=1, keepdims=True)
    for p in ps[1:]:
        denom = denom + p.sum(axis=1, keepdims=True)
    o = _dot(ps[0].astype(BF16), values[0])
    for p, v in zip(ps[1:], values[1:]):
        o = o + _dot(p.astype(BF16), v)
    return o * (1.0 / denom)


def _band_prompt_kernel(*refs, tq, n_kb):
    q_ref = refs[0]
    k_refs, v_refs = refs[1:1 + n_kb], refs[1 + n_kb:1 + 2 * n_kb]
    b_ref, o_ref = refs[1 + 2 * n_kb], refs[2 + 2 * n_kb]
    i = pl.program_id(1)
    q = (q_ref[...] * SCALE).astype(BF16)
    scores, values = [], []
    for kb in range(n_kb):
        s = _dot_nt(q, k_refs[kb][...].astype(BF16)) + b_ref[:, kb * tq:(kb + 1) * tq]
        scores.append(s + jnp.where(i - (n_kb - 1) + kb >= 0, 0.0, NEG_INF))
        values.append(v_refs[kb][...].astype(BF16))
    o_ref[...] = _softmax_pv(scores, values).astype(o_ref.dtype)


def band_prompt(p, rel_bias, s, n_heads, q_col, k_col, v_col):
    tq = _tile(s, 256)
    assert BAND_WINDOW % tq == 0 and tq % CHUNK == 0
    n_kb = BAND_WINDOW // tq + 1

    def kv_specs(col):
        return [pl.BlockSpec((tq, HEAD_DIM),
                             lambda h, i, kb=kb: (jnp.maximum(i - (n_kb - 1) + kb, 0), col + h))
                for kb in range(n_kb)]

    return pl.pallas_call(
        functools.partial(_band_prompt_kernel, tq=tq, n_kb=n_kb),
        grid=(n_heads, s // tq),
        in_specs=([pl.BlockSpec((tq, HEAD_DIM), lambda h, i: (i, q_col + h))] + kv_specs(k_col) + kv_specs(v_col)
                  + [pl.BlockSpec((None, tq, BAND_WINDOW + tq), lambda h, i: (h, 0, 0))]),
        out_specs=pl.BlockSpec((tq, HEAD_DIM), lambda h, i: (i, h)),
        out_shape=jax.ShapeDtypeStruct((s, n_heads * HEAD_DIM), BF16),
        compiler_params=_params("parallel", "parallel"),
        name="band_prompt",
    )(*([p] * (1 + 2 * n_kb)), _band_bias_table(rel_bias, tq))


def _band_sample_kernel(q_ref, kn_ref, vn_ref, kc_ref, vc_ref, b_ref, o_ref, ko_ref, vo_ref, *,
                        band_past, t, n_heads):
    keep = (band_past - t) * n_heads
    for h in range(n_heads):
        sl = slice(h * HEAD_DIM, (h + 1) * HEAD_DIM)
        q = (q_ref[:, sl] * SCALE).astype(BF16)
        kn, vn = kn_ref[:, sl], vn_ref[:, sl]
        kc = kc_ref[pl.ds(h, band_past, stride=n_heads), :].astype(BF16)
        vc = vc_ref[pl.ds(h, band_past, stride=n_heads), :].astype(BF16)
        s_c = _dot_nt(q, kc) + b_ref[h, :, :band_past]
        s_n = _dot_nt(q, kn.astype(BF16)) + b_ref[h, :, band_past:]
        o_ref[:, sl] = _softmax_pv([s_c, s_n], [vc, vn.astype(BF16)]).astype(o_ref.dtype)
        ko_ref[pl.ds(keep + h, t, stride=n_heads), :] = kn
        vo_ref[pl.ds(keep + h, t, stride=n_heads), :] = vn
    ko_ref[:keep] = kc_ref[t * n_heads:]
    vo_ref[:keep] = vc_ref[t * n_heads:]


def band_sample(p, cache_k, cache_v, rel_bias, row0, nb, t, q_col, k_col, v_col):
    _, band_past, n_heads, _ = cache_k.shape
    w = n_heads * HEAD_DIM
    assert t == CHUNK and band_past == BAND_WINDOW and row0 % t == 0
    r0 = row0 // t
    cache_spec = pl.BlockSpec((None, band_past * n_heads, HEAD_DIM), lambda b: (b, 0, 0))
    cache_shape = jax.ShapeDtypeStruct((nb, band_past * n_heads, HEAD_DIM), F32)
    cache_k = cache_k.reshape(cache_shape.shape)
    cache_v = cache_v.reshape(cache_shape.shape)
    o, k_out, v_out = pl.pallas_call(
        functools.partial(_band_sample_kernel, band_past=band_past, t=t, n_heads=n_heads),
        grid=(nb,),
        in_specs=[pl.BlockSpec((t, w), lambda b: (r0 + b, q_col)),
                  pl.BlockSpec((t, w), lambda b: (r0 + b, k_col)),
                  pl.BlockSpec((t, w), lambda b: (r0 + b, v_col)),
                  cache_spec, cache_spec,
                  pl.BlockSpec((n_heads, t, band_past + t), lambda b: (0, 0, 0))],
        out_specs=[pl.BlockSpec((t, w), lambda b: (b, 0)), cache_spec, cache_spec],
        out_shape=[jax.ShapeDtypeStruct((nb * t, w), BF16), cache_shape, cache_shape],
        compiler_params=_params("parallel"),
        name="band_sample",
    )(p, p, p, cache_k.astype(F32), cache_v.astype(F32), _band_bias_table(rel_bias, t))
    out_4d = (nb, band_past, n_heads, HEAD_DIM)
    return o, k_out.reshape(out_4d), v_out.reshape(out_4d)


def _cross_attn_kernel(q_ref, mk_ref, mv_ref, o_ref, *, n_heads):
    for h in range(n_heads):
        sl = slice(h * HEAD_DIM, (h + 1) * HEAD_DIM)
        q = (q_ref[:, sl] * SCALE).astype(BF16)
        s = _dot_nt(q, mk_ref[:, sl].astype(BF16))
        o_ref[:, sl] = _softmax_pv([s], [mv_ref[:, sl].astype(BF16)]).astype(o_ref.dtype)


def cross_attn(q, mk, mv, row0, nb, seq):
    n_mem, w = mk.shape[1], mk.shape[2]
    tq = _tile(seq, 512)
    n_t = seq // tq
    assert row0 % tq == 0
    r0 = row0 // tq
    return pl.pallas_call(
        functools.partial(_cross_attn_kernel, n_heads=w // HEAD_DIM),
        grid=(nb, n_t),
        in_specs=[pl.BlockSpec((tq, w), lambda b, i: (r0 + b * n_t + i, 0)),
                  pl.BlockSpec((None, n_mem, w), lambda b, i: (b, 0, 0)),
                  pl.BlockSpec((None, n_mem, w), lambda b, i: (b, 0, 0))],
        out_specs=pl.BlockSpec((tq, w), lambda b, i: (b * n_t + i, 0)),
        out_shape=jax.ShapeDtypeStruct((nb * seq, w), BF16),
        compiler_params=_params("parallel", "parallel"),
        name="cross_attn",
    )(q, mk, mv)


def kernel(x_prompt, x_sample, cache_sb_k, cache_sb_v, state_ret, cache_band_k, cache_band_v, cache_mem_k, cache_mem_v, mem_prompt, w_in_ab, w_out_ab, ret_norm_g, w_qkv_band, w_out_band, rel_bias_band, norm_g, mem_norm_g, w_xq, w_xk, w_xv, w_xo, ffn_w_gate, ffn_w_up, ffn_w_down, moe_router, moe_w_gate, moe_w_up, moe_w_down, final_norm_g):
    bp, s, d = x_prompt.shape
    nb, t, _ = x_sample.shape
    assert bp == 1
    past = cache_sb_k.shape[1]
    band_past = cache_band_k.shape[1]
    h_sb = cache_sb_k.shape[2]
    h_ret = state_ret.shape[1]
    h_band = cache_band_k.shape[2]
    d_sb, d_ret = h_sb * HEAD_DIM, h_ret * HEAD_DIM
    n_mem = mem_prompt.shape[1]
    d_x = w_xq.shape[2]
    depth = w_xq.shape[0]
    n_exp, _, d_ffe = moe_w_gate.shape
    d_ff = ffn_w_gate.shape[1]
    ms = nb * t
    assert d_sb == d_ret and 3 * d_sb + 4 * d_ret == w_in_ab.shape[1]

    x = jnp.concatenate([x_prompt.reshape(s, d), x_sample.reshape(ms, d)], axis=0)
    mem = mem_prompt.reshape(n_mem, d)
    bf = lambda a: a.astype(BF16)

    cos_p, sin_p = _rope_tables(jnp.arange(s, dtype=jnp.int32))
    cos_s, sin_s = _rope_tables(past + jnp.arange(t, dtype=jnp.int32))

    mem_k_list, mem_v_list = [], []
    for l in range(depth):
        mem_n = rmsnorm(mem, mem_norm_g[l], BF16)
        mk_p = matmul([(mem_n, 0, 0)], bf(w_xk[l]), d)
        mv_p = matmul([(mem_n, 0, 0)], bf(w_xv[l]), d)
        mem_k_list.append(mk_p.reshape(1, n_mem, H_X, HEAD_DIM))
        mem_v_list.append(mv_p.reshape(1, n_mem, H_X, HEAD_DIM))

        h = rmsnorm(x, norm_g[l, 0], BF16)
        if l % 2 == 0:
            p = matmul([(h, 0, 0)], bf(w_in_ab), d)
            o_sb_p = sb_prompt(p, s, h_sb, 0, h_sb, 2 * h_sb)
            o_sb_s = sb_sample(p, cache_sb_k, cache_sb_v, s, nb, t, 0, 1, 2)
            t_ret = _tile(s, 256, CHUNK)
            o_r_p, ret_state_prompt = retention(p, cos_p, sin_p, ret_norm_g, jnp.zeros((1,) + state_ret.shape[1:], F32),
                                                0, 1, s, t_ret, h_ret, 3, 4, 5, 6)
            o_r_s, ret_state_sample = retention(p, cos_s, sin_s, ret_norm_g, state_ret.astype(F32),
                                                s, nb, t, t, h_ret, 3, 4, 5, 6)
            o_sb = jnp.concatenate([o_sb_p, o_sb_s], axis=0)
            o_r = jnp.concatenate([o_r_p, o_r_s], axis=0)
            x = matmul([(o_sb, 0, 0), (o_r, 0, d_sb)], bf(w_out_ab), d_sb, residual=x)
            sb_k_prompt = p[:s, d_sb:2 * d_sb].reshape(1, s, h_sb, HEAD_DIM)
            sb_v_prompt = p[:s, 2 * d_sb:3 * d_sb].reshape(1, s, h_sb, HEAD_DIM)
            sb_k_sample = p[s:, d_sb:2 * d_sb].reshape(nb, t, h_sb, HEAD_DIM)
            sb_v_sample = p[s:, 2 * d_sb:3 * d_sb].reshape(nb, t, h_sb, HEAD_DIM)
        else:
            d_band = h_band * HEAD_DIM
            p = matmul([(h, 0, 0)], bf(w_qkv_band), d)
            o_p = band_prompt(p, rel_bias_band, s, h_band, 0, h_band, 2 * h_band)
            o_s, band_k_sample, band_v_sample = band_sample(p, cache_band_k, cache_band_v, rel_bias_band,
                                                            s, nb, t, 0, 1, 2)
            o = jnp.concatenate([o_p, o_s], axis=0)
            x = matmul([(o, 0, 0)], bf(w_out_band), d_band, residual=x)
            band_k_prompt = p[s - band_past:s, d_band:2 * d_band].reshape(1, band_past, h_band, HEAD_DIM)
            band_v_prompt = p[s - band_past:s, 2 * d_band:].reshape(1, band_past, h_band, HEAD_DIM)

        h = rmsnorm(x, norm_g[l, 1], BF16)
        q = matmul([(h, 0, 0)], bf(w_xq[l]), d)
        o_p = cross_attn(q, mk_p.reshape(1, n_mem, d_x), mv_p.reshape(1, n_mem, d_x), 0, 1, s)
        o_s = cross_attn(q, cache_mem_k[l].reshape(nb, n_mem, d_x), cache_mem_v[l].reshape(nb, n_mem, d_x), s, nb, t)
        o = jnp.concatenate([o_p, o_s], axis=0)
        x = matmul([(o, 0, 0)], bf(w_xo[l]), d_x, residual=x)

        h = rmsnorm(x, norm_g[l, 2], BF16)
        if l % 2 == 0:
            act = swiglu_up(h, bf(ffn_w_gate)[None], bf(ffn_w_up)[None], tn_pref=512)
            x = matmul([(act, 0, 0)], bf(ffn_w_down), d_ff, residual=x, tk_pref=2816)
        else:
            combine = moe_combine(x, norm_g[l, 2], moe_router)
            act = swiglu_up(h, bf(moe_w_gate), bf(moe_w_up), combine, tn_pref=256)
            x = matmul([(act, 0, 0)], bf(moe_w_down).reshape(n_exp * d_ffe, d), n_exp * d_ffe, residual=x,
                       tk_pref=2816)

    y = rmsnorm(x, final_norm_g, F32)
    y_prompt = y[:s].reshape(1, s, d)
    y_sample = y[s:].reshape(nb, t, d)
    mem_k_prompt = jnp.stack(mem_k_list, axis=0)
    mem_v_prompt = jnp.stack(mem_v_list, axis=0)
    return (y_prompt, y_sample, sb_k_prompt, sb_v_prompt, sb_k_sample, sb_v_sample,
            ret_state_prompt, ret_state_sample, band_k_prompt, band_v_prompt,
            band_k_sample, band_v_sample, mem_k_prompt, mem_v_prompt)
```

```python
import functools

import numpy as np
import jax
import jax.numpy as jnp
from jax import lax
from jax.experimental import pallas as pl
from jax.experimental.pallas import tpu as pltpu

F32 = jnp.float32
BF16 = jnp.bfloat16

HEAD_DIM = 128
CHUNK = 64
N_BAND_CHUNKS = 8
BAND_WINDOW = N_BAND_CHUNKS * CHUNK
MAX_REL = 128
H_X = 4
TOP_K = 2
RMS_EPS = 1e-6
ROPE_BASE = 10000.0
NEG_INF = -1e30
SCALE = HEAD_DIM ** -0.5
LOG2E = 1.4426950408889634

VMEM_LIMIT_BYTES = 56 * 1024 * 1024


def _params(*sem):
    return pltpu.CompilerParams(dimension_semantics=sem, vmem_limit_bytes=VMEM_LIMIT_BYTES)


def _tile(n, pref, mult=8):
    t = min(n, pref)
    while t > mult and (n % t or t % mult):
        t -= mult
    assert n % t == 0, (n, pref)
    return t


def _dot(a, b):
    return jnp.dot(a, b, preferred_element_type=F32)


def _dot_nt(a, b):
    return lax.dot_general(a, b, (((1,), (1,)), ((), ())), preferred_element_type=F32)


def _dot_tn(a, b):
    return lax.dot_general(a, b, (((0,), (0,)), ((), ())), preferred_element_type=F32)


def _silu(a):
    return a * (1.0 / (1.0 + jnp.exp(-a)))


def _rows(parts):
    return parts[0] if len(parts) == 1 else jnp.concatenate(parts, axis=0)


def _rmsnorm_kernel(x_ref, g_ref, o_ref):
    x = x_ref[...]
    y = x * lax.rsqrt(jnp.mean(x * x, axis=-1, keepdims=True) + RMS_EPS) * g_ref[...]
    o_ref[...] = y.astype(o_ref.dtype)


def rmsnorm(x, g, out_dtype):
    m, d = x.shape
    tm = _tile(m, 512)
    return pl.pallas_call(
        _rmsnorm_kernel,
        grid=(m // tm,),
        in_specs=[pl.BlockSpec((tm, d), lambda i: (i, 0)),
                  pl.BlockSpec((1, d), lambda i: (0, 0))],
        out_specs=pl.BlockSpec((tm, d), lambda i: (i, 0)),
        out_shape=jax.ShapeDtypeStruct((m, d), out_dtype),
        compiler_params=_params("parallel"),
        name="rmsnorm",
    )(x, g.reshape(1, d).astype(F32))


def _mm_kernel(*refs, n_parts, has_res, nk):
    xs, ws = refs[:n_parts], refs[n_parts:2 * n_parts]
    pos = 2 * n_parts
    res_ref = refs[pos] if has_res else None
    pos += int(has_res)
    o_ref = refs[pos]
    part = _dot(xs[0][...], ws[0][...])
    for x_ref, w_ref in zip(xs[1:], ws[1:]):
        part = part + _dot(x_ref[...], w_ref[...])
    if nk == 1:
        if has_res:
            part = res_ref[...] + part
        o_ref[...] = part.astype(o_ref.dtype)
        return
    acc_ref = refs[pos + 1]
    k = pl.program_id(2)

    @pl.when(k == 0)
    def _():
        acc_ref[...] = part

    @pl.when(k > 0)
    def _():
        acc_ref[...] += part

    @pl.when(k == nk - 1)
    def _():
        out = acc_ref[...]
        if has_res:
            out = res_ref[...] + out
        o_ref[...] = out.astype(o_ref.dtype)


def matmul(parts, w, k_part, *, col_off=0, n_cols=None, out_dtype=F32, residual=None,
           tm_pref=1024, tn_pref=512, tk_pref=2048):
    m = parts[0][0].shape[0]
    n_cols = w.shape[1] - col_off if n_cols is None else n_cols
    tm = _tile(m, tm_pref)
    tn = _tile(n_cols, tn_pref, 128)
    tk = _tile(k_part, tk_pref, 128)
    nk = k_part // tk
    assert col_off % tn == 0
    in_specs, args = [], []
    for x, xo, _ in parts:
        assert xo % tk == 0
        in_specs.append(pl.BlockSpec((tm, tk), lambda i, j, k, xo=xo: (i, xo // tk + k)))
        args.append(x)
    for _, _, wo in parts:
        assert wo % tk == 0
        in_specs.append(pl.BlockSpec((tk, tn), lambda i, j, k, wo=wo: (wo // tk + k, col_off // tn + j)))
        args.append(w)
    if residual is not None:
        in_specs.append(pl.BlockSpec((tm, tn), lambda i, j, k: (i, j)))
        args.append(residual)
    return pl.pallas_call(
        functools.partial(_mm_kernel, n_parts=len(parts), has_res=residual is not None, nk=nk),
        grid=(m // tm, n_cols // tn, nk),
        in_specs=in_specs,
        out_specs=pl.BlockSpec((tm, tn), lambda i, j, k: (i, j)),
        out_shape=jax.ShapeDtypeStruct((m, n_cols), out_dtype),
        scratch_shapes=[pltpu.VMEM((tm, tn), F32)] if nk > 1 else [],
        compiler_params=_params("parallel", "parallel", "arbitrary"),
        name="matmul",
    )(*args)


def _swiglu_up_kernel(x_ref, wg_ref, wu_ref, o_ref):
    x = x_ref[...]
    o_ref[...] = (_silu(_dot(x, wg_ref[...])) * _dot(x, wu_ref[...])).astype(o_ref.dtype)


def swiglu_up(x, wg, wu, *, tn_pref):
    m, d = x.shape
    f = wg.shape[1]
    tm = _tile(m, 1024)
    tn = _tile(f, tn_pref, 128)
    return pl.pallas_call(
        _swiglu_up_kernel,
        grid=(m // tm, f // tn),
        in_specs=[pl.BlockSpec((tm, d), lambda i, j: (i, 0)),
                  pl.BlockSpec((d, tn), lambda i, j: (0, j)),
                  pl.BlockSpec((d, tn), lambda i, j: (0, j))],
        out_specs=pl.BlockSpec((tm, tn), lambda i, j: (i, j)),
        out_shape=jax.ShapeDtypeStruct((m, f), BF16),
        compiler_params=_params("parallel", "parallel"),
        name="swiglu_up",
    )(x, wg, wu)


def _router_kernel(x_ref, g_ref, r_ref, o_ref, *, n_experts):
    x = x_ref[...]
    hn = x * lax.rsqrt(jnp.mean(x * x, axis=-1, keepdims=True) + RMS_EPS) * g_ref[...]
    logits = jnp.dot(hn, r_ref[...], preferred_element_type=F32, precision=lax.Precision.HIGHEST)
    lane = lax.broadcasted_iota(jnp.int32, logits.shape, 1)
    n_lanes = logits.shape[1]
    lg = jnp.where(lane < n_experts, logits, -jnp.inf)
    m1 = jnp.max(lg, axis=1, keepdims=True)
    i1 = jnp.min(jnp.where(lg == m1, lane, n_lanes), axis=1, keepdims=True)
    lg2 = jnp.where(lane == i1, -jnp.inf, lg)
    m2 = jnp.max(lg2, axis=1, keepdims=True)
    i2 = jnp.min(jnp.where(lg2 == m2, lane, n_lanes), axis=1, keepdims=True)
    e2 = jnp.exp(m2 - m1)
    inv = 1.0 / (1.0 + e2)
    o_ref[...] = (jnp.where(lane == 0, inv, 0.0) + jnp.where(lane == 1, e2 * inv, 0.0)
                  + jnp.where(lane == 2, i1.astype(F32), 0.0) + jnp.where(lane == 3, i2.astype(F32), 0.0))


def moe_route(x, g, router):
    m, d = x.shape
    n_e = router.shape[1]
    tm = _tile(m, 512)
    r_pad = jnp.zeros((d, 128), F32).at[:, :n_e].set(router.astype(F32))
    out = pl.pallas_call(
        functools.partial(_router_kernel, n_experts=n_e),
        grid=(m // tm,),
        in_specs=[pl.BlockSpec((tm, d), lambda i: (i, 0)),
                  pl.BlockSpec((1, d), lambda i: (0, 0)),
                  pl.BlockSpec((d, 128), lambda i: (0, 0))],
        out_specs=pl.BlockSpec((tm, 128), lambda i: (i, 0)),
        out_shape=jax.ShapeDtypeStruct((m, 128), F32),
        compiler_params=_params("parallel"),
        name="moe_router",
    )(x, g.reshape(1, d).astype(F32), r_pad)
    return out[:, :TOP_K], out[:, TOP_K:2 * TOP_K].astype(jnp.int32)


MOE_ROW_TILE = 256
MOE_GATHER_CHUNK = 512
MOE_COMBINE_TILE = 1024


def _i32(a):
    return a.astype(jnp.int32)


def _moe_plan(idx, gates, n_e, tr, tc, tt):
    m = idx.shape[0]
    assert (TOP_K * m) % tr == 0 and m % tc == 0 and m % tt == 0
    n_tiles = TOP_K * m // tr + n_e
    n_rows = n_tiles * tr
    routed = jnp.zeros((m, n_e), jnp.int32)
    for k in range(TOP_K):
        routed = routed + _i32(idx[:, k:k + 1] == jnp.arange(n_e, dtype=jnp.int32)[None, :])
    csum = jnp.cumsum(routed, axis=0, dtype=jnp.int32)
    rank = csum - routed
    tiles_e = (csum[-1] + tr - 1) // tr
    tile_end = jnp.cumsum(tiles_e, dtype=jnp.int32)
    row_start = (tile_end - tiles_e) * tr
    tok = jnp.arange(m, dtype=jnp.int32)
    src = jnp.full((n_rows,), -1, jnp.int32)
    gate_rows = jnp.zeros((n_rows,), F32)
    for k in range(TOP_K):
        dest = row_start[idx[:, k]] + jnp.take_along_axis(rank, idx[:, k:k + 1], axis=1)[:, 0]
        src = src.at[dest].set(tok, unique_indices=True)
        gate_rows = gate_rows.at[dest].set(gates[:, k], unique_indices=True)
    tile_ids = jnp.arange(n_tiles, dtype=jnp.int32)
    tile_valid = tile_ids < tile_end[-1]
    tile_expert = jnp.minimum(_i32(jnp.searchsorted(tile_end, tile_ids, side="right")), n_e - 1)

    def visits(first_chunk, n_visits, n_max, owner_of):
        end = jnp.cumsum(n_visits, dtype=jnp.int32)
        start = end - n_visits
        v = jnp.arange(n_max, dtype=jnp.int32)
        live = v < end[-1]
        slot = jnp.minimum(_i32(jnp.searchsorted(end, v, side="right")), n_visits.shape[0] - 1)
        owner = owner_of(slot)
        chunk = first_chunk[slot] + v - start[slot]
        last = jnp.maximum(end[-1] - 1, 0)
        owner = jnp.where(live, owner, owner[last])
        chunk = jnp.where(live, chunk, chunk[last])
        first = live & ((v == 0) | (owner != jnp.roll(owner, 1)))
        return _i32(owner), _i32(chunk), _i32(first) + 2 * _i32(live)

    src_t = src.reshape(n_tiles, tr)
    c_first = jnp.where(tile_valid, src_t[:, 0], 0) // tc
    c_last = jnp.where(tile_valid, src_t.max(axis=1), 0) // tc
    g_plan = visits(c_first, c_last - c_first + 1, n_tiles + n_e * (m // tc - 1), lambda s: s)
    n_tt = m // tt
    before = jnp.concatenate([jnp.zeros((1, n_e), jnp.int32), csum[tt - 1::tt]], axis=0)
    lo = row_start[None, :] + before[:-1]
    hi = row_start[None, :] + before[1:]
    n_vis = jnp.where(hi > lo, (hi - 1) // tr - lo // tr + 1, 0)
    c_plan = visits((lo // tr).reshape(-1), n_vis.reshape(-1), n_tt * n_e + n_tiles - 1, lambda s: s // n_e)
    return dict(n_tiles=n_tiles, src=src.reshape(n_tiles, 1, tr), gate_rows=gate_rows.reshape(n_rows, 1),
                tile_expert=tile_expert, tile_valid=_i32(tile_valid), gather=g_plan, combine=c_plan)


def _moe_gather_kernel(vt_ref, vc_ref, vf_ref, src_ref, h_ref, o_ref, *, tc):
    v = pl.program_id(0)
    flags = vf_ref[v]

    @pl.when(flags % 2 == 1)
    def _():
        o_ref[...] = jnp.zeros_like(o_ref)

    @pl.when(flags >= 2)
    def _():
        tok = vc_ref[v] * tc + lax.broadcasted_iota(jnp.int32, (tc, o_ref.shape[0]), 0)
        sel = jnp.where(tok == src_ref[...], 1.0, 0.0).astype(BF16)
        o_ref[...] += _dot_tn(sel, h_ref[...]).astype(o_ref.dtype)


def _moe_up_kernel(te_ref, tv_ref, x_ref, wg_ref, wu_ref, o_ref):
    j = pl.program_id(1)

    @pl.when(tv_ref[j] != 0)
    def _():
        x = x_ref[...]
        o_ref[...] = (_silu(_dot(x, wg_ref[...])) * _dot(x, wu_ref[...])).astype(o_ref.dtype)

    @pl.when(tv_ref[j] == 0)
    def _():
        o_ref[...] = jnp.zeros_like(o_ref)


def _moe_down_kernel(te_ref, a_ref, wd_ref, g_ref, o_ref):
    o_ref[...] = (_dot(a_ref[...], wd_ref[...]) * g_ref[...]).astype(o_ref.dtype)


def _moe_combine_kernel(vi_ref, vc_ref, vf_ref, src_ref, y_ref, x_ref, o_ref, *, tt):
    v = pl.program_id(0)
    flags = vf_ref[v]

    @pl.when(flags % 2 == 1)
    def _():
        o_ref[...] = x_ref[...]

    @pl.when(flags >= 2)
    def _():
        tok = vi_ref[v] * tt + lax.broadcasted_iota(jnp.int32, (tt, y_ref.shape[0]), 0)
        sel = jnp.where(tok == src_ref[...], 1.0, 0.0).astype(BF16)
        o_ref[...] += _dot(sel, y_ref[...])


def moe_top2(x, h, idx, gates, wg, wu, wd):
    m, d = h.shape
    n_e, _, f = wg.shape
    tr, tc, tt = MOE_ROW_TILE, _tile(m, MOE_GATHER_CHUNK, 16), _tile(m, MOE_COMBINE_TILE)
    plan =_moe_plan(idx, gates, n_e, tr, tc, tt)
    n_tiles = plan["n_tiles"]
    n_rows = n_tiles * tr

    vt, vc, vf = plan["gather"]
    xs = pl.pallas_call(
        functools.partial(_moe_gather_kernel, tc=tc),
        grid_spec=pltpu.PrefetchScalarGridSpec(
            num_scalar_prefetch=3, grid=(vt.shape[0],),
            in_specs=[pl.BlockSpec((None, 1, tr), lambda v, vt, vc, vf: (vt[v], 0, 0)),
                      pl.BlockSpec((tc, d), lambda v, vt, vc, vf: (vc[v], 0))],
            out_specs=pl.BlockSpec((tr, d), lambda v, vt, vc, vf: (vt[v], 0))),
        out_shape=jax.ShapeDtypeStruct((n_rows, d), BF16),
        compiler_params=_params("arbitrary"),
        name="moe_gather",
    )(vt, vc, vf, plan["src"], h)

    tf = _tile(f, 1408, 128)
    act = pl.pallas_call(
        _moe_up_kernel,
        grid_spec=pltpu.PrefetchScalarGridSpec(
            num_scalar_prefetch=2, grid=(f // tf, n_tiles),
            in_specs=[pl.BlockSpec((tr, d), lambda c, j, te, tv: (j, 0)),
                      pl.BlockSpec((None, d, tf), lambda c, j, te, tv: (te[j], 0, c)),
                      pl.BlockSpec((None, d, tf), lambda c, j, te, tv: (te[j], 0, c))],
            out_specs=pl.BlockSpec((tr, tf), lambda c, j, te, tv: (j, c))),
        out_shape=jax.ShapeDtypeStruct((n_rows, f), BF16),
        compiler_params=_params("parallel", "parallel"),
        name="moe_up",
    )(plan["tile_expert"], plan["tile_valid"], xs, wg, wu)

    y = pl.pallas_call(
        _moe_down_kernel,
        grid_spec=pltpu.PrefetchScalarGridSpec(
            num_scalar_prefetch=1, grid=(n_tiles,),
            in_specs=[pl.BlockSpec((tr, f), lambda j, te: (j, 0)),
                      pl.BlockSpec((None, f, d), lambda j, te: (te[j], 0, 0)),
                      pl.BlockSpec((tr, 1), lambda j, te: (j, 0))],
            out_specs=pl.BlockSpec((tr, d), lambda j, te: (j, 0))),
        out_shape=jax.ShapeDtypeStruct((n_rows, d), BF16),
        compiler_params=_params("parallel"),
        name="moe_down",
    )(plan["tile_expert"], act, wd, plan["gate_rows"])

    vi, vc, vf = plan["combine"]
    return pl.pallas_call(
        functools.partial(_moe_combine_kernel, tt=tt),
        grid_spec=pltpu.PrefetchScalarGridSpec(
            num_scalar_prefetch=3, grid=(vi.shape[0],),
            in_specs=[pl.BlockSpec((None, 1, tr), lambda v, vi, vc, vf: (vc[v], 0, 0)),
                      pl.BlockSpec((tr, d), lambda v, vi, vc, vf: (vc[v], 0)),
                      pl.BlockSpec((tt, d), lambda v, vi, vc, vf: (vi[v], 0))],
            out_specs=pl.BlockSpec((tt, d), lambda v, vi, vc, vf: (vi[v], 0))),
        out_shape=jax.ShapeDtypeStruct((m, d), F32),
        compiler_params=_params("arbitrary"),
        name="moe_combine",
    )(vi, vc, vf, plan["src"], y, x)


def _cumsum_matrix(tk):
    r = np.arange(2 * tk)[:, None] % tk
    c = np.arange(2 * tk)[None, :]
    return jnp.asarray(np.where(c < tk, r > c, True).astype(np.float32), dtype=BF16)


def _sb_sweep_step(qs, k2s, v2s, before, u, acc, carry):
    tq = qs[0].shape[0]
    tk = k2s[0].shape[0] // 2
    z = _rows([_dot_nt(q, k2) for q, k2 in zip(qs, k2s)])
    neg_l = jnp.maximum(z, 0.0) + jnp.log2(1.0 + jnp.exp2(-jnp.abs(z)))
    log_b = z - neg_l
    if before is not None:
        neg_l = jnp.where(before, neg_l, 0.0)
    hi = neg_l.astype(BF16)
    lo = (neg_l - hi.astype(F32)).astype(BF16)
    s_new = _dot(jnp.concatenate([hi[:, tk:], lo[:, tk:]], axis=1), u)
    s_old = _dot(jnp.concatenate([hi[:, :tk], lo[:, :tk]], axis=1), u)
    c_old = carry + s_new[:, tk:]
    gap = jnp.concatenate([s_old[:, :tk] + c_old, s_new[:, :tk] + carry], axis=1)
    w = jnp.exp2(log_b - gap)
    if before is not None:
        w = jnp.where(before, w, 0.0)
    w = w.astype(BF16)
    acc = acc + _rows([_dot(w[n * tq:(n + 1) * tq], v2) for n, v2 in enumerate(v2s)])
    return acc, c_old + s_old[:, tk:]


def _sb_prompt_kernel(q_ref, k_ref, v_ref, u_ref, o_ref, kb_ref, vb_ref, *, tq, tk):
    i = pl.program_id(1)

    @pl.when(i == 0)
    def _():
        kb_ref[...] = k_ref[...].astype(BF16)
        vb_ref[...] = v_ref[...].astype(BF16)

    q = (q_ref[...] * (SCALE * LOG2E)).astype(BF16)
    u = u_ref[...]
    steps_per_tile = tq // (2 * tk)
    row = lax.broadcasted_iota(jnp.int32, (tq, 2 * tk), 0)
    lane = lax.broadcasted_iota(jnp.int32, (tq, 2 * tk), 1)
    state = (jnp.zeros((tq, HEAD_DIM), F32), jnp.zeros((tq, tk), F32))

    def step(off, before, state):
        return _sb_sweep_step([q], [kb_ref[pl.ds(off, 2 * tk), :]], [vb_ref[pl.ds(off, 2 * tk), :]],
                              before, u, *state)

    for d in reversed(range(steps_per_tile)):
        state = step(pl.multiple_of(i * tq + d * 2 * tk, 2 * tk), lane + d * 2 * tk < row, state)
    n_steps = i * steps_per_tile
    acc, _ = lax.fori_loop(
        0, n_steps, lambda s, st: step(pl.multiple_of((n_steps - 1 - s) * 2 * tk, 2 * tk), None, st), state)
    o_ref[...] = acc.astype(o_ref.dtype)


def sb_prompt(p, s, n_heads, q_col, k_col, v_col):
    tk = HEAD_DIM
    tq = _tile(s, 1024, 2 * tk)
    return pl.pallas_call(
        functools.partial(_sb_prompt_kernel, tq=tq, tk=tk),
        grid=(n_heads, s // tq),
        in_specs=[pl.BlockSpec((tq, HEAD_DIM), lambda h, i: (i, q_col + h)),
                  pl.BlockSpec((s, HEAD_DIM), lambda h, i: (0, k_col + h)),
                  pl.BlockSpec((s, HEAD_DIM), lambda h, i: (0, v_col + h)),
                  pl.BlockSpec((2 * tk, 2 * tk), lambda h, i: (0, 0))],
        out_specs=pl.BlockSpec((tq, HEAD_DIM), lambda h, i: (i, h)),
        out_shape=jax.ShapeDtypeStruct((s, n_heads * HEAD_DIM), BF16),
        scratch_shapes=[pltpu.VMEM((s, HEAD_DIM), BF16), pltpu.VMEM((s, HEAD_DIM), BF16)],
        compiler_params=_params("parallel", "arbitrary"),
        name="sb_prompt",
    )(p, p, p, _cumsum_matrix(tk))


def _sb_sample_kernel(q_ref, kn_ref, vn_ref, kc_ref, vc_ref, u_ref, o_ref, acc_ref, carry_ref, kb_ref, vb_ref, *,
                      t, tk, ck, n_heads):
    c = pl.program_id(1)
    heads = [slice(h * HEAD_DIM, (h + 1) * HEAD_DIM) for h in range(n_heads)]
    qs = [(q_ref[:, sl] * (SCALE * LOG2E)).astype(BF16) for sl in heads]
    u = u_ref[...]

    @pl.when(c == 0)
    def _():
        reps = 2 * tk // t
        k2s = [jnp.concatenate([kn_ref[:, sl].astype(BF16)] * reps, axis=0) for sl in heads]
        v2s = [jnp.concatenate([vn_ref[:, sl].astype(BF16)] * reps, axis=0) for sl in heads]
        before = (lax.broadcasted_iota(jnp.int32, (t, 2 * tk), 1) < lax.broadcasted_iota(jnp.int32, (t, 2 * tk), 0))
        acc, carry = _sb_sweep_step(qs, k2s, v2s, _rows([before] * n_heads), u,
                                    jnp.zeros(acc_ref.shape, F32), jnp.zeros(carry_ref.shape, F32))
        acc_ref[...] = acc
        carry_ref[...] = carry

    for h in range(n_heads):
        kb_ref[h] = kc_ref[pl.ds(h, ck, stride=n_heads), :].astype(BF16)
        vb_ref[h] = vc_ref[pl.ds(h, ck, stride=n_heads), :].astype(BF16)
    n_steps = ck // (2 * tk)

    def body(s, state):
        off = pl.multiple_of((n_steps - 1 - s) * 2 * tk, 2 * tk)
        k2s = [kb_ref[h, pl.ds(off, 2 * tk), :] for h in range(n_heads)]
        v2s = [vb_ref[h, pl.ds(off, 2 * tk), :] for h in range(n_heads)]
        return _sb_sweep_step(qs, k2s, v2s, None, u, *state)

    acc, carry = lax.fori_loop(0, n_steps, body, (acc_ref[...], carry_ref[...]), unroll=2)
    acc_ref[...] = acc
    carry_ref[...] = carry

    @pl.when(c == pl.num_programs(1) - 1)
    def _():
        for h, sl in enumerate(heads):
            o_ref[:, sl] = acc_ref[h * t:(h + 1) * t, :].astype(o_ref.dtype)


def sb_sample(p, cache_k, cache_v, row0, nb, t, q_col, k_col, v_col):
    _, past, n_heads, _ = cache_k.shape
    tk = HEAD_DIM
    w = n_heads * HEAD_DIM
    ck = _tile(past, 1024, 2 * tk)
    n_c = past // ck
    assert (2 * tk) % t == 0 and row0 % t == 0
    r0 = row0 // t
    cache_spec = pl.BlockSpec((None, ck * n_heads, HEAD_DIM), lambda b, c: (b, n_c - 1 - c, 0))
    cache_k = cache_k.reshape(nb, past * n_heads, HEAD_DIM)
    cache_v = cache_v.reshape(nb, past * n_heads, HEAD_DIM)
    return pl.pallas_call(
        functools.partial(_sb_sample_kernel, t=t, tk=tk, ck=ck, n_heads=n_heads),
        grid=(nb, n_c),
        in_specs=[pl.BlockSpec((t, w), lambda b, c: (r0 + b, q_col)),
                  pl.BlockSpec((t, w), lambda b, c: (r0 + b, k_col)),
                  pl.BlockSpec((t, w), lambda b, c: (r0 + b, v_col)),
                  cache_spec, cache_spec,
                  pl.BlockSpec((2 * tk, 2 * tk), lambda b, c: (0, 0))],
        out_specs=pl.BlockSpec((t, w), lambda b, c: (b, 0)),
        out_shape=jax.ShapeDtypeStruct((nb * t, w), BF16),
        scratch_shapes=[pltpu.VMEM((n_heads * t, HEAD_DIM), F32), pltpu.VMEM((n_heads * t, tk), F32),
                        pltpu.VMEM((n_heads, ck, HEAD_DIM), BF16), pltpu.VMEM((n_heads, ck, HEAD_DIM), BF16)],
        compiler_params=_params("parallel", "arbitrary"),
        name="sb_sample",
    )(p, p, p, cache_k, cache_v, _cumsum_matrix(tk))


def _retention_log_decay(n_heads):
    return [float(np.log1p(-np.float32(2.0 ** (-5.0 - h)))) for h in range(n_heads)]


def _retention_kernel(q_ref, k_ref, v_ref, g_ref, cos_ref, sin_ref, gn_ref, s0_ref, o_ref, sout_ref,
                      state_ref, dec_ref, *, t, n_chunks, log_g):
    c = pl.program_id(1)

    @pl.when(c == 0)
    def _():
        state_ref[...] = s0_ref[...]
        rel = (lax.broadcasted_iota(jnp.int32, (t, t), 0) - lax.broadcasted_iota(jnp.int32, (t, t), 1)).astype(F32)
        for h, lg in enumerate(log_g):
            dec_ref[h] = jnp.where(rel >= 0, jnp.exp(lg * jnp.maximum(rel, 0.0)), 0.0)

    cos, sin = cos_ref[...], sin_ref[...]
    idx = lax.broadcasted_iota(jnp.int32, (t, HEAD_DIM), 0).astype(F32)
    for h, lg in enumerate(log_g):
        sl = slice(h * HEAD_DIM, (h + 1) * HEAD_DIM)
        qh, kh = q_ref[:, sl], k_ref[:, sl]
        qh = qh * cos + pltpu.roll(qh, HEAD_DIM // 2, 1) * sin
        kh = (kh * cos + pltpu.roll(kh, HEAD_DIM // 2, 1) * sin) * SCALE
        qb, vb = qh.astype(BF16), v_ref[:, sl].astype(BF16)
        scores = _dot_nt(qb, kh.astype(BF16)) * dec_ref[h]
        state = state_ref[h]
        o = _dot(scores.astype(BF16), vb) + _dot(qb, state.astype(BF16)) * jnp.exp((idx + 1.0) * lg)
        k_dec = (kh * jnp.exp((t - 1.0 - idx) * lg)).astype(BF16)
        state_ref[h] = float(np.exp(np.float32(t * lg))) * state + _dot_tn(k_dec, vb)
        o = o * lax.rsqrt(jnp.mean(o * o, axis=-1, keepdims=True) + RMS_EPS) * gn_ref[:, sl]
        o_ref[:, sl] = (o * _silu(g_ref[:, sl])).astype(o_ref.dtype)

    @pl.when(c == n_chunks - 1)
    def _():
        sout_ref[...] = state_ref[...]


def retention(p, cos2, sin2, ret_norm_g, state0, row0, nb, seq, t, n_heads, q_col, k_col, v_col, g_col):
    w = n_heads * HEAD_DIM
    n_chunks = seq // t
    assert row0 % t == 0 and seq % t == 0
    r0 = row0 // t

    def rows(col):
        return pl.BlockSpec((t, w), lambda b, c: (r0 + b * n_chunks + c, col))

    return pl.pallas_call(
        functools.partial(_retention_kernel, t=t, n_chunks=n_chunks, log_g=_retention_log_decay(n_heads)),
        grid=(nb, n_chunks),
        in_specs=[rows(q_col), rows(k_col), rows(v_col), rows(g_col),
                  pl.BlockSpec((t, HEAD_DIM), lambda b, c: (c, 0)),
                  pl.BlockSpec((t, HEAD_DIM), lambda b, c: (c, 0)),
                  pl.BlockSpec((1, w), lambda b, c: (0, 0)),
                  pl.BlockSpec((None, n_heads, HEAD_DIM, HEAD_DIM), lambda b, c: (b, 0, 0, 0))],
        out_specs=[pl.BlockSpec((t, w), lambda b, c: (b * n_chunks + c, 0)),
                   pl.BlockSpec((None, n_heads, HEAD_DIM, HEAD_DIM), lambda b, c: (b, 0, 0, 0))],
        out_shape=[jax.ShapeDtypeStruct((nb * seq, w), BF16),
                   jax.ShapeDtypeStruct((nb, n_heads, HEAD_DIM, HEAD_DIM), F32)],
        scratch_shapes=[pltpu.VMEM((n_heads, HEAD_DIM, HEAD_DIM), F32), pltpu.VMEM((n_heads, t, t), F32)],
        compiler_params=_params("parallel", "arbitrary"),
        name="retention",
    )(p, p, p, p, cos2, sin2, ret_norm_g.reshape(1, w).astype(F32), state0)


def _rope_tables(pos):
    half = HEAD_DIM // 2
    inv = ROPE_BASE ** (-jnp.arange(half, dtype=F32) / half)
    ang = pos.astype(F32)[:, None] * inv[None, :]
    cos, sin = jnp.cos(ang), jnp.sin(ang)
    return jnp.concatenate([cos, cos], axis=1), jnp.concatenate([-sin, sin], axis=1)


def _band_bias_table(rel_bias, tq):
    n_heads = rel_bias.shape[0]
    w = BAND_WINDOW + tq
    n = tq + w - 1
    dist = BAND_WINDOW + tq - 1 - np.arange(n)
    vec = rel_bias.astype(F32)[:, np.clip(dist, -MAX_REL, MAX_REL) + MAX_REL]
    vec = jnp.roll(vec, -(tq - 1), axis=1)
    bias = jnp.tile(vec, (1, tq))[:, :tq * (n - 1)].reshape(n_heads, tq, n - 1)[:, :, :w]
    t = np.arange(tq)[:, None]
    j = np.arange(w)[None, :]
    lo = (t // CHUNK) * CHUNK
    in_band = (j >= lo) & (j < lo + BAND_WINDOW + CHUNK)
    return jnp.where(jnp.asarray(in_band)[None], bias, NEG_INF)


def _softmax_pv(scores, values):
    m = scores[0].max(axis=1, keepdims=True)
    for s in scores[1:]:
        m = jnp.maximum(m, s.max(axis=1, keepdims=True))
    ps = [jnp.exp(s - m) for s in scores]
    denom = ps[0].sum(axis=1, keepdims=True)
    for p in ps[1:]:
        denom = denom + p.sum(axis=1, keepdims=True)
    o = _dot(ps[0].astype(BF16), values[0])
    for p, v in zip(ps[1:], values[1:]):
        o = o + _dot(p.astype(BF16), v)
    return o * (1.0 / denom)


BAND_HEADS_PER_STEP = 4


def _band_prompt_kernel(*refs, tq, n_kb, hb):
    q_ref = refs[0]
    k_refs, v_refs = refs[1:1 + n_kb], refs[1 + n_kb:1 + 2 * n_kb]
    b_ref, o_ref = refs[1 + 2 * n_kb], refs[2 + 2 * n_kb]
    i = pl.program_id(1)
    for h in range(hb):
        sl = slice(h * HEAD_DIM, (h + 1) * HEAD_DIM)
        q = (q_ref[:, sl] * SCALE).astype(BF16)
        scores, values = [], []
        for kb in range(n_kb):
            s = _dot_nt(q, k_refs[kb][:, sl].astype(BF16)) + b_ref[h, :, kb * tq:(kb + 1) * tq]
            scores.append(s + jnp.where(i - (n_kb - 1) + kb >= 0, 0.0, NEG_INF))
            values.append(v_refs[kb][:, sl].astype(BF16))
        o_ref[:, sl] = _softmax_pv(scores, values).astype(o_ref.dtype)


def band_prompt(p, rel_bias, s, n_heads, q_col, k_col, v_col):
    tq = _tile(s, 256)
    hb = BAND_HEADS_PER_STEP
    w = hb * HEAD_DIM
    assert BAND_WINDOW % tq == 0 and tq % CHUNK == 0 and n_heads % hb == 0
    assert q_col % hb == 0 and k_col % hb == 0 and v_col % hb == 0
    n_kb = BAND_WINDOW // tq + 1

    def kv_specs(col):
        return [pl.BlockSpec((tq, w), lambda h, i, kb=kb: (jnp.maximum(i - (n_kb - 1) + kb, 0), col // hb + h))
                for kb in range(n_kb)]

    return pl.pallas_call(
        functools.partial(_band_prompt_kernel, tq=tq, n_kb=n_kb, hb=hb),
        grid=(n_heads // hb, s // tq),
        in_specs=([pl.BlockSpec((tq, w), lambda h, i: (i, q_col // hb + h))] + kv_specs(k_col) + kv_specs(v_col)
                  + [pl.BlockSpec((hb, tq, BAND_WINDOW + tq), lambda h, i: (h, 0, 0))]),
        out_specs=pl.BlockSpec((tq, w), lambda h, i: (i, h)),
        out_shape=jax.ShapeDtypeStruct((s, n_heads * HEAD_DIM), BF16),
        compiler_params=_params("parallel", "parallel"),
        name="band_prompt",
    )(*([p] * (1 + 2 * n_kb)), _band_bias_table(rel_bias, tq))


def _band_sample_kernel(q_ref, kn_ref, vn_ref, kc_ref, vc_ref, b_ref, o_ref, ko_ref, vo_ref, *,
                        band_past, t, n_heads):
    keep = (band_past - t) * n_heads
    for h in range(n_heads):
        sl = slice(h * HEAD_DIM, (h + 1) * HEAD_DIM)
        q = (q_ref[:, sl] * SCALE).astype(BF16)
        kn, vn = kn_ref[:, sl], vn_ref[:, sl]
        kc = kc_ref[pl.ds(h, band_past, stride=n_heads), :].astype(BF16)
        vc = vc_ref[pl.ds(h, band_past, stride=n_heads), :].astype(BF16)
        s_c = _dot_nt(q, kc) + b_ref[h, :, :band_past]
        s_n = _dot_nt(q, kn.astype(BF16)) + b_ref[h, :, band_past:]
        o_ref[:, sl] = _softmax_pv([s_c, s_n], [vc, vn.astype(BF16)]).astype(o_ref.dtype)
        ko_ref[pl.ds(keep + h, t, stride=n_heads), :] = kn
        vo_ref[pl.ds(keep + h, t, stride=n_heads), :] = vn
    ko_ref[:keep] = kc_ref[t * n_heads:]
    vo_ref[:keep] = vc_ref[t * n_heads:]


def band_sample(p, cache_k, cache_v, rel_bias, row0, nb, t, q_col, k_col, v_col):
    _, band_past, n_heads, _ = cache_k.shape
    w = n_heads * HEAD_DIM
    assert t == CHUNK and band_past == BAND_WINDOW and row0 % t == 0
    r0 = row0 // t
    cache_spec = pl.BlockSpec((None, band_past * n_heads, HEAD_DIM), lambda b: (b, 0, 0))
    cache_shape = jax.ShapeDtypeStruct((nb, band_past * n_heads, HEAD_DIM), F32)
    cache_k = cache_k.reshape(cache_shape.shape)
    cache_v = cache_v.reshape(cache_shape.shape)
    o, k_out, v_out = pl.pallas_call(
        functools.partial(_band_sample_kernel, band_past=band_past, t=t, n_heads=n_heads),
        grid=(nb,),
        in_specs=[pl.BlockSpec((t, w), lambda b: (r0 + b, q_col)),
                  pl.BlockSpec((t, w), lambda b: (r0 + b, k_col)),
                  pl.BlockSpec((t, w), lambda b: (r0 + b, v_col)),
                  cache_spec, cache_spec,
                  pl.BlockSpec((n_heads, t, band_past + t), lambda b: (0, 0, 0))],
        out_specs=[pl.BlockSpec((t, w), lambda b: (b, 0)), cache_spec, cache_spec],
        out_shape=[jax.ShapeDtypeStruct((nb * t, w), BF16), cache_shape, cache_shape],
        compiler_params=_params("parallel"),
        name="band_sample",
    )(p, p, p, cache_k.astype(F32), cache_v.astype(F32), _band_bias_table(rel_bias, t))
    out_4d = (nb, band_past, n_heads, HEAD_DIM)
    return o, k_out.reshape(out_4d), v_out.reshape(out_4d)


def _cross_attn_kernel(q_ref, mk_ref, mv_ref, o_ref, *, n_heads):
    for h in range(n_heads):
        sl = slice(h * HEAD_DIM, (h + 1) * HEAD_DIM)
        q = (q_ref[:, sl] * SCALE).astype(BF16)
        s = _dot_nt(q, mk_ref[:, sl].astype(BF16))
        o_ref[:, sl] = _softmax_pv([s], [mv_ref[:, sl].astype(BF16)]).astype(o_ref.dtype)


def cross_attn(q, mk, mv, row0, nb, seq):
    n_mem, w = mk.shape[1], mk.shape[2]
    tq = _tile(seq, 512)
    n_t = seq // tq
    assert row0 % tq == 0
    r0 = row0 // tq
    return pl.pallas_call(
        functools.partial(_cross_attn_kernel, n_heads=w // HEAD_DIM),
        grid=(nb, n_t),
        in_specs=[pl.BlockSpec((tq, w), lambda b, i: (r0 + b * n_t + i, 0)),
                  pl.BlockSpec((None, n_mem, w), lambda b, i: (b, 0, 0)),
                  pl.BlockSpec((None, n_mem, w), lambda b, i: (b, 0, 0))],
        out_specs=pl.BlockSpec((tq, w), lambda b, i: (b * n_t + i, 0)),
        out_shape=jax.ShapeDtypeStruct((nb * seq, w), BF16),
        compiler_params=_params("parallel", "parallel"),
        name="cross_attn",
    )(q, mk, mv)


def kernel(x_prompt, x_sample, cache_sb_k, cache_sb_v, state_ret, cache_band_k, cache_band_v, cache_mem_k, cache_mem_v, mem_prompt, w_in_ab, w_out_ab, ret_norm_g, w_qkv_band, w_out_band, rel_bias_band, norm_g, mem_norm_g, w_xq, w_xk, w_xv, w_xo, ffn_w_gate, ffn_w_up, ffn_w_down, moe_router, moe_w_gate, moe_w_up, moe_w_down, final_norm_g):
    bp, s, d = x_prompt.shape
    nb, t, _ = x_sample.shape
    assert bp == 1
    past = cache_sb_k.shape[1]
    band_past = cache_band_k.shape[1]
    h_sb = cache_sb_k.shape[2]
    h_ret = state_ret.shape[1]
    h_band = cache_band_k.shape[2]
    d_sb, d_ret = h_sb * HEAD_DIM, h_ret * HEAD_DIM
    n_mem = mem_prompt.shape[1]
    d_x = w_xq.shape[2]
    depth = w_xq.shape[0]
    d_ff = ffn_w_gate.shape[1]
    ms = nb * t
    assert d_sb == d_ret and 3 * d_sb + 4 * d_ret == w_in_ab.shape[1]

    x = jnp.concatenate([x_prompt.reshape(s, d), x_sample.reshape(ms, d)], axis=0)
    mem = mem_prompt.reshape(n_mem, d)
    bf = lambda a: a.astype(BF16)

    cos_p, sin_p = _rope_tables(jnp.arange(s, dtype=jnp.int32))
    cos_s, sin_s = _rope_tables(past + jnp.arange(t, dtype=jnp.int32))

    mem_k_list, mem_v_list = [], []
    for l in range(depth):
        mem_n = rmsnorm(mem, mem_norm_g[l], BF16)
        mk_p = matmul([(mem_n, 0, 0)], bf(w_xk[l]), d)
        mv_p = matmul([(mem_n, 0, 0)], bf(w_xv[l]), d)
        mem_k_list.append(mk_p.reshape(1, n_mem, H_X, HEAD_DIM))
        mem_v_list.append(mv_p.reshape(1, n_mem, H_X, HEAD_DIM))

        h = rmsnorm(x, norm_g[l, 0], BF16)
        if l % 2 == 0:
            p = matmul([(h, 0, 0)], bf(w_in_ab), d)
            o_sb_p = sb_prompt(p, s, h_sb, 0, h_sb, 2 * h_sb)
            o_sb_s = sb_sample(p, cache_sb_k, cache_sb_v, s, nb, t, 0, 1, 2)
            t_ret = _tile(s, 256, CHUNK)
            o_r_p, ret_state_prompt = retention(p, cos_p, sin_p, ret_norm_g, jnp.zeros((1,) + state_ret.shape[1:], F32),
                                                0, 1, s, t_ret, h_ret, 3, 4, 5, 6)
            o_r_s, ret_state_sample = retention(p, cos_s, sin_s, ret_norm_g, state_ret.astype(F32),
                                                s, nb, t, t, h_ret, 3, 4, 5, 6)
            o_sb = jnp.concatenate([o_sb_p, o_sb_s], axis=0)
            o_r = jnp.concatenate([o_r_p, o_r_s], axis=0)
            x = matmul([(o_sb, 0, 0), (o_r, 0, d_sb)], bf(w_out_ab), d_sb, residual=x)
            sb_k_prompt = p[:s, d_sb:2 * d_sb].reshape(1, s, h_sb, HEAD_DIM)
            sb_v_prompt = p[:s, 2 * d_sb:3 * d_sb].reshape(1, s, h_sb, HEAD_DIM)
            sb_k_sample = p[s:, d_sb:2 * d_sb].reshape(nb, t, h_sb, HEAD_DIM)
            sb_v_sample = p[s:, 2 * d_sb:3 * d_sb].reshape(nb, t, h_sb, HEAD_DIM)
        else:
            d_band = h_band * HEAD_DIM
            p = matmul([(h, 0, 0)], bf(w_qkv_band), d)
            o_p = band_prompt(p, rel_bias_band, s, h_band, 0, h_band, 2 * h_band)
            o_s, band_k_sample, band_v_sample = band_sample(p, cache_band_k, cache_band_v, rel_bias_band,
                                                            s, nb, t, 0, 1, 2)
            o = jnp.concatenate([o_p, o_s], axis=0)
            x = matmul([(o, 0, 0)], bf(w_out_band), d_band, residual=x)
            band_k_prompt = p[s - band_past:s, d_band:2 * d_band].reshape(1, band_past, h_band, HEAD_DIM)
            band_v_prompt = p[s - band_past:s, 2 * d_band:].reshape(1, band_past, h_band, HEAD_DIM)

        h = rmsnorm(x, norm_g[l, 1], BF16)
        q = matmul([(h, 0, 0)], bf(w_xq[l]), d)
        o_p = cross_attn(q, mk_p.reshape(1, n_mem, d_x), mv_p.reshape(1, n_mem, d_x), 0, 1, s)
        o_s = cross_attn(q, cache_mem_k[l].reshape(nb, n_mem, d_x), cache_mem_v[l].reshape(nb, n_mem, d_x), s, nb, t)
        o = jnp.concatenate([o_p, o_s], axis=0)
        x = matmul([(o, 0, 0)], bf(w_xo[l]), d_x, residual=x)

        h = rmsnorm(x, norm_g[l, 2], BF16)
        if l % 2 == 0:
            act = swiglu_up(h, bf(ffn_w_gate), bf(ffn_w_up), tn_pref=512)
            x = matmul([(act, 0, 0)], bf(ffn_w_down), d_ff, residual=x, tk_pref=2816)
        else:
            gates, idx = moe_route(x, norm_g[l, 2], moe_router)
            x = moe_top2(x, h, idx, gates, bf(moe_w_gate), bf(moe_w_up), bf(moe_w_down))

    y = rmsnorm(x, final_norm_g, F32)
    y_prompt = y[:s].reshape(1, s, d)
    y_sample = y[s:].reshape(nb, t, d)
    mem_k_prompt = jnp.stack(mem_k_list, axis=0)
    mem_v_prompt = jnp.stack(mem_v_list, axis=0)
    return (y_prompt, y_sample, sb_k_prompt, sb_v_prompt, sb_k_sample, sb_v_sample,
            ret_state_prompt, ret_state_sample, band_k_prompt, band_v_prompt,
            band_k_sample, band_v_sample, mem_k_prompt, mem_v_prompt)
```

```python
import functools

import numpy as np
import jax
import jax.numpy as jnp
from jax import lax
from jax.experimental import pallas as pl
from jax.experimental.pallas import tpu as pltpu

F32 = jnp.float32
BF16 = jnp.bfloat16

HEAD_DIM = 128
CHUNK = 64
N_BAND_CHUNKS = 8
BAND_WINDOW = N_BAND_CHUNKS * CHUNK
MAX_REL = 128
H_X = 4
TOP_K = 2
RMS_EPS = 1e-6
ROPE_BASE = 10000.0
NEG_INF = -1e30
SCALE = HEAD_DIM ** -0.5
LOG2E = 1.4426950408889634

VMEM_LIMIT_BYTES = 56 * 1024 * 1024


def _params(*sem):
    return pltpu.CompilerParams(dimension_semantics=sem, vmem_limit_bytes=VMEM_LIMIT_BYTES)


def _tile(n, pref, mult=8):
    t = min(n, pref)
    while t > mult and (n % t or t % mult):
        t -= mult
    assert n % t == 0, (n, pref)
    return t


def _dot(a, b):
    return jnp.dot(a, b, preferred_element_type=F32)


def _dot_nt(a, b):
    return lax.dot_general(a, b, (((1,), (1,)), ((), ())), preferred_element_type=F32)


def _dot_tn(a, b):
    return lax.dot_general(a, b, (((0,), (0,)), ((), ())), preferred_element_type=F32)


def _silu(a):
    return a * (1.0 / (1.0 + jnp.exp(-a)))


def _rows(parts):
    return parts[0] if len(parts) == 1 else jnp.concatenate(parts, axis=0)


def _rmsnorm_kernel(x_ref, g_ref, o_ref):
    x = x_ref[...]
    y = x * lax.rsqrt(jnp.mean(x * x, axis=-1, keepdims=True) + RMS_EPS) * g_ref[...]
    o_ref[...] = y.astype(o_ref.dtype)


def rmsnorm(x, g, out_dtype, row0=0, rows=None):
    d = x.shape[1]
    m = x.shape[0] - row0 if rows is None else rows
    tm = _tile(m, 512)
    assert row0 % tm == 0
    return pl.pallas_call(
        _rmsnorm_kernel,
        grid=(m // tm,),
        in_specs=[pl.BlockSpec((tm, d), lambda i: (row0 // tm + i, 0)),
                  pl.BlockSpec((1, d), lambda i: (0, 0))],
        out_specs=pl.BlockSpec((tm, d), lambda i: (i, 0)),
        out_shape=jax.ShapeDtypeStruct((m, d), out_dtype),
        compiler_params=_params("parallel"),
        name="rmsnorm",
    )(x, g.reshape(1, d).astype(F32))


def _mm_kernel(*refs, n_parts, has_res, nk):
    xs, ws = refs[:n_parts], refs[n_parts:2 * n_parts]
    pos = 2 * n_parts
    res_ref = refs[pos] if has_res else None
    pos += int(has_res)
    o_ref = refs[pos]
    part = _dot(xs[0][...], ws[0][...])
    for x_ref, w_ref in zip(xs[1:], ws[1:]):
        part = part + _dot(x_ref[...], w_ref[...])
    if nk == 1:
        if has_res:
            part = res_ref[...] + part
        o_ref[...] = part.astype(o_ref.dtype)
        return
    acc_ref = refs[pos + 1]
    k = pl.program_id(2)

    @pl.when(k == 0)
    def _():
        acc_ref[...] = part

    @pl.when(k > 0)
    def _():
        acc_ref[...] += part

    @pl.when(k == nk - 1)
    def _():
        out = acc_ref[...]
        if has_res:
            out = res_ref[...] + out
        o_ref[...] = out.astype(o_ref.dtype)


def matmul(parts, w, k_part, *, col_off=0, n_cols=None, out_dtype=F32, residual=None,
           tm_pref=1024, tn_pref=512, tk_pref=2048):
    m = parts[0][0].shape[0]
    n_cols = w.shape[1] - col_off if n_cols is None else n_cols
    tm = _tile(m, tm_pref)
    tn = _tile(n_cols, tn_pref, 128)
    tk = _tile(k_part, tk_pref, 128)
    nk = k_part // tk
    assert col_off % tn == 0
    in_specs, args = [], []
    for x, xo, _ in parts:
        assert xo % tk == 0
        in_specs.append(pl.BlockSpec((tm, tk), lambda i, j, k, xo=xo: (i, xo // tk + k)))
        args.append(x)
    for _, _, wo in parts:
        assert wo % tk == 0
        in_specs.append(pl.BlockSpec((tk, tn), lambda i, j, k, wo=wo: (wo // tk + k, col_off // tn + j)))
        args.append(w)
    if residual is not None:
        in_specs.append(pl.BlockSpec((tm, tn), lambda i, j, k: (i, j)))
        args.append(residual)
    return pl.pallas_call(
        functools.partial(_mm_kernel, n_parts=len(parts), has_res=residual is not None, nk=nk),
        grid=(m // tm, n_cols // tn, nk),
        in_specs=in_specs,
        out_specs=pl.BlockSpec((tm, tn), lambda i, j, k: (i, j)),
        out_shape=jax.ShapeDtypeStruct((m, n_cols), out_dtype),
        scratch_shapes=[pltpu.VMEM((tm, tn), F32)] if nk > 1 else [],
        compiler_params=_params("parallel", "parallel", "arbitrary"),
        name="matmul",
    )(*args)


def _swiglu_up_kernel(x_ref, wg_ref, wu_ref, o_ref):
    x = x_ref[...]
    o_ref[...] = (_silu(_dot(x, wg_ref[...])) * _dot(x, wu_ref[...])).astype(o_ref.dtype)


def swiglu_up(x, wg, wu, *, tn_pref):
    m, d = x.shape
    f = wg.shape[1]
    tm = _tile(m, 1024)
    tn = _tile(f, tn_pref, 128)
    return pl.pallas_call(
        _swiglu_up_kernel,
        grid=(m // tm, f // tn),
        in_specs=[pl.BlockSpec((tm, d), lambda i, j: (i, 0)),
                  pl.BlockSpec((d, tn), lambda i, j: (0, j)),
                  pl.BlockSpec((d, tn), lambda i, j: (0, j))],
        out_specs=pl.BlockSpec((tm, tn), lambda i, j: (i, j)),
        out_shape=jax.ShapeDtypeStruct((m, f), BF16),
        compiler_params=_params("parallel", "parallel"),
        name="swiglu_up",
    )(x, wg, wu)


def _router_kernel(x_ref, g_ref, r_ref, o_ref, *, n_experts):
    x = x_ref[...]
    hn = x * lax.rsqrt(jnp.mean(x * x, axis=-1, keepdims=True) + RMS_EPS) * g_ref[...]
    logits = jnp.dot(hn, r_ref[...], preferred_element_type=F32, precision=lax.Precision.HIGHEST)
    lane = lax.broadcasted_iota(jnp.int32, logits.shape, 1)
    n_lanes = logits.shape[1]
    lg = jnp.where(lane < n_experts, logits, -jnp.inf)
    m1 = jnp.max(lg, axis=1, keepdims=True)
    i1 = jnp.min(jnp.where(lg == m1, lane, n_lanes), axis=1, keepdims=True)
    lg2 = jnp.where(lane == i1, -jnp.inf, lg)
    m2 = jnp.max(lg2, axis=1, keepdims=True)
    i2 = jnp.min(jnp.where(lg2 == m2, lane, n_lanes), axis=1, keepdims=True)
    e2 = jnp.exp(m2 - m1)
    inv = 1.0 / (1.0 + e2)
    o_ref[...] = (jnp.where(lane == 0, inv, 0.0) + jnp.where(lane == 1, e2 * inv, 0.0)
                  + jnp.where(lane == 2, i1.astype(F32), 0.0) + jnp.where(lane == 3, i2.astype(F32), 0.0))


def moe_route(x, g, router):
    m, d = x.shape
    n_e = router.shape[1]
    tm = _tile(m, 512)
    r_pad = jnp.zeros((d, 128), F32).at[:, :n_e].set(router.astype(F32))
    out = pl.pallas_call(
        functools.partial(_router_kernel, n_experts=n_e),
        grid=(m // tm,),
        in_specs=[pl.BlockSpec((tm, d), lambda i: (i, 0)),
                  pl.BlockSpec((1, d), lambda i: (0, 0)),
                  pl.BlockSpec((d, 128), lambda i: (0, 0))],
        out_specs=pl.BlockSpec((tm, 128), lambda i: (i, 0)),
        out_shape=jax.ShapeDtypeStruct((m, 128), F32),
        compiler_params=_params("parallel"),
        name="moe_router",
    )(x, g.reshape(1, d).astype(F32), r_pad)
    return out[:, :TOP_K], out[:, TOP_K:2 * TOP_K].astype(jnp.int32)


MOE_ROW_TILE = 256
MOE_GATHER_CHUNK = 512
MOE_COMBINE_TILE = 1024


def _i32(a):
    return a.astype(jnp.int32)


def _count_le(ends, v):
    return jnp.sum(_i32(ends[None, :] <= v[:, None]), axis=1, dtype=jnp.int32)


def _moe_plan(idx, gates, n_e, tr, tc, tt):
    m = idx.shape[0]
    assert (TOP_K * m) % tr == 0 and m % tc == 0 and m % tt == 0
    n_tiles = TOP_K * m // tr + n_e
    n_rows = n_tiles * tr
    routed = jnp.zeros((m, n_e), jnp.int32)
    for k in range(TOP_K):
        routed = routed + _i32(idx[:, k:k + 1] == jnp.arange(n_e, dtype=jnp.int32)[None, :])
    csum = jnp.cumsum(routed, axis=0, dtype=jnp.int32)
    rank = csum - routed
    tiles_e = (csum[-1] + tr - 1) // tr
    tile_end = jnp.cumsum(tiles_e, dtype=jnp.int32)
    row_start = (tile_end - tiles_e) * tr
    dest = row_start[idx] + jnp.take_along_axis(rank, idx, axis=1)
    gate_rows = jnp.zeros((n_rows,), F32).at[dest.reshape(-1)].set(gates.reshape(-1), unique_indices=True)
    tile_ids = jnp.arange(n_tiles, dtype=jnp.int32)
    tile_valid = tile_ids < tile_end[-1]
    tile_expert = jnp.minimum(_count_le(tile_end, tile_ids), n_e - 1)

    def visits(first_chunk, n_visits, n_max, owner_of):
        end = jnp.cumsum(n_visits, dtype=jnp.int32)
        start = end - n_visits
        v = jnp.arange(n_max, dtype=jnp.int32)
        live = v < end[-1]
        slot = jnp.minimum(_count_le(end, v), n_visits.shape[0] - 1)
        owner = owner_of(slot)
        chunk = first_chunk[slot] + v - start[slot]
        last = jnp.maximum(end[-1] - 1, 0)
        owner = jnp.where(live, owner, owner[last])
        chunk = jnp.where(live, chunk, chunk[last])
        first = live & ((v == 0) | (owner != jnp.roll(owner, 1)))
        return _i32(owner), _i32(chunk), _i32(first) + 2 * _i32(live)

    rank0 = (tile_ids - (tile_end - tiles_e)[tile_expert]) * tr
    rank1 = jnp.minimum(rank0 + tr, csum[-1][tile_expert]) - 1
    csum_e = csum.T[tile_expert]
    first_tok = jnp.sum(_i32(csum_e <= rank0[:, None]), axis=1, dtype=jnp.int32)
    last_tok = jnp.sum(_i32(csum_e <= rank1[:, None]), axis=1, dtype=jnp.int32)
    c_first = jnp.where(tile_valid, first_tok, 0) // tc
    c_last = jnp.where(tile_valid, last_tok, 0) // tc
    g_plan = visits(c_first, c_last - c_first + 1, n_tiles + n_e * (m // tc - 1), lambda s: s)
    n_tt = m // tt
    before = jnp.concatenate([jnp.zeros((1, n_e), jnp.int32), csum[tt - 1::tt]], axis=0)
    lo = row_start[None, :] + before[:-1]
    hi = row_start[None, :] + before[1:]
    n_vis = jnp.where(hi > lo, (hi - 1) // tr - lo // tr + 1, 0)
    c_plan = visits((lo // tr).reshape(-1), n_vis.reshape(-1), n_tt * n_e + n_tiles - 1, lambda s: s // n_e)
    return dict(n_tiles=n_tiles, dest=dest, gate_rows=gate_rows.reshape(n_rows, 1),
                tile_expert=tile_expert, tile_valid=_i32(tile_valid), gather=g_plan, combine=c_plan)


def _moe_select(dest_ref, row0, n_rows):
    row = row0 + lax.broadcasted_iota(jnp.int32, (dest_ref.shape[0], n_rows), 1)
    hit = row == dest_ref[:, 0:1]
    for k in range(1, TOP_K):
        hit = hit | (row == dest_ref[:, k:k + 1])
    return jnp.where(hit, 1.0, 0.0).astype(BF16)


def _moe_gather_kernel(vt_ref, vc_ref, vf_ref, dest_ref, h_ref, o_ref):
    v = pl.program_id(0)
    flags = vf_ref[v]
    tr = o_ref.shape[0]

    @pl.when(flags % 2 == 1)
    def _():
        o_ref[...] = jnp.zeros_like(o_ref)

    @pl.when(flags >= 2)
    def _():
        sel = _moe_select(dest_ref, vt_ref[v] * tr, tr)
        o_ref[...] += _dot_tn(sel, h_ref[...]).astype(o_ref.dtype)


def _moe_up_kernel(te_ref, tv_ref, x_ref, wg_ref, wu_ref, o_ref):
    j = pl.program_id(1)

    @pl.when(tv_ref[j] != 0)
    def _():
        x = x_ref[...]
        o_ref[...] = (_silu(_dot(x, wg_ref[...])) * _dot(x, wu_ref[...])).astype(o_ref.dtype)

    @pl.when(tv_ref[j] == 0)
    def _():
        o_ref[...] = jnp.zeros_like(o_ref)


def _moe_down_kernel(te_ref, a_ref, wd_ref, g_ref, o_ref):
    o_ref[...] = (_dot(a_ref[...], wd_ref[...]) * g_ref[...]).astype(o_ref.dtype)


def _moe_combine_kernel(vi_ref, vc_ref, vf_ref, dest_ref, y_ref, x_ref, o_ref):
    v = pl.program_id(0)
    flags = vf_ref[v]
    tr = y_ref.shape[0]

    @pl.when(flags % 2 == 1)
    def _():
        o_ref[...] = x_ref[...]

    @pl.when(flags >= 2)
    def _():
        sel = _moe_select(dest_ref, vc_ref[v] * tr, tr)
        o_ref[...] += _dot(sel, y_ref[...])


def moe_top2(x, h, idx, gates, wg, wu, wd):
    m, d = h.shape
    n_e, _, f = wg.shape
    tr, tc, tt = MOE_ROW_TILE, _tile(m, MOE_GATHER_CHUNK, 16), _tile(m, MOE_COMBINE_TILE)
    plan =_moe_plan(idx, gates, n_e, tr, tc, tt)
    n_tiles = plan["n_tiles"]
    n_rows = n_tiles * tr

    vt, vc, vf = plan["gather"]
    xs = pl.pallas_call(
        _moe_gather_kernel,
        grid_spec=pltpu.PrefetchScalarGridSpec(
            num_scalar_prefetch=3, grid=(vt.shape[0],),
            in_specs=[pl.BlockSpec((tc, TOP_K), lambda v, vt, vc, vf: (vc[v], 0)),
                      pl.BlockSpec((tc, d), lambda v, vt, vc, vf: (vc[v], 0))],
            out_specs=pl.BlockSpec((tr, d), lambda v, vt, vc, vf: (vt[v], 0))),
        out_shape=jax.ShapeDtypeStruct((n_rows, d), BF16),
        compiler_params=_params("arbitrary"),
        name="moe_gather",
    )(vt, vc, vf, plan["dest"], h)

    tf = _tile(f, 1408, 128)
    act = pl.pallas_call(
        _moe_up_kernel,
        grid_spec=pltpu.PrefetchScalarGridSpec(
            num_scalar_prefetch=2, grid=(f // tf, n_tiles),
            in_specs=[pl.BlockSpec((tr, d), lambda c, j, te, tv: (j, 0)),
                      pl.BlockSpec((None, d, tf), lambda c, j, te, tv: (te[j], 0, c)),
                      pl.BlockSpec((None, d, tf), lambda c, j, te, tv: (te[j], 0, c))],
            out_specs=pl.BlockSpec((tr, tf), lambda c, j, te, tv: (j, c))),
        out_shape=jax.ShapeDtypeStruct((n_rows, f), BF16),
        compiler_params=_params("parallel", "parallel"),
        name="moe_up",
    )(plan["tile_expert"], plan["tile_valid"], xs, wg, wu)

    y = pl.pallas_call(
        _moe_down_kernel,
        grid_spec=pltpu.PrefetchScalarGridSpec(
            num_scalar_prefetch=1, grid=(n_tiles,),
            in_specs=[pl.BlockSpec((tr, f), lambda j, te: (j, 0)),
                      pl.BlockSpec((None, f, d), lambda j, te: (te[j], 0, 0)),
                      pl.BlockSpec((tr, 1), lambda j, te: (j, 0))],
            out_specs=pl.BlockSpec((tr, d), lambda j, te: (j, 0))),
        out_shape=jax.ShapeDtypeStruct((n_rows, d), BF16),
        compiler_params=_params("parallel"),
        name="moe_down",
    )(plan["tile_expert"], act, wd, plan["gate_rows"])

    vi, vc, vf = plan["combine"]
    return pl.pallas_call(
        _moe_combine_kernel,
        grid_spec=pltpu.PrefetchScalarGridSpec(
            num_scalar_prefetch=3, grid=(vi.shape[0],),
            in_specs=[pl.BlockSpec((tt, TOP_K), lambda v, vi, vc, vf: (vi[v], 0)),
                      pl.BlockSpec((tr, d), lambda v, vi, vc, vf: (vc[v], 0)),
                      pl.BlockSpec((tt, d), lambda v, vi, vc, vf: (vi[v], 0))],
            out_specs=pl.BlockSpec((tt, d), lambda v, vi, vc, vf: (vi[v], 0))),
        out_shape=jax.ShapeDtypeStruct((m, d), F32),
        compiler_params=_params("arbitrary"),
        name="moe_combine",
    )(vi, vc, vf, plan["dest"], y, x)


def _cumsum_matrix(tk):
    r = np.arange(2 * tk)[:, None]
    c = np.arange(2 * tk)[None, :]
    return jnp.asarray((r > c).astype(np.float32), dtype=BF16)


def _sb_sweep_step(qs, k2s, v2s, before, u, acc, carry):
    tq = qs[0].shape[0]
    tk = k2s[0].shape[0] // 2
    z = _rows([_dot_nt(q, k2) for q, k2 in zip(qs, k2s)])
    neg_l = jnp.maximum(z, 0.0) + jnp.log2(1.0 + jnp.exp2(-jnp.abs(z)))
    log_b = z - neg_l
    if before is not None:
        neg_l = jnp.where(before, neg_l, 0.0)
    neg_lb = neg_l.astype(BF16)
    after = _dot(neg_lb, u)
    w = jnp.exp2(log_b - (after + jnp.concatenate([carry, carry], axis=1)))
    if before is not None:
        w = jnp.where(before, w, 0.0)
    w = w.astype(BF16)
    acc = acc + _rows([_dot(w[n * tq:(n + 1) * tq], v2) for n, v2 in enumerate(v2s)])
    total = after[:, :1] + neg_lb[:, :1].astype(F32)
    return acc, carry + total


def _sb_prompt_kernel(q_ref, k_ref, v_ref, u_ref, o_ref, kb_ref, vb_ref, *, tq, tk):
    i = pl.program_id(1)

    @pl.when(i == 0)
    def _():
        kb_ref[...] = k_ref[...].astype(BF16)
        vb_ref[...] = v_ref[...].astype(BF16)

    q = (q_ref[...] * (SCALE * LOG2E)).astype(BF16)
    u = u_ref[...]
    steps_per_tile = tq // (2 * tk)
    state = (jnp.zeros((tq, HEAD_DIM), F32), jnp.zeros((tq, tk), F32))

    def step(off, r0, before, state):
        new = _sb_sweep_step([q[r0:]], [kb_ref[pl.ds(off, 2 * tk), :]], [vb_ref[pl.ds(off, 2 * tk), :]],
                             before, u, *[a[r0:] for a in state])
        return tuple(jnp.concatenate([a[:r0], b], axis=0) if r0 else b for a, b in zip(state, new))

    for r0 in reversed(range(0, tq, 2 * tk)):
        before = (lax.broadcasted_iota(jnp.int32, (tq - r0, 2 * tk), 1)
                  < lax.broadcasted_iota(jnp.int32, (tq - r0, 2 * tk), 0))
        state = step(pl.multiple_of(i * tq + r0, 2 * tk), r0, before, state)
    n_steps = i * steps_per_tile
    acc, _ = lax.fori_loop(
        0, n_steps, lambda s, st: step(pl.multiple_of((n_steps - 1 - s) * 2 * tk, 2 * tk), 0, None, st), state)
    o_ref[...] = acc.astype(o_ref.dtype)


def sb_prompt(p, s, n_heads, q_col, k_col, v_col):
    tk = HEAD_DIM
    tq = _tile(s, 1024, 2 * tk)
    return pl.pallas_call(
        functools.partial(_sb_prompt_kernel, tq=tq, tk=tk),
        grid=(n_heads, s // tq),
        in_specs=[pl.BlockSpec((tq, HEAD_DIM), lambda h, i: (i, q_col + h)),
                  pl.BlockSpec((s, HEAD_DIM), lambda h, i: (0, k_col + h)),
                  pl.BlockSpec((s, HEAD_DIM), lambda h, i: (0, v_col + h)),
                  pl.BlockSpec((2 * tk, 2 * tk), lambda h, i: (0, 0))],
        out_specs=pl.BlockSpec((tq, HEAD_DIM), lambda h, i: (i, h)),
        out_shape=jax.ShapeDtypeStruct((p.shape[0], n_heads * HEAD_DIM), BF16),
        scratch_shapes=[pltpu.VMEM((s, HEAD_DIM), BF16), pltpu.VMEM((s, HEAD_DIM), BF16)],
        compiler_params=_params("parallel", "arbitrary"),
        name="sb_prompt",
    )(p, p, p, _cumsum_matrix(tk))


def _sb_sample_kernel(q_ref, kn_ref, vn_ref, kc_ref, vc_ref, u_ref, _into_ref, o_ref,
                      acc_ref, carry_ref, kb_ref, vb_ref, *, t, tk, ck, n_heads):
    c = pl.program_id(1)
    heads = [slice(h * HEAD_DIM, (h + 1) * HEAD_DIM) for h in range(n_heads)]
    qs = [(q_ref[:, sl] * (SCALE * LOG2E)).astype(BF16) for sl in heads]
    u = u_ref[...]

    @pl.when(c == 0)
    def _():
        reps = 2 * tk // t
        k2s = [jnp.concatenate([kn_ref[:, sl].astype(BF16)] * reps, axis=0) for sl in heads]
        v2s = [jnp.concatenate([vn_ref[:, sl].astype(BF16)] * reps, axis=0) for sl in heads]
        before = (lax.broadcasted_iota(jnp.int32, (t, 2 * tk), 1) < lax.broadcasted_iota(jnp.int32, (t, 2 * tk), 0))
        acc, carry = _sb_sweep_step(qs, k2s, v2s, _rows([before] * n_heads), u,
                                    jnp.zeros(acc_ref.shape, F32), jnp.zeros(carry_ref.shape, F32))
        acc_ref[...] = acc
        carry_ref[...] = carry

    for h in range(n_heads):
        kb_ref[h] = kc_ref[pl.ds(h, ck, stride=n_heads), :].astype(BF16)
        vb_ref[h] = vc_ref[pl.ds(h, ck, stride=n_heads), :].astype(BF16)
    n_steps = ck // (2 * tk)

    def body(s, state):
        off = pl.multiple_of((n_steps - 1 - s) * 2 * tk, 2 * tk)
        k2s = [kb_ref[h, pl.ds(off, 2 * tk), :] for h in range(n_heads)]
        v2s = [vb_ref[h, pl.ds(off, 2 * tk), :] for h in range(n_heads)]
        return _sb_sweep_step(qs, k2s, v2s, None, u, *state)

    acc, carry = lax.fori_loop(0, n_steps, body, (acc_ref[...], carry_ref[...]), unroll=2)
    acc_ref[...] = acc
    carry_ref[...] = carry

    @pl.when(c == pl.num_programs(1) - 1)
    def _():
        for h, sl in enumerate(heads):
            o_ref[:, sl] = acc_ref[h * t:(h + 1) * t, :].astype(o_ref.dtype)


def sb_sample(p, cache_k, cache_v, into, row0, nb, t, q_col, k_col, v_col):
    _, past, n_heads, _ = cache_k.shape
    tk = HEAD_DIM
    w = n_heads * HEAD_DIM
    ck = _tile(past, 1024, 2 * tk)
    n_c = past // ck
    assert (2 * tk) % t == 0 and row0 % t == 0
    r0 = row0 // t
    cache_spec = pl.BlockSpec((None, ck * n_heads, HEAD_DIM), lambda b, c: (b, n_c - 1 - c, 0))
    cache_k = cache_k.reshape(nb, past * n_heads, HEAD_DIM)
    cache_v = cache_v.reshape(nb, past * n_heads, HEAD_DIM)
    return pl.pallas_call(
        functools.partial(_sb_sample_kernel, t=t, tk=tk, ck=ck, n_heads=n_heads),
        grid=(nb, n_c),
        in_specs=[pl.BlockSpec((t, w), lambda b, c: (r0 + b, q_col)),
                  pl.BlockSpec((t, w), lambda b, c: (r0 + b, k_col)),
                  pl.BlockSpec((t, w), lambda b, c: (r0 + b, v_col)),
                  cache_spec, cache_spec,
                  pl.BlockSpec((2 * tk, 2 * tk), lambda b, c: (0, 0)),
                  pl.BlockSpec(memory_space=pl.ANY)],
        out_specs=pl.BlockSpec((t, w), lambda b, c: (r0 + b, 0)),
        out_shape=jax.ShapeDtypeStruct(into.shape, into.dtype),
        input_output_aliases={6: 0},
        scratch_shapes=[pltpu.VMEM((n_heads * t, HEAD_DIM), F32), pltpu.VMEM((n_heads * t, tk), F32),
                        pltpu.VMEM((n_heads, ck, HEAD_DIM), BF16), pltpu.VMEM((n_heads, ck, HEAD_DIM), BF16)],
        compiler_params=_params("parallel", "arbitrary"),
        name="sb_sample",
    )(p, p, p, cache_k, cache_v, _cumsum_matrix(tk), into)


def _retention_log_decay(n_heads):
    return [float(np.log1p(-np.float32(2.0 ** (-5.0 - h)))) for h in range(n_heads)]


def _retention_kernel(q_ref, k_ref, v_ref, g_ref, cos_ref, sin_ref, gn_ref, s0_ref, *rest, t, n_chunks, log_g):
    o_ref, sout_ref, state_ref, dec_ref = rest[-4:]
    c = pl.program_id(1)

    @pl.when(c == 0)
    def _():
        state_ref[...] = s0_ref[...]
        rel = (lax.broadcasted_iota(jnp.int32, (t, t), 0) - lax.broadcasted_iota(jnp.int32, (t, t), 1)).astype(F32)
        for h, lg in enumerate(log_g):
            dec_ref[h] = jnp.where(rel >= 0, jnp.exp(lg * jnp.maximum(rel, 0.0)), 0.0)

    cos, sin = cos_ref[...], sin_ref[...]
    idx = lax.broadcasted_iota(jnp.int32, (t, HEAD_DIM), 0).astype(F32)
    for h, lg in enumerate(log_g):
        sl = slice(h * HEAD_DIM, (h + 1) * HEAD_DIM)
        qh, kh = q_ref[:, sl], k_ref[:, sl]
        qh = qh * cos + pltpu.roll(qh, HEAD_DIM // 2, 1) * sin
        kh = (kh * cos + pltpu.roll(kh, HEAD_DIM // 2, 1) * sin) * SCALE
        qb, vb = qh.astype(BF16), v_ref[:, sl].astype(BF16)
        scores = _dot_nt(qb, kh.astype(BF16)) * dec_ref[h]
        state = state_ref[h]
        o = _dot(scores.astype(BF16), vb) + _dot(qb, state.astype(BF16)) * jnp.exp((idx + 1.0) * lg)
        k_dec = (kh * jnp.exp((t - 1.0 - idx) * lg)).astype(BF16)
        state_ref[h] = float(np.exp(np.float32(t * lg))) * state + _dot_tn(k_dec, vb)
        o = o * lax.rsqrt(jnp.mean(o * o, axis=-1, keepdims=True) + RMS_EPS) * gn_ref[:, sl]
        o_ref[:, sl] = (o * _silu(g_ref[:, sl])).astype(o_ref.dtype)

    @pl.when(c == n_chunks - 1)
    def _():
        sout_ref[...] = state_ref[...]


def retention(p, cos2, sin2, ret_norm_g, state0, into, row0, nb, seq, t, n_heads, q_col, k_col, v_col, g_col):
    w = n_heads * HEAD_DIM
    n_chunks = seq // t
    assert row0 % t == 0 and seq % t == 0
    r0 = row0 // t

    def rows(col):
        return pl.BlockSpec((t, w), lambda b, c: (r0 + b * n_chunks + c, col))

    args = [p, p, p, p, cos2, sin2, ret_norm_g.reshape(1, w).astype(F32), state0]
    in_specs = [rows(q_col), rows(k_col), rows(v_col), rows(g_col),
                pl.BlockSpec((t, HEAD_DIM), lambda b, c: (c, 0)),
                pl.BlockSpec((t, HEAD_DIM), lambda b, c: (c, 0)),
                pl.BlockSpec((1, w), lambda b, c: (0, 0)),
                pl.BlockSpec((None, n_heads, HEAD_DIM, HEAD_DIM), lambda b, c: (b, 0, 0, 0))]
    aliases = {}
    if into is not None:
        aliases = {len(args): 0}
        args.append(into)
        in_specs.append(pl.BlockSpec(memory_space=pl.ANY))
    return pl.pallas_call(
        functools.partial(_retention_kernel, t=t, n_chunks=n_chunks, log_g=_retention_log_decay(n_heads)),
        grid=(nb, n_chunks),
        in_specs=in_specs,
        out_specs=[rows(0),
                   pl.BlockSpec((None, n_heads, HEAD_DIM, HEAD_DIM), lambda b, c: (b, 0, 0, 0))],
        out_shape=[jax.ShapeDtypeStruct((p.shape[0], w), BF16),
                   jax.ShapeDtypeStruct((nb, n_heads, HEAD_DIM, HEAD_DIM), F32)],
        input_output_aliases=aliases,
        scratch_shapes=[pltpu.VMEM((n_heads, HEAD_DIM, HEAD_DIM), F32), pltpu.VMEM((n_heads, t, t), F32)],
        compiler_params=_params("parallel", "arbitrary"),
        name="retention",
    )(*args)


def _rope_tables(pos):
    half = HEAD_DIM // 2
    inv = ROPE_BASE ** (-jnp.arange(half, dtype=F32) / half)
    ang = pos.astype(F32)[:, None] * inv[None, :]
    cos, sin = jnp.cos(ang), jnp.sin(ang)
    return jnp.concatenate([cos, cos], axis=1), jnp.concatenate([-sin, sin], axis=1)


def _band_bias_table(rel_bias, tq):
    n_heads = rel_bias.shape[0]
    w = BAND_WINDOW + tq
    n = tq + w - 1
    dist = BAND_WINDOW + tq - 1 - np.arange(n)
    vec = rel_bias.astype(F32)[:, np.clip(dist, -MAX_REL, MAX_REL) + MAX_REL]
    vec = jnp.roll(vec, -(tq - 1), axis=1)
    bias = jnp.tile(vec, (1, tq))[:, :tq * (n - 1)].reshape(n_heads, tq, n - 1)[:, :, :w]
    t = np.arange(tq)[:, None]
    j = np.arange(w)[None, :]
    lo = (t // CHUNK) * CHUNK
    in_band = (j >= lo) & (j < lo + BAND_WINDOW + CHUNK)
    return jnp.where(jnp.asarray(in_band)[None], bias, NEG_INF)


def _softmax_pv(scores, values):
    m = scores[0].max(axis=1, keepdims=True)
    for s in scores[1:]:
        m = jnp.maximum(m, s.max(axis=1, keepdims=True))
    ps = [jnp.exp(s - m) for s in scores]
    denom = ps[0].sum(axis=1, keepdims=True)
    for p in ps[1:]:
        denom = denom + p.sum(axis=1, keepdims=True)
    o = _dot(ps[0].astype(BF16), values[0])
    for p, v in zip(ps[1:], values[1:]):
        o = o + _dot(p.astype(BF16), v)
    return o * (1.0 / denom)


BAND_HEADS_PER_STEP = 4


def _band_prompt_kernel(*refs, tq, n_kb, hb):
    q_ref = refs[0]
    k_refs, v_refs = refs[1:1 + n_kb], refs[1 + n_kb:1 + 2 * n_kb]
    b_ref, o_ref = refs[1 + 2 * n_kb], refs[2 + 2 * n_kb]
    i = pl.program_id(1)
    for h in range(hb):
        sl = slice(h * HEAD_DIM, (h + 1) * HEAD_DIM)
        q = (q_ref[:, sl] * SCALE).astype(BF16)
        scores, values = [], []
        for kb in range(n_kb):
            s = _dot_nt(q, k_refs[kb][:, sl].astype(BF16)) + b_ref[h, :, kb * tq:(kb + 1) * tq]
            scores.append(s + jnp.where(i - (n_kb - 1) + kb >= 0, 0.0, NEG_INF))
            values.append(v_refs[kb][:, sl].astype(BF16))
        o_ref[:, sl] = _softmax_pv(scores, values).astype(o_ref.dtype)


def band_prompt(p, rel_bias, s, n_heads, q_col, k_col, v_col):
    tq = _tile(s, 256)
    hb = BAND_HEADS_PER_STEP
    w = hb * HEAD_DIM
    assert BAND_WINDOW % tq == 0 and tq % CHUNK == 0 and n_heads % hb == 0
    assert q_col % hb == 0 and k_col % hb == 0 and v_col % hb == 0
    n_kb = BAND_WINDOW // tq + 1

    def kv_specs(col):
        return [pl.BlockSpec((tq, w), lambda h, i, kb=kb: (jnp.maximum(i - (n_kb - 1) + kb, 0), col // hb + h))
                for kb in range(n_kb)]

    return pl.pallas_call(
        functools.partial(_band_prompt_kernel, tq=tq, n_kb=n_kb, hb=hb),
        grid=(n_heads // hb, s // tq),
        in_specs=([pl.BlockSpec((tq, w), lambda h, i: (i, q_col // hb + h))] + kv_specs(k_col) + kv_specs(v_col)
                  + [pl.BlockSpec((hb, tq, BAND_WINDOW + tq), lambda h, i: (h, 0, 0))]),
        out_specs=pl.BlockSpec((tq, w), lambda h, i: (i, h)),
        out_shape=jax.ShapeDtypeStruct((p.shape[0], n_heads * HEAD_DIM), BF16),
        compiler_params=_params("parallel", "parallel"),
        name="band_prompt",
    )(*([p] * (1 + 2 * n_kb)), _band_bias_table(rel_bias, tq))


def _band_sample_kernel(q_ref, kn_ref, vn_ref, kc_ref, vc_ref, b_ref, _into_ref, o_ref, ko_ref, vo_ref, *,
                        band_past, t, n_heads):
    keep = (band_past - t) * n_heads
    for h in range(n_heads):
        sl = slice(h * HEAD_DIM, (h + 1) * HEAD_DIM)
        q = (q_ref[:, sl] * SCALE).astype(BF16)
        kn, vn = kn_ref[:, sl], vn_ref[:, sl]
        kc = kc_ref[pl.ds(h, band_past, stride=n_heads), :].astype(BF16)
        vc = vc_ref[pl.ds(h, band_past, stride=n_heads), :].astype(BF16)
        s_c = _dot_nt(q, kc) + b_ref[h, :, :band_past]
        s_n = _dot_nt(q, kn.astype(BF16)) + b_ref[h, :, band_past:]
        o_ref[:, sl] = _softmax_pv([s_c, s_n], [vc, vn.astype(BF16)]).astype(o_ref.dtype)
        ko_ref[pl.ds(keep + h, t, stride=n_heads), :] = kn
        vo_ref[pl.ds(keep + h, t, stride=n_heads), :] = vn
    ko_ref[:keep] = kc_ref[t * n_heads:]
    vo_ref[:keep] = vc_ref[t * n_heads:]


def band_sample(p, cache_k, cache_v, rel_bias, into, row0, nb, t, q_col, k_col, v_col):
    _, band_past, n_heads, _ = cache_k.shape
    w = n_heads * HEAD_DIM
    assert t == CHUNK and band_past == BAND_WINDOW and row0 % t == 0
    r0 = row0 // t
    cache_spec = pl.BlockSpec((None, band_past * n_heads, HEAD_DIM), lambda b: (b, 0, 0))
    cache_shape = jax.ShapeDtypeStruct((nb, band_past * n_heads, HEAD_DIM), F32)
    cache_k = cache_k.reshape(cache_shape.shape)
    cache_v = cache_v.reshape(cache_shape.shape)
    o, k_out, v_out = pl.pallas_call(
        functools.partial(_band_sample_kernel, band_past=band_past, t=t, n_heads=n_heads),
        grid=(nb,),
        in_specs=[pl.BlockSpec((t, w), lambda b: (r0 + b, q_col)),
                  pl.BlockSpec((t, w), lambda b: (r0 + b, k_col)),
                  pl.BlockSpec((t, w), lambda b: (r0 + b, v_col)),
                  cache_spec, cache_spec,
                  pl.BlockSpec((n_heads, t, band_past + t), lambda b: (0, 0, 0)),
                  pl.BlockSpec(memory_space=pl.ANY)],
        out_specs=[pl.BlockSpec((t, w), lambda b: (r0 + b, 0)), cache_spec, cache_spec],
        out_shape=[jax.ShapeDtypeStruct(into.shape, into.dtype), cache_shape, cache_shape],
        input_output_aliases={6: 0},
        compiler_params=_params("parallel"),
        name="band_sample",
    )(p, p, p, cache_k.astype(F32), cache_v.astype(F32), _band_bias_table(rel_bias, t), into)
    out_4d = (nb, band_past, n_heads, HEAD_DIM)
    return o, k_out.reshape(out_4d), v_out.reshape(out_4d)


def _cross_attn_kernel(q_ref, mk_ref, mv_ref, *rest, n_heads, n_mem):
    o_ref = rest[-1]
    for h in range(n_heads):
        sl = slice(h * HEAD_DIM, (h + 1) * HEAD_DIM)
        q = (q_ref[:, sl] * SCALE).astype(BF16)
        mk = mk_ref[pl.ds(h, n_mem, stride=n_heads), :].astype(BF16)
        mv = mv_ref[pl.ds(h, n_mem, stride=n_heads), :].astype(BF16)
        o_ref[:, sl] = _softmax_pv([_dot_nt(q, mk)], [mv]).astype(o_ref.dtype)


def cross_attn(q, mk, mv, into, row0, nb, seq):
    _, n_mem, n_heads, _ = mk.shape
    w = n_heads * HEAD_DIM
    tq = _tile(seq, 512)
    n_t = seq // tq
    assert row0 % tq == 0
    r0 = row0 // tq
    mem_spec = pl.BlockSpec((None, n_mem * n_heads, HEAD_DIM), lambda b, i: (b, 0, 0))
    args = [q, mk.reshape(nb, n_mem * n_heads, HEAD_DIM), mv.reshape(nb, n_mem * n_heads, HEAD_DIM)]
    in_specs = [pl.BlockSpec((tq, w), lambda b, i: (r0 + b * n_t + i, 0)), mem_spec, mem_spec]
    aliases = {}
    if into is not None:
        aliases = {len(args): 0}
        args.append(into)
        in_specs.append(pl.BlockSpec(memory_space=pl.ANY))
    return pl.pallas_call(
        functools.partial(_cross_attn_kernel, n_heads=n_heads, n_mem=n_mem),
        grid=(nb, n_t),
        in_specs=in_specs,
        out_specs=pl.BlockSpec((tq, w), lambda b, i: (r0 + b * n_t + i, 0)),
        out_shape=jax.ShapeDtypeStruct((q.shape[0], w), BF16),
        input_output_aliases=aliases,
        compiler_params=_params("parallel", "parallel"),
        name="cross_attn",
    )(*args)


def kernel(x_prompt, x_sample, cache_sb_k, cache_sb_v, state_ret, cache_band_k, cache_band_v, cache_mem_k, cache_mem_v, mem_prompt, w_in_ab, w_out_ab, ret_norm_g, w_qkv_band, w_out_band, rel_bias_band, norm_g, mem_norm_g, w_xq, w_xk, w_xv, w_xo, ffn_w_gate, ffn_w_up, ffn_w_down, moe_router, moe_w_gate, moe_w_up, moe_w_down, final_norm_g):
    bp, s, d = x_prompt.shape
    nb, t, _ = x_sample.shape
    assert bp == 1
    past = cache_sb_k.shape[1]
    band_past = cache_band_k.shape[1]
    h_sb = cache_sb_k.shape[2]
    h_ret = state_ret.shape[1]
    h_band = cache_band_k.shape[2]
    d_sb, d_ret = h_sb * HEAD_DIM, h_ret * HEAD_DIM
    n_mem = mem_prompt.shape[1]
    d_x = w_xq.shape[2]
    depth = w_xq.shape[0]
    d_ff = ffn_w_gate.shape[1]
    ms = nb * t
    assert d_sb == d_ret and 3 * d_sb + 4 * d_ret == w_in_ab.shape[1]

    x = jnp.concatenate([x_prompt.reshape(s, d), x_sample.reshape(ms, d)], axis=0)
    mem = mem_prompt.reshape(n_mem, d)
    bf = lambda a: a.astype(BF16)

    cos_p, sin_p = _rope_tables(jnp.arange(s, dtype=jnp.int32))
    cos_s, sin_s = _rope_tables(past + jnp.arange(t, dtype=jnp.int32))

    mem_k_list, mem_v_list = [], []
    for l in range(depth):
        mem_n = rmsnorm(mem, mem_norm_g[l], BF16)
        mk_p = matmul([(mem_n, 0, 0)], bf(w_xk[l]), d)
        mv_p = matmul([(mem_n, 0, 0)], bf(w_xv[l]), d)
        mem_k_list.append(mk_p.reshape(1, n_mem, H_X, HEAD_DIM))
        mem_v_list.append(mv_p.reshape(1, n_mem, H_X, HEAD_DIM))

        h = rmsnorm(x, norm_g[l, 0], BF16)
        if l % 2 == 0:
            p = matmul([(h, 0, 0)], bf(w_in_ab), d)
            o_sb = sb_prompt(p, s, h_sb, 0, h_sb, 2 * h_sb)
            o_sb = sb_sample(p, cache_sb_k, cache_sb_v, o_sb, s, nb, t, 0, 1, 2)
            t_ret = _tile(s, 256, CHUNK)
            o_r, ret_state_prompt = retention(p, cos_p, sin_p, ret_norm_g, jnp.zeros((1,) + state_ret.shape[1:], F32),
                                              None, 0, 1, s, t_ret, h_ret, 3, 4, 5, 6)
            o_r, ret_state_sample = retention(p, cos_s, sin_s, ret_norm_g, state_ret.astype(F32),
                                              o_r, s, nb, t, t, h_ret, 3, 4, 5, 6)
            x = matmul([(o_sb, 0, 0), (o_r, 0, d_sb)], bf(w_out_ab), d_sb, residual=x)
            sb_k_prompt = p[:s, d_sb:2 * d_sb].reshape(1, s, h_sb, HEAD_DIM)
            sb_v_prompt = p[:s, 2 * d_sb:3 * d_sb].reshape(1, s, h_sb, HEAD_DIM)
            sb_k_sample = p[s:, d_sb:2 * d_sb].reshape(nb, t, h_sb, HEAD_DIM)
            sb_v_sample = p[s:, 2 * d_sb:3 * d_sb].reshape(nb, t, h_sb, HEAD_DIM)
        else:
            d_band = h_band * HEAD_DIM
            p = matmul([(h, 0, 0)], bf(w_qkv_band), d)
            o = band_prompt(p, rel_bias_band, s, h_band, 0, h_band, 2 * h_band)
            o, band_k_sample, band_v_sample = band_sample(p, cache_band_k, cache_band_v, rel_bias_band, o,
                                                          s, nb, t, 0, 1, 2)
            x = matmul([(o, 0, 0)], bf(w_out_band), d_band, residual=x)
            band_k_prompt = p[s - band_past:s, d_band:2 * d_band].reshape(1, band_past, h_band, HEAD_DIM)
            band_v_prompt = p[s - band_past:s, 2 * d_band:].reshape(1, band_past, h_band, HEAD_DIM)

        h = rmsnorm(x, norm_g[l, 1], BF16)
        q = matmul([(h, 0, 0)], bf(w_xq[l]), d)
        o = cross_attn(q, mem_k_list[-1], mem_v_list[-1], None, 0, 1, s)
        o = cross_attn(q, cache_mem_k[l], cache_mem_v[l], o, s, nb, t)
        x = matmul([(o, 0, 0)], bf(w_xo[l]), d_x, residual=x)

        h = rmsnorm(x, norm_g[l, 2], BF16)
        if l % 2 == 0:
            act = swiglu_up(h, bf(ffn_w_gate), bf(ffn_w_up), tn_pref=512)
            x = matmul([(act, 0, 0)], bf(ffn_w_down), d_ff, residual=x, tk_pref=d_ff)
        else:
            gates, idx = moe_route(x, norm_g[l, 2], moe_router)
            x = moe_top2(x, h, idx, gates, bf(moe_w_gate), bf(moe_w_up), bf(moe_w_down))

    y_prompt = rmsnorm(x, final_norm_g, F32, 0, s).reshape(1, s, d)
    y_sample = rmsnorm(x, final_norm_g, F32, s, ms).reshape(nb, t, d)
    mem_k_prompt = jnp.stack(mem_k_list, axis=0)
    mem_v_prompt = jnp.stack(mem_v_list, axis=0)
    return (y_prompt, y_sample, sb_k_prompt, sb_v_prompt, sb_k_sample, sb_v_sample,
            ret_state_prompt, ret_state_sample, band_k_prompt, band_v_prompt,
            band_k_sample, band_v_sample, mem_k_prompt, mem_v_prompt)
```

```python
import functools

import numpy as np
import jax
import jax.numpy as jnp
from jax import lax
from jax.experimental import pallas as pl
from jax.experimental.pallas import tpu as pltpu

F32 = jnp.float32
BF16 = jnp.bfloat16

HEAD_DIM = 128
CHUNK = 64
N_BAND_CHUNKS = 8
BAND_WINDOW = N_BAND_CHUNKS * CHUNK
MAX_REL = 128
H_X = 4
TOP_K = 2
RMS_EPS = 1e-6
ROPE_BASE = 10000.0
NEG_INF = -1e30
SCALE = HEAD_DIM ** -0.5
LOG2E = 1.4426950408889634

VMEM_LIMIT_BYTES = 56 * 1024 * 1024


def _params(*sem):
    return pltpu.CompilerParams(dimension_semantics=sem, vmem_limit_bytes=VMEM_LIMIT_BYTES)


def _tile(n, pref, mult=8):
    t = min(n, pref)
    while t > mult and (n % t or t % mult):
        t -= mult
    assert n % t == 0, (n, pref)
    return t


def _dot(a, b):
    return jnp.dot(a, b, preferred_element_type=F32)


def _dot_nt(a, b):
    return lax.dot_general(a, b, (((1,), (1,)), ((), ())), preferred_element_type=F32)


def _dot_tn(a, b):
    return lax.dot_general(a, b, (((0,), (0,)), ((), ())), preferred_element_type=F32)


def _silu(a):
    return a * (1.0 / (1.0 + jnp.exp(-a)))


def _rows(parts):
    return parts[0] if len(parts) == 1 else jnp.concatenate(parts, axis=0)


def _rmsnorm_kernel(x_ref, g_ref, o_ref):
    x = x_ref[...]
    y = x * lax.rsqrt(jnp.mean(x * x, axis=-1, keepdims=True) + RMS_EPS) * g_ref[...]
    o_ref[...] = y.astype(o_ref.dtype)


def rmsnorm(x, g, out_dtype, row0=0, rows=None):
    d = x.shape[1]
    m = x.shape[0] - row0 if rows is None else rows
    tm = _tile(m, 512)
    assert row0 % tm == 0
    return pl.pallas_call(
        _rmsnorm_kernel,
        grid=(m // tm,),
        in_specs=[pl.BlockSpec((tm, d), lambda i: (row0 // tm + i, 0)),
                  pl.BlockSpec((1, d), lambda i: (0, 0))],
        out_specs=pl.BlockSpec((tm, d), lambda i: (i, 0)),
        out_shape=jax.ShapeDtypeStruct((m, d), out_dtype),
        compiler_params=_params("parallel"),
        name="rmsnorm",
    )(x, g.reshape(1, d).astype(F32))


def _mm_kernel(*refs, n_parts, has_res, nk):
    xs, ws = refs[:n_parts], refs[n_parts:2 * n_parts]
    pos = 2 * n_parts
    res_ref = refs[pos] if has_res else None
    pos += int(has_res)
    o_ref = refs[pos]
    part = _dot(xs[0][...], ws[0][...])
    for x_ref, w_ref in zip(xs[1:], ws[1:]):
        part = part + _dot(x_ref[...], w_ref[...])
    if nk == 1:
        if has_res:
            part = res_ref[...] + part
        o_ref[...] = part.astype(o_ref.dtype)
        return
    acc_ref = refs[pos + 1]
    k = pl.program_id(2)

    @pl.when(k == 0)
    def _():
        acc_ref[...] = part

    @pl.when(k > 0)
    def _():
        acc_ref[...] += part

    @pl.when(k == nk - 1)
    def _():
        out = acc_ref[...]
        if has_res:
            out = res_ref[...] + out
        o_ref[...] = out.astype(o_ref.dtype)


def matmul(parts, w, k_part, *, col_off=0, n_cols=None, out_dtype=F32, residual=None,
           tm_pref=1024, tn_pref=512, tk_pref=2048):
    m = parts[0][0].shape[0]
    n_cols = w.shape[1] - col_off if n_cols is None else n_cols
    tm = _tile(m, tm_pref)
    tn = _tile(n_cols, tn_pref, 128)
    tk = _tile(k_part, tk_pref, 128)
    nk = k_part // tk
    assert col_off % tn == 0
    in_specs, args = [], []
    for x, xo, _ in parts:
        assert xo % tk == 0
        in_specs.append(pl.BlockSpec((tm, tk), lambda i, j, k, xo=xo: (i, xo // tk + k)))
        args.append(x)
    for _, _, wo in parts:
        assert wo % tk == 0
        in_specs.append(pl.BlockSpec((tk, tn), lambda i, j, k, wo=wo: (wo // tk + k, col_off // tn + j)))
        args.append(w)
    if residual is not None:
        in_specs.append(pl.BlockSpec((tm, tn), lambda i, j, k: (i, j)))
        args.append(residual)
    return pl.pallas_call(
        functools.partial(_mm_kernel, n_parts=len(parts), has_res=residual is not None, nk=nk),
        grid=(m // tm, n_cols // tn, nk),
        in_specs=in_specs,
        out_specs=pl.BlockSpec((tm, tn), lambda i, j, k: (i, j)),
        out_shape=jax.ShapeDtypeStruct((m, n_cols), out_dtype),
        scratch_shapes=[pltpu.VMEM((tm, tn), F32)] if nk > 1 else [],
        compiler_params=_params("parallel", "parallel", "arbitrary"),
        name="matmul",
    )(*args)


def _norm_linear_kernel(x_ref, g_ref, *rest, n_w):
    w_refs, o_ref, wb_refs = rest[:n_w], rest[n_w], rest[n_w + 1:]

    @pl.when(pl.program_id(1) == 0)
    def _():
        for w_ref, wb_ref in zip(w_refs, wb_refs):
            wb_ref[...] = w_ref[...].astype(BF16)

    x = x_ref[...]
    h = (x * lax.rsqrt(jnp.mean(x * x, axis=-1, keepdims=True) + RMS_EPS) * g_ref[...]).astype(BF16)
    if n_w == 1:
        out = _dot(h, wb_refs[0][...])
    else:
        out = _silu(_dot(h, wb_refs[0][...])) * _dot(h, wb_refs[1][...])
    o_ref[...] = out.astype(o_ref.dtype)


def norm_linear(x, g, ws, out_dtype, *, tn_pref):
    m, d = x.shape
    n = ws[0].shape[1]
    tm = _tile(m, 1024)
    tn = _tile(n, tn_pref, 128)
    return pl.pallas_call(
        functools.partial(_norm_linear_kernel, n_w=len(ws)),
        grid=(n // tn, m // tm),
        in_specs=([pl.BlockSpec((tm, d), lambda j, i: (i, 0)), pl.BlockSpec((1, d), lambda j, i: (0, 0))]
                  + [pl.BlockSpec((d, tn), lambda j, i: (0, j))] * len(ws)),
        out_specs=pl.BlockSpec((tm, tn), lambda j, i: (i, j)),
        out_shape=jax.ShapeDtypeStruct((m, n), out_dtype),
        scratch_shapes=[pltpu.VMEM((d, tn), BF16)] * len(ws),
        compiler_params=_params("parallel", "arbitrary"),
        name="norm_linear",
    )(x, g.reshape(1, d).astype(F32), *[w.astype(F32) for w in ws])


def _router_kernel(x_ref, g_ref, r_ref, o_ref, *, n_experts):
    x = x_ref[...]
    hn = x * lax.rsqrt(jnp.mean(x * x, axis=-1, keepdims=True) + RMS_EPS) * g_ref[...]
    logits = jnp.dot(hn, r_ref[...], preferred_element_type=F32, precision=lax.Precision.HIGHEST)
    lane = lax.broadcasted_iota(jnp.int32, logits.shape, 1)
    n_lanes = logits.shape[1]
    lg = jnp.where(lane < n_experts, logits, -jnp.inf)
    m1 = jnp.max(lg, axis=1, keepdims=True)
    i1 = jnp.min(jnp.where(lg == m1, lane, n_lanes), axis=1, keepdims=True)
    lg2 = jnp.where(lane == i1, -jnp.inf, lg)
    m2 = jnp.max(lg2, axis=1, keepdims=True)
    i2 = jnp.min(jnp.where(lg2 == m2, lane, n_lanes), axis=1, keepdims=True)
    e2 = jnp.exp(m2 - m1)
    inv = 1.0 / (1.0 + e2)
    o_ref[...] = (jnp.where(lane == 0, inv, 0.0) + jnp.where(lane == 1, e2 * inv, 0.0)
                  + jnp.where(lane == 2, i1.astype(F32), 0.0) + jnp.where(lane == 3, i2.astype(F32), 0.0))


def moe_route(x, g, router):
    m, d = x.shape
    n_e = router.shape[1]
    tm = _tile(m, 512)
    r_pad = jnp.zeros((d, 128), F32).at[:, :n_e].set(router.astype(F32))
    out = pl.pallas_call(
        functools.partial(_router_kernel, n_experts=n_e),
        grid=(m // tm,),
        in_specs=[pl.BlockSpec((tm, d), lambda i: (i, 0)),
                  pl.BlockSpec((1, d), lambda i: (0, 0)),
                  pl.BlockSpec((d, 128), lambda i: (0, 0))],
        out_specs=pl.BlockSpec((tm, 128), lambda i: (i, 0)),
        out_shape=jax.ShapeDtypeStruct((m, 128), F32),
        compiler_params=_params("parallel"),
        name="moe_router",
    )(x, g.reshape(1, d).astype(F32), r_pad)
    return out[:, :TOP_K], out[:, TOP_K:2 * TOP_K].astype(jnp.int32)


MOE_ROW_TILE = 256
MOE_GATHER_CHUNK = 512
MOE_COMBINE_TILE = 512


def _i32(a):
    return a.astype(jnp.int32)


def _count_le(ends, v):
    return jnp.sum(_i32(ends[None, :] <= v[:, None]), axis=1, dtype=jnp.int32)


def _moe_plan(idx, gates, n_e, tr, tc, tt):
    m = idx.shape[0]
    assert (TOP_K * m) % tr == 0 and m % tc == 0 and m % tt == 0
    n_tiles = TOP_K * m // tr + n_e
    n_rows = n_tiles * tr
    routed = jnp.zeros((m, n_e), jnp.int32)
    for k in range(TOP_K):
        routed = routed + _i32(idx[:, k:k + 1] == jnp.arange(n_e, dtype=jnp.int32)[None, :])
    csum = jnp.cumsum(routed, axis=0, dtype=jnp.int32)
    rank = csum - routed
    tiles_e = (csum[-1] + tr - 1) // tr
    tile_end = jnp.cumsum(tiles_e, dtype=jnp.int32)
    row_start = (tile_end - tiles_e) * tr
    dest = row_start[idx] + jnp.take_along_axis(rank, idx, axis=1)
    gate_rows = jnp.zeros((n_rows,), F32).at[dest.reshape(-1)].set(gates.reshape(-1), unique_indices=True)
    tile_ids = jnp.arange(n_tiles, dtype=jnp.int32)
    tile_valid = tile_ids < tile_end[-1]
    tile_expert = jnp.minimum(_count_le(tile_end, tile_ids), n_e - 1)

    def visits(first_chunk, n_visits, n_max, owner_of):
        end = jnp.cumsum(n_visits, dtype=jnp.int32)
        start = end - n_visits
        v = jnp.arange(n_max, dtype=jnp.int32)
        live = v < end[-1]
        slot = jnp.minimum(_count_le(end, v), n_visits.shape[0] - 1)
        owner = owner_of(slot)
        chunk = first_chunk[slot] + v - start[slot]
        last = jnp.maximum(end[-1] - 1, 0)
        owner = jnp.where(live, owner, owner[last])
        chunk = jnp.where(live, chunk, chunk[last])
        first = live & ((v == 0) | (owner != jnp.roll(owner, 1)))
        return _i32(owner), _i32(chunk), _i32(first) + 2 * _i32(live)

    rank0 = (tile_ids - (tile_end - tiles_e)[tile_expert]) * tr
    rank1 = jnp.minimum(rank0 + tr, csum[-1][tile_expert]) - 1
    csum_e = csum.T[tile_expert]
    first_tok = jnp.sum(_i32(csum_e <= rank0[:, None]), axis=1, dtype=jnp.int32)
    last_tok = jnp.sum(_i32(csum_e <= rank1[:, None]), axis=1, dtype=jnp.int32)
    c_first = jnp.where(tile_valid, first_tok, 0) // tc
    c_last = jnp.where(tile_valid, last_tok, 0) // tc
    g_plan = visits(c_first, c_last - c_first + 1, n_tiles + n_e * (m // tc - 1), lambda s: s)
    n_tt = m // tt
    before = jnp.concatenate([jnp.zeros((1, n_e), jnp.int32), csum[tt - 1::tt]], axis=0)
    lo = row_start[None, :] + before[:-1]
    hi = row_start[None, :] + before[1:]
    n_vis = jnp.where(hi > lo, (hi - 1) // tr - lo // tr + 1, 0)
    c_plan = visits((lo // tr).reshape(-1), n_vis.reshape(-1), n_tt * n_e + n_tiles - 1, lambda s: s // n_e)
    return dict(n_tiles=n_tiles, dest=dest, gate_rows=gate_rows.reshape(n_rows, 1),
                tile_expert=tile_expert, tile_valid=_i32(tile_valid), gather=g_plan, combine=c_plan)


def _moe_select(dest_ref, row0, n_rows):
    row = row0 + lax.broadcasted_iota(jnp.int32, (dest_ref.shape[0], n_rows), 1)
    hit = row == dest_ref[:, 0:1]
    for k in range(1, TOP_K):
        hit = hit | (row == dest_ref[:, k:k + 1])
    return jnp.where(hit, 1.0, 0.0).astype(BF16)


def _moe_gather_kernel(vt_ref, vc_ref, vf_ref, dest_ref, h_ref, o_ref):
    v = pl.program_id(0)
    flags = vf_ref[v]
    tr = o_ref.shape[0]

    @pl.when(flags % 2 == 1)
    def _():
        o_ref[...] = jnp.zeros_like(o_ref)

    @pl.when(flags >= 2)
    def _():
        sel = _moe_select(dest_ref, vt_ref[v] * tr, tr)
        o_ref[...] += _dot_tn(sel, h_ref[...]).astype(o_ref.dtype)


def _moe_up_kernel(te_ref, tv_ref, x_ref, wg_ref, wu_ref, o_ref):
    j = pl.program_id(1)

    @pl.when(tv_ref[j] != 0)
    def _():
        x = x_ref[...]
        o_ref[...] = (_silu(_dot(x, wg_ref[...])) * _dot(x, wu_ref[...])).astype(o_ref.dtype)

    @pl.when(tv_ref[j] == 0)
    def _():
        o_ref[...] = jnp.zeros_like(o_ref)


def _moe_down_kernel(te_ref, a_ref, wd_ref, g_ref, o_ref):
    o_ref[...] = (_dot(a_ref[...], wd_ref[...]) * g_ref[...]).astype(o_ref.dtype)


def _moe_combine_kernel(vi_ref, vc_ref, vf_ref, dest_ref, y_ref, x_ref, o_ref):
    v = pl.program_id(0)
    flags = vf_ref[v]
    tr = y_ref.shape[0]

    @pl.when(flags % 2 == 1)
    def _():
        o_ref[...] = x_ref[...]

    @pl.when(flags >= 2)
    def _():
        sel = _moe_select(dest_ref, vc_ref[v] * tr, tr)
        o_ref[...] += _dot(sel, y_ref[...])


def moe_top2(x, h, idx, gates, wg, wu, wd):
    m, d = h.shape
    n_e, _, f = wg.shape
    tr, tc, tt = MOE_ROW_TILE, _tile(m, MOE_GATHER_CHUNK, 16), _tile(m, MOE_COMBINE_TILE)
    plan =_moe_plan(idx, gates, n_e, tr, tc, tt)
    n_tiles = plan["n_tiles"]
    n_rows = n_tiles * tr

    vt, vc, vf = plan["gather"]
    xs = pl.pallas_call(
        _moe_gather_kernel,
        grid_spec=pltpu.PrefetchScalarGridSpec(
            num_scalar_prefetch=3, grid=(vt.shape[0],),
            in_specs=[pl.BlockSpec((tc, TOP_K), lambda v, vt, vc, vf: (vc[v], 0)),
                      pl.BlockSpec((tc, d), lambda v, vt, vc, vf: (vc[v], 0))],
            out_specs=pl.BlockSpec((tr, d), lambda v, vt, vc, vf: (vt[v], 0))),
        out_shape=jax.ShapeDtypeStruct((n_rows, d), BF16),
        compiler_params=_params("arbitrary"),
        name="moe_gather",
    )(vt, vc, vf, plan["dest"], h)

    tf = _tile(f, 1408, 128)
    act = pl.pallas_call(
        _moe_up_kernel,
        grid_spec=pltpu.PrefetchScalarGridSpec(
            num_scalar_prefetch=2, grid=(f // tf, n_tiles),
            in_specs=[pl.BlockSpec((tr, d), lambda c, j, te, tv: (j, 0)),
                      pl.BlockSpec((None, d, tf), lambda c, j, te, tv: (te[j], 0, c)),
                      pl.BlockSpec((None, d, tf), lambda c, j, te, tv: (te[j], 0, c))],
            out_specs=pl.BlockSpec((tr, tf), lambda c, j, te, tv: (j, c))),
        out_shape=jax.ShapeDtypeStruct((n_rows, f), BF16),
        compiler_params=_params("parallel", "parallel"),
        name="moe_up",
    )(plan["tile_expert"], plan["tile_valid"], xs, wg, wu)

    y = pl.pallas_call(
        _moe_down_kernel,
        grid_spec=pltpu.PrefetchScalarGridSpec(
            num_scalar_prefetch=1, grid=(n_tiles,),
            in_specs=[pl.BlockSpec((tr, f), lambda j, te: (j, 0)),
                      pl.BlockSpec((None, f, d), lambda j, te: (te[j], 0, 0)),
                      pl.BlockSpec((tr, 1), lambda j, te: (j, 0))],
            out_specs=pl.BlockSpec((tr, d), lambda j, te: (j, 0))),
        out_shape=jax.ShapeDtypeStruct((n_rows, d), BF16),
        compiler_params=_params("parallel"),
        name="moe_down",
    )(plan["tile_expert"], act, wd, plan["gate_rows"])

    vi, vc, vf = plan["combine"]
    return pl.pallas_call(
        _moe_combine_kernel,
        grid_spec=pltpu.PrefetchScalarGridSpec(
            num_scalar_prefetch=3, grid=(vi.shape[0],),
            in_specs=[pl.BlockSpec((tt, TOP_K), lambda v, vi, vc, vf: (vi[v], 0)),
                      pl.BlockSpec((tr, d), lambda v, vi, vc, vf: (vc[v], 0)),
                      pl.BlockSpec((tt, d), lambda v, vi, vc, vf: (vi[v], 0))],
            out_specs=pl.BlockSpec((tt, d), lambda v, vi, vc, vf: (vi[v], 0))),
        out_shape=jax.ShapeDtypeStruct((m, d), F32),
        compiler_params=_params("arbitrary"),
        name="moe_combine",
    )(vi, vc, vf, plan["dest"], y, x)


def _cumsum_matrix(tk):
    r = np.arange(2 * tk)[:, None]
    c = np.arange(2 * tk)[None, :]
    return jnp.asarray((r > c).astype(np.float32), dtype=BF16)


def _sb_sweep_step(qs, k2s, v2s, before, u, acc, carry):
    tq = qs[0].shape[0]
    tk = k2s[0].shape[0] // 2
    z = _rows([_dot_nt(q, k2) for q, k2 in zip(qs, k2s)])
    neg_l = jnp.maximum(z, 0.0) + jnp.log2(1.0 + jnp.exp2(-jnp.abs(z)))
    log_b = z - neg_l
    if before is not None:
        neg_l = jnp.where(before, neg_l, 0.0)
    neg_lb = neg_l.astype(BF16)
    after = _dot(neg_lb, u)
    w = jnp.exp2(log_b - (after + jnp.concatenate([carry, carry], axis=1)))
    if before is not None:
        w = jnp.where(before, w, 0.0)
    w = w.astype(BF16)
    acc = acc + _rows([_dot(w[n * tq:(n + 1) * tq], v2) for n, v2 in enumerate(v2s)])
    total = after[:, :1] + neg_lb[:, :1].astype(F32)
    return acc, carry + total


def _sb_prompt_kernel(q_ref, k_ref, v_ref, u_ref, o_ref, kb_ref, vb_ref, *, tq, tk):
    i = pl.program_id(1)

    @pl.when(i == 0)
    def _():
        kb_ref[...] = k_ref[...].astype(BF16)
        vb_ref[...] = v_ref[...].astype(BF16)

    q = (q_ref[...] * (SCALE * LOG2E)).astype(BF16)
    u = u_ref[...]
    steps_per_tile = tq // (2 * tk)
    state = (jnp.zeros((tq, HEAD_DIM), F32), jnp.zeros((tq, tk), F32))

    def step(off, r0, before, state):
        new = _sb_sweep_step([q[r0:]], [kb_ref[pl.ds(off, 2 * tk), :]], [vb_ref[pl.ds(off, 2 * tk), :]],
                             before, u, *[a[r0:] for a in state])
        return tuple(jnp.concatenate([a[:r0], b], axis=0) if r0 else b for a, b in zip(state, new))

    for r0 in reversed(range(0, tq, 2 * tk)):
        before = (lax.broadcasted_iota(jnp.int32, (tq - r0, 2 * tk), 1)
                  < lax.broadcasted_iota(jnp.int32, (tq - r0, 2 * tk), 0))
        state = step(pl.multiple_of(i * tq + r0, 2 * tk), r0, before, state)
    n_steps = i * steps_per_tile
    unroll = 2 if steps_per_tile % 2 == 0 else 1

    def body(s, st):
        for k in range(unroll):
            st = step(pl.multiple_of((n_steps - 1 - unroll * s - k) * 2 * tk, 2 * tk), 0, None, st)
        return st

    acc, _ = lax.fori_loop(0, n_steps // unroll, body, state)
    o_ref[...] = acc.astype(o_ref.dtype)


def sb_prompt(p, s, n_heads, q_col, k_col, v_col):
    tk = HEAD_DIM
    tq = _tile(s, 1024, 2 * tk)
    return pl.pallas_call(
        functools.partial(_sb_prompt_kernel, tq=tq, tk=tk),
        grid=(n_heads, s // tq),
        in_specs=[pl.BlockSpec((tq, HEAD_DIM), lambda h, i: (i, q_col + h)),
                  pl.BlockSpec((s, HEAD_DIM), lambda h, i: (0, k_col + h)),
                  pl.BlockSpec((s, HEAD_DIM), lambda h, i: (0, v_col + h)),
                  pl.BlockSpec((2 * tk, 2 * tk), lambda h, i: (0, 0))],
        out_specs=pl.BlockSpec((tq, HEAD_DIM), lambda h, i: (i, h)),
        out_shape=jax.ShapeDtypeStruct((p.shape[0], n_heads * HEAD_DIM), BF16),
        scratch_shapes=[pltpu.VMEM((s, HEAD_DIM), BF16), pltpu.VMEM((s, HEAD_DIM), BF16)],
        compiler_params=_params("parallel", "arbitrary"),
        name="sb_prompt",
    )(p, p, p, _cumsum_matrix(tk))


def _sb_sample_kernel(q_ref, kn_ref, vn_ref, kc_ref, vc_ref, u_ref, _into_ref, o_ref,
                      acc_ref, carry_ref, kb_ref, vb_ref, *, t, tk, ck, n_heads):
    c = pl.program_id(1)
    heads = [slice(h * HEAD_DIM, (h + 1) * HEAD_DIM) for h in range(n_heads)]
    qs = [(q_ref[:, sl] * (SCALE * LOG2E)).astype(BF16) for sl in heads]
    u = u_ref[...]

    @pl.when(c == 0)
    def _():
        reps = 2 * tk // t
        k2s = [jnp.concatenate([kn_ref[:, sl].astype(BF16)] * reps, axis=0) for sl in heads]
        v2s = [jnp.concatenate([vn_ref[:, sl].astype(BF16)] * reps, axis=0) for sl in heads]
        before = (lax.broadcasted_iota(jnp.int32, (t, 2 * tk), 1) < lax.broadcasted_iota(jnp.int32, (t, 2 * tk), 0))
        acc, carry = _sb_sweep_step(qs, k2s, v2s, _rows([before] * n_heads), u,
                                    jnp.zeros(acc_ref.shape, F32), jnp.zeros(carry_ref.shape, F32))
        acc_ref[...] = acc
        carry_ref[...] = carry

    for h in range(n_heads):
        kb_ref[h] = kc_ref[pl.ds(h, ck, stride=n_heads), :].astype(BF16)
        vb_ref[h] = vc_ref[pl.ds(h, ck, stride=n_heads), :].astype(BF16)
    n_steps = ck // (2 * tk)

    def body(s, state):
        off = pl.multiple_of((n_steps - 1 - s) * 2 * tk, 2 * tk)
        k2s = [kb_ref[h, pl.ds(off, 2 * tk), :] for h in range(n_heads)]
        v2s = [vb_ref[h, pl.ds(off, 2 * tk), :] for h in range(n_heads)]
        return _sb_sweep_step(qs, k2s, v2s, None, u, *state)

    acc, carry = lax.fori_loop(0, n_steps, body, (acc_ref[...], carry_ref[...]), unroll=2)
    acc_ref[...] = acc
    carry_ref[...] = carry

    @pl.when(c == pl.num_programs(1) - 1)
    def _():
        for h, sl in enumerate(heads):
            o_ref[:, sl] = acc_ref[h * t:(h + 1) * t, :].astype(o_ref.dtype)


def sb_sample(p, cache_k, cache_v, into, row0, nb, t, q_col, k_col, v_col):
    _, past, n_heads, _ = cache_k.shape
    tk = HEAD_DIM
    w = n_heads * HEAD_DIM
    ck = _tile(past, 1024, 2 * tk)
    n_c = past // ck
    assert (2 * tk) % t == 0 and row0 % t == 0
    r0 = row0 // t
    cache_spec = pl.BlockSpec((None, ck * n_heads, HEAD_DIM), lambda b, c: (b, n_c - 1 - c, 0))
    cache_k = cache_k.reshape(nb, past * n_heads, HEAD_DIM)
    cache_v = cache_v.reshape(nb, past * n_heads, HEAD_DIM)
    return pl.pallas_call(
        functools.partial(_sb_sample_kernel, t=t, tk=tk, ck=ck, n_heads=n_heads),
        grid=(nb, n_c),
        in_specs=[pl.BlockSpec((t, w), lambda b, c: (r0 + b, q_col)),
                  pl.BlockSpec((t, w), lambda b, c: (r0 + b, k_col)),
                  pl.BlockSpec((t, w), lambda b, c: (r0 + b, v_col)),
                  cache_spec, cache_spec,
                  pl.BlockSpec((2 * tk, 2 * tk), lambda b, c: (0, 0)),
                  pl.BlockSpec(memory_space=pl.ANY)],
        out_specs=pl.BlockSpec((t, w), lambda b, c: (r0 + b, 0)),
        out_shape=jax.ShapeDtypeStruct(into.shape, into.dtype),
        input_output_aliases={6: 0},
        scratch_shapes=[pltpu.VMEM((n_heads * t, HEAD_DIM), F32), pltpu.VMEM((n_heads * t, tk), F32),
                        pltpu.VMEM((n_heads, ck, HEAD_DIM), BF16), pltpu.VMEM((n_heads, ck, HEAD_DIM), BF16)],
        compiler_params=_params("parallel", "arbitrary"),
        name="sb_sample",
    )(p, p, p, cache_k, cache_v, _cumsum_matrix(tk), into)


def _retention_log_decay(n_heads):
    return [float(np.log1p(-np.float32(2.0 ** (-5.0 - h)))) for h in range(n_heads)]


def _retention_kernel(q_ref, k_ref, v_ref, g_ref, cos_ref, sin_ref, gn_ref, s0_ref, *rest, t, n_chunks, log_g):
    o_ref, sout_ref, state_ref, dec_ref = rest[-4:]
    c = pl.program_id(1)

    @pl.when(c == 0)
    def _():
        state_ref[...] = s0_ref[...]
        rel = (lax.broadcasted_iota(jnp.int32, (t, t), 0) - lax.broadcasted_iota(jnp.int32, (t, t), 1)).astype(F32)
        for h, lg in enumerate(log_g):
            dec_ref[h] = jnp.where(rel >= 0, jnp.exp(lg * jnp.maximum(rel, 0.0)), 0.0)

    cos, sin = cos_ref[...], sin_ref[...]
    idx = lax.broadcasted_iota(jnp.int32, (t, HEAD_DIM), 0).astype(F32)
    for h, lg in enumerate(log_g):
        sl = slice(h * HEAD_DIM, (h + 1) * HEAD_DIM)
        qh, kh = q_ref[:, sl], k_ref[:, sl]
        qh = qh * cos + pltpu.roll(qh, HEAD_DIM // 2, 1) * sin
        kh = (kh * cos + pltpu.roll(kh, HEAD_DIM // 2, 1) * sin) * SCALE
        qb, vb = qh.astype(BF16), v_ref[:, sl].astype(BF16)
        scores = _dot_nt(qb, kh.astype(BF16)) * dec_ref[h]
        state = state_ref[h]
        o = _dot(scores.astype(BF16), vb) + _dot(qb, state.astype(BF16)) * jnp.exp((idx + 1.0) * lg)
        k_dec = (kh * jnp.exp((t - 1.0 - idx) * lg)).astype(BF16)
        state_ref[h] = float(np.exp(np.float32(t * lg))) * state + _dot_tn(k_dec, vb)
        o = o * lax.rsqrt(jnp.mean(o * o, axis=-1, keepdims=True) + RMS_EPS) * gn_ref[:, sl]
        o_ref[:, sl] = (o * _silu(g_ref[:, sl])).astype(o_ref.dtype)

    @pl.when(c == n_chunks - 1)
    def _():
        sout_ref[...] = state_ref[...]


def retention(p, cos2, sin2, ret_norm_g, state0, into, row0, nb, seq, t, n_heads, q_col, k_col, v_col, g_col):
    w = n_heads * HEAD_DIM
    n_chunks = seq // t
    assert row0 % t == 0 and seq % t == 0
    r0 = row0 // t

    def rows(col):
        return pl.BlockSpec((t, w), lambda b, c: (r0 + b * n_chunks + c, col))

    args = [p, p, p, p, cos2, sin2, ret_norm_g.reshape(1, w).astype(F32), state0]
    in_specs = [rows(q_col), rows(k_col), rows(v_col), rows(g_col),
                pl.BlockSpec((t, HEAD_DIM), lambda b, c: (c, 0)),
                pl.BlockSpec((t, HEAD_DIM), lambda b, c: (c, 0)),
                pl.BlockSpec((1, w), lambda b, c: (0, 0)),
                pl.BlockSpec((None, n_heads, HEAD_DIM, HEAD_DIM), lambda b, c: (b, 0, 0, 0))]
    aliases = {}
    if into is not None:
        aliases = {len(args): 0}
        args.append(into)
        in_specs.append(pl.BlockSpec(memory_space=pl.ANY))
    return pl.pallas_call(
        functools.partial(_retention_kernel, t=t, n_chunks=n_chunks, log_g=_retention_log_decay(n_heads)),
        grid=(nb, n_chunks),
        in_specs=in_specs,
        out_specs=[rows(0),
                   pl.BlockSpec((None, n_heads, HEAD_DIM, HEAD_DIM), lambda b, c: (b, 0, 0, 0))],
        out_shape=[jax.ShapeDtypeStruct((p.shape[0], w), BF16),
                   jax.ShapeDtypeStruct((nb, n_heads, HEAD_DIM, HEAD_DIM), F32)],
        input_output_aliases=aliases,
        scratch_shapes=[pltpu.VMEM((n_heads, HEAD_DIM, HEAD_DIM), F32), pltpu.VMEM((n_heads, t, t), F32)],
        compiler_params=_params("parallel", "arbitrary"),
        name="retention",
    )(*args)


def _rope_tables(pos):
    half = HEAD_DIM // 2
    inv = ROPE_BASE ** (-jnp.arange(half, dtype=F32) / half)
    ang = pos.astype(F32)[:, None] * inv[None, :]
    cos, sin = jnp.cos(ang), jnp.sin(ang)
    return jnp.concatenate([cos, cos], axis=1), jnp.concatenate([-sin, sin], axis=1)


def _band_bias_table(rel_bias, tq):
    n_heads = rel_bias.shape[0]
    w = BAND_WINDOW + tq
    n = tq + w - 1
    dist = BAND_WINDOW + tq - 1 - np.arange(n)
    vec = rel_bias.astype(F32)[:, np.clip(dist, -MAX_REL, MAX_REL) + MAX_REL]
    vec = jnp.roll(vec, -(tq - 1), axis=1)
    bias = jnp.tile(vec, (1, tq))[:, :tq * (n - 1)].reshape(n_heads, tq, n - 1)[:, :, :w]
    t = np.arange(tq)[:, None]
    j = np.arange(w)[None, :]
    lo = (t // CHUNK) * CHUNK
    in_band = (j >= lo) & (j < lo + BAND_WINDOW + CHUNK)
    return jnp.where(jnp.asarray(in_band)[None], bias, NEG_INF)


def _softmax_pv(scores, values):
    m = scores[0].max(axis=1, keepdims=True)
    for s in scores[1:]:
        m = jnp.maximum(m, s.max(axis=1, keepdims=True))
    ps = [jnp.exp(s - m) for s in scores]
    denom = ps[0].sum(axis=1, keepdims=True)
    for p in ps[1:]:
        denom = denom + p.sum(axis=1, keepdims=True)
    o = _dot(ps[0].astype(BF16), values[0])
    for p, v in zip(ps[1:], values[1:]):
        o = o + _dot(p.astype(BF16), v)
    return o * (1.0 / denom)


BAND_HEADS_PER_STEP = 4


def _band_prompt_kernel(*refs, tq, n_kb, hb):
    q_ref = refs[0]
    k_refs, v_refs = refs[1:1 + n_kb], refs[1 + n_kb:1 + 2 * n_kb]
    b_ref, o_ref = refs[1 + 2 * n_kb], refs[2 + 2 * n_kb]
    i = pl.program_id(1)
    for h in range(hb):
        sl = slice(h * HEAD_DIM, (h + 1) * HEAD_DIM)
        q = (q_ref[:, sl] * SCALE).astype(BF16)
        scores, values = [], []
        for kb in range(n_kb):
            s = _dot_nt(q, k_refs[kb][:, sl].astype(BF16)) + b_ref[h, :, kb * tq:(kb + 1) * tq]
            scores.append(s + jnp.where(i - (n_kb - 1) + kb >= 0, 0.0, NEG_INF))
            values.append(v_refs[kb][:, sl].astype(BF16))
        o_ref[:, sl] = _softmax_pv(scores, values).astype(o_ref.dtype)


def band_prompt(p, rel_bias, s, n_heads, q_col, k_col, v_col):
    tq = _tile(s, 256)
    hb = BAND_HEADS_PER_STEP
    w = hb * HEAD_DIM
    assert BAND_WINDOW % tq == 0 and tq % CHUNK == 0 and n_heads % hb == 0
    assert q_col % hb == 0 and k_col % hb == 0 and v_col % hb == 0
    n_kb = BAND_WINDOW // tq + 1

    def kv_specs(col):
        return [pl.BlockSpec((tq, w), lambda h, i, kb=kb: (jnp.maximum(i - (n_kb - 1) + kb, 0), col // hb + h))
                for kb in range(n_kb)]

    return pl.pallas_call(
        functools.partial(_band_prompt_kernel, tq=tq, n_kb=n_kb, hb=hb),
        grid=(n_heads // hb, s // tq),
        in_specs=([pl.BlockSpec((tq, w), lambda h, i: (i, q_col // hb + h))] + kv_specs(k_col) + kv_specs(v_col)
                  + [pl.BlockSpec((hb, tq, BAND_WINDOW + tq), lambda h, i: (h, 0, 0))]),
        out_specs=pl.BlockSpec((tq, w), lambda h, i: (i, h)),
        out_shape=jax.ShapeDtypeStruct((p.shape[0], n_heads * HEAD_DIM), BF16),
        compiler_params=_params("parallel", "parallel"),
        name="band_prompt",
    )(*([p] * (1 + 2 * n_kb)), _band_bias_table(rel_bias, tq))


def _band_sample_kernel(q_ref, kn_ref, vn_ref, kc_ref, vc_ref, b_ref, _into_ref, o_ref, ko_ref, vo_ref, *,
                        band_past, t, n_heads):
    keep = (band_past - t) * n_heads
    for h in range(n_heads):
        sl = slice(h * HEAD_DIM, (h + 1) * HEAD_DIM)
        q = (q_ref[:, sl] * SCALE).astype(BF16)
        kn, vn = kn_ref[:, sl], vn_ref[:, sl]
        kc = kc_ref[pl.ds(h, band_past, stride=n_heads), :].astype(BF16)
        vc = vc_ref[pl.ds(h, band_past, stride=n_heads), :].astype(BF16)
        s_c = _dot_nt(q, kc) + b_ref[h, :, :band_past]
        s_n = _dot_nt(q, kn.astype(BF16)) + b_ref[h, :, band_past:]
        o_ref[:, sl] = _softmax_pv([s_c, s_n], [vc, vn.astype(BF16)]).astype(o_ref.dtype)
        ko_ref[pl.ds(keep + h, t, stride=n_heads), :] = kn
        vo_ref[pl.ds(keep + h, t, stride=n_heads), :] = vn
    ko_ref[:keep] = kc_ref[t * n_heads:]
    vo_ref[:keep] = vc_ref[t * n_heads:]


def band_sample(p, cache_k, cache_v, rel_bias, into, row0, nb, t, q_col, k_col, v_col):
    _, band_past, n_heads, _ = cache_k.shape
    w = n_heads * HEAD_DIM
    assert t == CHUNK and band_past == BAND_WINDOW and row0 % t == 0
    r0 = row0 // t
    cache_spec = pl.BlockSpec((None, band_past * n_heads, HEAD_DIM), lambda b: (b, 0, 0))
    cache_shape = jax.ShapeDtypeStruct((nb, band_past * n_heads, HEAD_DIM), F32)
    cache_k = cache_k.reshape(cache_shape.shape)
    cache_v = cache_v.reshape(cache_shape.shape)
    o, k_out, v_out = pl.pallas_call(
        functools.partial(_band_sample_kernel, band_past=band_past, t=t, n_heads=n_heads),
        grid=(nb,),
        in_specs=[pl.BlockSpec((t, w), lambda b: (r0 + b, q_col)),
                  pl.BlockSpec((t, w), lambda b: (r0 + b, k_col)),
                  pl.BlockSpec((t, w), lambda b: (r0 + b, v_col)),
                  cache_spec, cache_spec,
                  pl.BlockSpec((n_heads, t, band_past + t), lambda b: (0, 0, 0)),
                  pl.BlockSpec(memory_space=pl.ANY)],
        out_specs=[pl.BlockSpec((t, w), lambda b: (r0 + b, 0)), cache_spec, cache_spec],
        out_shape=[jax.ShapeDtypeStruct(into.shape, into.dtype), cache_shape, cache_shape],
        input_output_aliases={6: 0},
        compiler_params=_params("parallel"),
        name="band_sample",
    )(p, p, p, cache_k.astype(F32), cache_v.astype(F32), _band_bias_table(rel_bias, t), into)
    out_4d = (nb, band_past, n_heads, HEAD_DIM)
    return o, k_out.reshape(out_4d), v_out.reshape(out_4d)


def _cross_attn_kernel(q_ref, mk_ref, mv_ref, *rest, n_heads, n_mem):
    o_ref = rest[-1]
    for h in range(n_heads):
        sl = slice(h * HEAD_DIM, (h + 1) * HEAD_DIM)
        q = (q_ref[:, sl] * SCALE).astype(BF16)
        mk = mk_ref[pl.ds(h, n_mem, stride=n_heads), :].astype(BF16)
        mv = mv_ref[pl.ds(h, n_mem, stride=n_heads), :].astype(BF16)
        o_ref[:, sl] = _softmax_pv([_dot_nt(q, mk)], [mv]).astype(o_ref.dtype)


def cross_attn(q, mk, mv, into, row0, nb, seq):
    _, n_mem, n_heads, _ = mk.shape
    w = n_heads * HEAD_DIM
    tq = _tile(seq, 512)
    n_t = seq // tq
    assert row0 % tq == 0
    r0 = row0 // tq
    mem_spec = pl.BlockSpec((None, n_mem * n_heads, HEAD_DIM), lambda b, i: (b, 0, 0))
    args = [q, mk.reshape(nb, n_mem * n_heads, HEAD_DIM), mv.reshape(nb, n_mem * n_heads, HEAD_DIM)]
    in_specs = [pl.BlockSpec((tq, w), lambda b, i: (r0 + b * n_t + i, 0)), mem_spec, mem_spec]
    aliases = {}
    if into is not None:
        aliases = {len(args): 0}
        args.append(into)
        in_specs.append(pl.BlockSpec(memory_space=pl.ANY))
    return pl.pallas_call(
        functools.partial(_cross_attn_kernel, n_heads=n_heads, n_mem=n_mem),
        grid=(nb, n_t),
        in_specs=in_specs,
        out_specs=pl.BlockSpec((tq, w), lambda b, i: (r0 + b * n_t + i, 0)),
        out_shape=jax.ShapeDtypeStruct((q.shape[0], w), BF16),
        input_output_aliases=aliases,
        compiler_params=_params("parallel", "parallel"),
        name="cross_attn",
    )(*args)


def kernel(x_prompt, x_sample, cache_sb_k, cache_sb_v, state_ret, cache_band_k, cache_band_v, cache_mem_k, cache_mem_v, mem_prompt, w_in_ab, w_out_ab, ret_norm_g, w_qkv_band, w_out_band, rel_bias_band, norm_g, mem_norm_g, w_xq, w_xk, w_xv, w_xo, ffn_w_gate, ffn_w_up, ffn_w_down, moe_router, moe_w_gate, moe_w_up, moe_w_down, final_norm_g):
    bp, s, d = x_prompt.shape
    nb, t, _ = x_sample.shape
    assert bp == 1
    past = cache_sb_k.shape[1]
    band_past = cache_band_k.shape[1]
    h_sb = cache_sb_k.shape[2]
    h_ret = state_ret.shape[1]
    h_band = cache_band_k.shape[2]
    d_sb, d_ret = h_sb * HEAD_DIM, h_ret * HEAD_DIM
    n_mem = mem_prompt.shape[1]
    d_x = w_xq.shape[2]
    depth = w_xq.shape[0]
    d_ff = ffn_w_gate.shape[1]
    ms = nb * t
    assert d_sb == d_ret and 3 * d_sb + 4 * d_ret == w_in_ab.shape[1]

    x = jnp.concatenate([x_prompt.reshape(s, d), x_sample.reshape(ms, d)], axis=0)
    mem = mem_prompt.reshape(n_mem, d)
    bf = lambda a: a.astype(BF16)

    cos_p, sin_p = _rope_tables(jnp.arange(s, dtype=jnp.int32))
    cos_s, sin_s = _rope_tables(past + jnp.arange(t, dtype=jnp.int32))

    mem_k_list, mem_v_list = [], []
    for l in range(depth):
        mem_n = rmsnorm(mem, mem_norm_g[l], BF16)
        mk_p = matmul([(mem_n, 0, 0)], bf(w_xk[l]), d)
        mv_p = matmul([(mem_n, 0, 0)], bf(w_xv[l]), d)
        mem_k_list.append(mk_p.reshape(1, n_mem, H_X, HEAD_DIM))
        mem_v_list.append(mv_p.reshape(1, n_mem, H_X, HEAD_DIM))

        if l % 2 == 0:
            p = norm_linear(x, norm_g[l, 0], [w_in_ab], F32, tn_pref=1024)
            o_sb = sb_prompt(p, s, h_sb, 0, h_sb, 2 * h_sb)
            o_sb = sb_sample(p, cache_sb_k, cache_sb_v, o_sb, s, nb, t, 0, 1, 2)
            t_ret = _tile(s, 256, CHUNK)
            o_r, ret_state_prompt = retention(p, cos_p, sin_p, ret_norm_g, jnp.zeros((1,) + state_ret.shape[1:], F32),
                                              None, 0, 1, s, t_ret, h_ret, 3, 4, 5, 6)
            o_r, ret_state_sample = retention(p, cos_s, sin_s, ret_norm_g, state_ret.astype(F32),
                                              o_r, s, nb, t, t, h_ret, 3, 4, 5, 6)
            x = matmul([(o_sb, 0, 0), (o_r, 0, d_sb)], bf(w_out_ab), d_sb, residual=x)
            sb_k_prompt = p[:s, d_sb:2 * d_sb].reshape(1, s, h_sb, HEAD_DIM)
            sb_v_prompt = p[:s, 2 * d_sb:3 * d_sb].reshape(1, s, h_sb, HEAD_DIM)
            sb_k_sample = p[s:, d_sb:2 * d_sb].reshape(nb, t, h_sb, HEAD_DIM)
            sb_v_sample = p[s:, 2 * d_sb:3 * d_sb].reshape(nb, t, h_sb, HEAD_DIM)
        else:
            d_band = h_band * HEAD_DIM
            p = norm_linear(x, norm_g[l, 0], [w_qkv_band], F32, tn_pref=1024)
            o = band_prompt(p, rel_bias_band, s, h_band, 0, h_band, 2 * h_band)
            o, band_k_sample, band_v_sample = band_sample(p, cache_band_k, cache_band_v, rel_bias_band, o,
                                                          s, nb, t, 0, 1, 2)
            x = matmul([(o, 0, 0)], bf(w_out_band), d_band, residual=x)
            band_k_prompt = p[s - band_past:s, d_band:2 * d_band].reshape(1, band_past, h_band, HEAD_DIM)
            band_v_prompt = p[s - band_past:s, 2 * d_band:].reshape(1, band_past, h_band, HEAD_DIM)

        q = norm_linear(x, norm_g[l, 1], [w_xq[l]], F32, tn_pref=1024)
        o = cross_attn(q, mem_k_list[-1], mem_v_list[-1], None, 0, 1, s)
        o = cross_attn(q, cache_mem_k[l], cache_mem_v[l], o, s, nb, t)
        x = matmul([(o, 0, 0)], bf(w_xo[l]), d_x, residual=x)

        if l % 2 == 0:
            act = norm_linear(x, norm_g[l, 2], [ffn_w_gate, ffn_w_up], BF16, tn_pref=512)
            x = matmul([(act, 0, 0)], bf(ffn_w_down), d_ff, residual=x, tk_pref=d_ff)
        else:
            h = rmsnorm(x, norm_g[l, 2], BF16)
            gates, idx = moe_route(x, norm_g[l, 2], moe_router)
            x = moe_top2(x, h, idx, gates, bf(moe_w_gate), bf(moe_w_up), bf(moe_w_down))

    y_prompt = rmsnorm(x, final_norm_g, F32, 0, s).reshape(1, s, d)
    y_sample = rmsnorm(x, final_norm_g, F32, s, ms).reshape(nb, t, d)
    mem_k_prompt = jnp.stack(mem_k_list, axis=0)
    mem_v_prompt = jnp.stack(mem_v_list, axis=0)
    return (y_prompt, y_sample, sb_k_prompt, sb_v_prompt, sb_k_sample, sb_v_sample,
            ret_state_prompt, ret_state_sample, band_k_prompt, band_v_prompt,
            band_k_sample, band_v_sample, mem_k_prompt, mem_v_prompt)
```

```python
import functools

import numpy as np
import jax
import jax.numpy as jnp
from jax import lax
from jax.experimental import pallas as pl
from jax.experimental.pallas import tpu as pltpu

F32 = jnp.float32
BF16 = jnp.bfloat16

HEAD_DIM = 128
CHUNK = 64
N_BAND_CHUNKS = 8
BAND_WINDOW = N_BAND_CHUNKS * CHUNK
MAX_REL = 128
H_X = 4
TOP_K = 2
RMS_EPS = 1e-6
ROPE_BASE = 10000.0
NEG_INF = -1e30
SCALE = HEAD_DIM ** -0.5
LOG2E = 1.4426950408889634

VMEM_LIMIT_BYTES = 56 * 1024 * 1024


def _params(*sem):
    return pltpu.CompilerParams(dimension_semantics=sem, vmem_limit_bytes=VMEM_LIMIT_BYTES)


def _tile(n, pref, mult=8):
    t = min(n, pref)
    while t > mult and (n % t or t % mult):
        t -= mult
    assert n % t == 0, (n, pref)
    return t


def _dot(a, b):
    return jnp.dot(a, b, preferred_element_type=F32)


def _dot_nt(a, b):
    return lax.dot_general(a, b, (((1,), (1,)), ((), ())), preferred_element_type=F32)


def _dot_tn(a, b):
    return lax.dot_general(a, b, (((0,), (0,)), ((), ())), preferred_element_type=F32)


def _silu(a):
    return a * (1.0 / (1.0 + jnp.exp(-a)))


def _rows(parts):
    return parts[0] if len(parts) == 1 else jnp.concatenate(parts, axis=0)


def _rmsnorm_kernel(x_ref, g_ref, o_ref):
    x = x_ref[...]
    y = x * lax.rsqrt(jnp.mean(x * x, axis=-1, keepdims=True) + RMS_EPS) * g_ref[...]
    o_ref[...] = y.astype(o_ref.dtype)


def rmsnorm(x, g, out_dtype, row0=0, rows=None):
    d = x.shape[1]
    m = x.shape[0] - row0 if rows is None else rows
    tm = _tile(m, 512)
    assert row0 % tm == 0
    return pl.pallas_call(
        _rmsnorm_kernel,
        grid=(m // tm,),
        in_specs=[pl.BlockSpec((tm, d), lambda i: (row0 // tm + i, 0)),
                  pl.BlockSpec((1, d), lambda i: (0, 0))],
        out_specs=pl.BlockSpec((tm, d), lambda i: (i, 0)),
        out_shape=jax.ShapeDtypeStruct((m, d), out_dtype),
        compiler_params=_params("parallel"),
        name="rmsnorm",
    )(x, g.reshape(1, d).astype(F32))


def _mm_kernel(*refs, n_parts, has_res, nk):
    xs, ws = refs[:n_parts], refs[n_parts:2 * n_parts]
    pos = 2 * n_parts
    res_ref = refs[pos] if has_res else None
    pos += int(has_res)
    o_ref = refs[pos]
    part = _dot(xs[0][...], ws[0][...])
    for x_ref, w_ref in zip(xs[1:], ws[1:]):
        part = part + _dot(x_ref[...], w_ref[...])
    if nk == 1:
        if has_res:
            part = res_ref[...] + part
        o_ref[...] = part.astype(o_ref.dtype)
        return
    acc_ref = refs[pos + 1]
    k = pl.program_id(2)

    @pl.when(k == 0)
    def _():
        acc_ref[...] = part

    @pl.when(k > 0)
    def _():
        acc_ref[...] += part

    @pl.when(k == nk - 1)
    def _():
        out = acc_ref[...]
        if has_res:
            out = res_ref[...] + out
        o_ref[...] = out.astype(o_ref.dtype)


def matmul(parts, w, k_part, *, col_off=0, n_cols=None, out_dtype=F32, residual=None,
           tm_pref=1024, tn_pref=512, tk_pref=2048):
    m = parts[0][0].shape[0]
    n_cols = w.shape[1] - col_off if n_cols is None else n_cols
    tm = _tile(m, tm_pref)
    tn = _tile(n_cols, tn_pref, 128)
    tk = _tile(k_part, tk_pref, 128)
    nk = k_part // tk
    assert col_off % tn == 0
    in_specs, args = [], []
    for x, xo, _ in parts:
        assert xo % tk == 0
        in_specs.append(pl.BlockSpec((tm, tk), lambda i, j, k, xo=xo: (i, xo // tk + k)))
        args.append(x)
    for _, _, wo in parts:
        assert wo % tk == 0
        in_specs.append(pl.BlockSpec((tk, tn), lambda i, j, k, wo=wo: (wo // tk + k, col_off // tn + j)))
        args.append(w)
    if residual is not None:
        in_specs.append(pl.BlockSpec((tm, tn), lambda i, j, k: (i, j)))
        args.append(residual)
    return pl.pallas_call(
        functools.partial(_mm_kernel, n_parts=len(parts), has_res=residual is not None, nk=nk),
        grid=(m // tm, n_cols // tn, nk),
        in_specs=in_specs,
        out_specs=pl.BlockSpec((tm, tn), lambda i, j, k: (i, j)),
        out_shape=jax.ShapeDtypeStruct((m, n_cols), out_dtype),
        scratch_shapes=[pltpu.VMEM((tm, tn), F32)] if nk > 1 else [],
        compiler_params=_params("parallel", "parallel", "arbitrary"),
        name="matmul",
    )(*args)


def _norm_linear_kernel(x_ref, g_ref, *rest, n_w):
    w_refs, o_ref, wb_refs = rest[:n_w], rest[n_w], rest[n_w + 1:]

    @pl.when(pl.program_id(1) == 0)
    def _():
        for w_ref, wb_ref in zip(w_refs, wb_refs):
            wb_ref[...] = w_ref[...].astype(BF16)

    x = x_ref[...]
    h = (x * lax.rsqrt(jnp.mean(x * x, axis=-1, keepdims=True) + RMS_EPS) * g_ref[...]).astype(BF16)
    if n_w == 1:
        out = _dot(h, wb_refs[0][...])
    else:
        out = _silu(_dot(h, wb_refs[0][...])) * _dot(h, wb_refs[1][...])
    o_ref[...] = out.astype(o_ref.dtype)


def norm_linear(x, g, ws, out_dtype, *, tn_pref):
    m, d = x.shape
    n = ws[0].shape[1]
    tm = _tile(m, 1024)
    tn = _tile(n, tn_pref, 128)
    return pl.pallas_call(
        functools.partial(_norm_linear_kernel, n_w=len(ws)),
        grid=(n // tn, m // tm),
        in_specs=([pl.BlockSpec((tm, d), lambda j, i: (i, 0)), pl.BlockSpec((1, d), lambda j, i: (0, 0))]
                  + [pl.BlockSpec((d, tn), lambda j, i: (0, j))] * len(ws)),
        out_specs=pl.BlockSpec((tm, tn), lambda j, i: (i, j)),
        out_shape=jax.ShapeDtypeStruct((m, n), out_dtype),
        scratch_shapes=[pltpu.VMEM((d, tn), BF16)] * len(ws),
        compiler_params=_params("parallel", "arbitrary"),
        name="norm_linear",
    )(x, g.reshape(1, d).astype(F32), *[w.astype(F32) for w in ws])


def _router_kernel(x_ref, g_ref, r_ref, o_ref, *, n_experts):
    x = x_ref[...]
    hn = x * lax.rsqrt(jnp.mean(x * x, axis=-1, keepdims=True) + RMS_EPS) * g_ref[...]
    logits = jnp.dot(hn, r_ref[...], preferred_element_type=F32, precision=lax.Precision.HIGHEST)
    lane = lax.broadcasted_iota(jnp.int32, logits.shape, 1)
    n_lanes = logits.shape[1]
    lg = jnp.where(lane < n_experts, logits, -jnp.inf)
    m1 = jnp.max(lg, axis=1, keepdims=True)
    i1 = jnp.min(jnp.where(lg == m1, lane, n_lanes), axis=1, keepdims=True)
    lg2 = jnp.where(lane == i1, -jnp.inf, lg)
    m2 = jnp.max(lg2, axis=1, keepdims=True)
    i2 = jnp.min(jnp.where(lg2 == m2, lane, n_lanes), axis=1, keepdims=True)
    e2 = jnp.exp(m2 - m1)
    inv = 1.0 / (1.0 + e2)
    o_ref[...] = (jnp.where(lane == 0, inv, 0.0) + jnp.where(lane == 1, e2 * inv, 0.0)
                  + jnp.where(lane == 2, i1.astype(F32), 0.0) + jnp.where(lane == 3, i2.astype(F32), 0.0))


def moe_route(x, g, router):
    m, d = x.shape
    n_e = router.shape[1]
    tm = _tile(m, 512)
    r_pad = jnp.zeros((d, 128), F32).at[:, :n_e].set(router.astype(F32))
    out = pl.pallas_call(
        functools.partial(_router_kernel, n_experts=n_e),
        grid=(m // tm,),
        in_specs=[pl.BlockSpec((tm, d), lambda i: (i, 0)),
                  pl.BlockSpec((1, d), lambda i: (0, 0)),
                  pl.BlockSpec((d, 128), lambda i: (0, 0))],
        out_specs=pl.BlockSpec((tm, 128), lambda i: (i, 0)),
        out_shape=jax.ShapeDtypeStruct((m, 128), F32),
        compiler_params=_params("parallel"),
        name="moe_router",
    )(x, g.reshape(1, d).astype(F32), r_pad)
    return out[:, :TOP_K], out[:, TOP_K:2 * TOP_K].astype(jnp.int32)


MOE_ROW_TILE = 256
MOE_GATHER_CHUNK = 512
MOE_COMBINE_TILE = 512


def _i32(a):
    return a.astype(jnp.int32)


def _count_le(ends, v):
    return jnp.sum(_i32(ends[None, :] <= v[:, None]), axis=1, dtype=jnp.int32)


def _moe_plan(idx, gates, n_e, tr, tc, tt):
    m = idx.shape[0]
    assert (TOP_K * m) % tr == 0 and m % tc == 0 and m % tt == 0
    n_tiles = TOP_K * m // tr + n_e
    n_rows = n_tiles * tr
    routed = jnp.zeros((m, n_e), jnp.int32)
    for k in range(TOP_K):
        routed = routed + _i32(idx[:, k:k + 1] == jnp.arange(n_e, dtype=jnp.int32)[None, :])
    csum = jnp.cumsum(routed, axis=0, dtype=jnp.int32)
    rank = csum - routed
    tiles_e = (csum[-1] + tr - 1) // tr
    tile_end = jnp.cumsum(tiles_e, dtype=jnp.int32)
    row_start = (tile_end - tiles_e) * tr
    dest = row_start[idx] + jnp.take_along_axis(rank, idx, axis=1)
    gate_rows = jnp.zeros((n_rows,), F32).at[dest.reshape(-1)].set(gates.reshape(-1), unique_indices=True)
    tile_ids = jnp.arange(n_tiles, dtype=jnp.int32)
    tile_valid = tile_ids < tile_end[-1]
    tile_expert = jnp.minimum(_count_le(tile_end, tile_ids), n_e - 1)

    def visits(first_chunk, n_visits, n_max, owner_of):
        end = jnp.cumsum(n_visits, dtype=jnp.int32)
        start = end - n_visits
        v = jnp.arange(n_max, dtype=jnp.int32)
        live = v < end[-1]
        slot = jnp.minimum(_count_le(end, v), n_visits.shape[0] - 1)
        owner = owner_of(slot)
        chunk = first_chunk[slot] + v - start[slot]
        last = jnp.maximum(end[-1] - 1, 0)
        owner = jnp.where(live, owner, owner[last])
        chunk = jnp.where(live, chunk, chunk[last])
        first = live & ((v == 0) | (owner != jnp.roll(owner, 1)))
        return _i32(owner), _i32(chunk), _i32(first) + 2 * _i32(live)

    rank0 = (tile_ids - (tile_end - tiles_e)[tile_expert]) * tr
    rank1 = jnp.minimum(rank0 + tr, csum[-1][tile_expert]) - 1
    csum_e = csum.T[tile_expert]
    first_tok = jnp.sum(_i32(csum_e <= rank0[:, None]), axis=1, dtype=jnp.int32)
    last_tok = jnp.sum(_i32(csum_e <= rank1[:, None]), axis=1, dtype=jnp.int32)
    c_first = jnp.where(tile_valid, first_tok, 0) // tc
    c_last = jnp.where(tile_valid, last_tok, 0) // tc
    g_plan = visits(c_first, c_last - c_first + 1, n_tiles + n_e * (m // tc - 1), lambda s: s)
    n_tt = m // tt
    before = jnp.concatenate([jnp.zeros((1, n_e), jnp.int32), csum[tt - 1::tt]], axis=0)
    lo = row_start[None, :] + before[:-1]
    hi = row_start[None, :] + before[1:]
    n_vis = jnp.where(hi > lo, (hi - 1) // tr - lo // tr + 1, 0)
    c_plan = visits((lo // tr).reshape(-1), n_vis.reshape(-1), n_tt * n_e + n_tiles - 1, lambda s: s // n_e)
    return dict(n_tiles=n_tiles, dest=dest, gate_rows=gate_rows.reshape(n_rows, 1),
                tile_expert=tile_expert, tile_valid=_i32(tile_valid), gather=g_plan, combine=c_plan)


def _moe_select(dest_ref, row0, n_rows):
    row = row0 + lax.broadcasted_iota(jnp.int32, (dest_ref.shape[0], n_rows), 1)
    hit = row == dest_ref[:, 0:1]
    for k in range(1, TOP_K):
        hit = hit | (row == dest_ref[:, k:k + 1])
    return jnp.where(hit, 1.0, 0.0).astype(BF16)


def _moe_gather_kernel(vt_ref, vc_ref, vf_ref, dest_ref, h_ref, o_ref):
    v = pl.program_id(0)
    flags = vf_ref[v]
    tr = o_ref.shape[0]

    @pl.when(flags % 2 == 1)
    def _():
        o_ref[...] = jnp.zeros_like(o_ref)

    @pl.when(flags >= 2)
    def _():
        sel = _moe_select(dest_ref, vt_ref[v] * tr, tr)
        o_ref[...] += _dot_tn(sel, h_ref[...]).astype(o_ref.dtype)


def _moe_up_kernel(te_ref, tv_ref, x_ref, wg_ref, wu_ref, o_ref):
    j = pl.program_id(1)

    @pl.when(tv_ref[j] != 0)
    def _():
        x = x_ref[...]
        o_ref[...] = (_silu(_dot(x, wg_ref[...])) * _dot(x, wu_ref[...])).astype(o_ref.dtype)

    @pl.when(tv_ref[j] == 0)
    def _():
        o_ref[...] = jnp.zeros_like(o_ref)


def _moe_down_kernel(te_ref, a_ref, wd_ref, g_ref, o_ref):
    o_ref[...] = (_dot(a_ref[...], wd_ref[...]) * g_ref[...]).astype(o_ref.dtype)


def _moe_combine_kernel(vi_ref, vc_ref, vf_ref, dest_ref, y_ref, x_ref, o_ref):
    v = pl.program_id(0)
    flags = vf_ref[v]
    tr = y_ref.shape[0]

    @pl.when(flags % 2 == 1)
    def _():
        o_ref[...] = x_ref[...]

    @pl.when(flags >= 2)
    def _():
        sel = _moe_select(dest_ref, vc_ref[v] * tr, tr)
        o_ref[...] += _dot(sel, y_ref[...])


def moe_top2(x, h, idx, gates, wg, wu, wd):
    m, d = h.shape
    n_e, _, f = wg.shape
    tr, tc, tt = MOE_ROW_TILE, _tile(m, MOE_GATHER_CHUNK, 16), _tile(m, MOE_COMBINE_TILE)
    plan =_moe_plan(idx, gates, n_e, tr, tc, tt)
    n_tiles = plan["n_tiles"]
    n_rows = n_tiles * tr

    vt, vc, vf = plan["gather"]
    xs = pl.pallas_call(
        _moe_gather_kernel,
        grid_spec=pltpu.PrefetchScalarGridSpec(
            num_scalar_prefetch=3, grid=(vt.shape[0],),
            in_specs=[pl.BlockSpec((tc, TOP_K), lambda v, vt, vc, vf: (vc[v], 0)),
                      pl.BlockSpec((tc, d), lambda v, vt, vc, vf: (vc[v], 0))],
            out_specs=pl.BlockSpec((tr, d), lambda v, vt, vc, vf: (vt[v], 0))),
        out_shape=jax.ShapeDtypeStruct((n_rows, d), BF16),
        compiler_params=_params("arbitrary"),
        name="moe_gather",
    )(vt, vc, vf, plan["dest"], h)

    tf = _tile(f, 1408, 128)
    act = pl.pallas_call(
        _moe_up_kernel,
        grid_spec=pltpu.PrefetchScalarGridSpec(
            num_scalar_prefetch=2, grid=(f // tf, n_tiles),
            in_specs=[pl.BlockSpec((tr, d), lambda c, j, te, tv: (j, 0)),
                      pl.BlockSpec((None, d, tf), lambda c, j, te, tv: (te[j], 0, c)),
                      pl.BlockSpec((None, d, tf), lambda c, j, te, tv: (te[j], 0, c))],
            out_specs=pl.BlockSpec((tr, tf), lambda c, j, te, tv: (j, c))),
        out_shape=jax.ShapeDtypeStruct((n_rows, f), BF16),
        compiler_params=_params("parallel", "parallel"),
        name="moe_up",
    )(plan["tile_expert"], plan["tile_valid"], xs, wg, wu)

    y = pl.pallas_call(
        _moe_down_kernel,
        grid_spec=pltpu.PrefetchScalarGridSpec(
            num_scalar_prefetch=1, grid=(n_tiles,),
            in_specs=[pl.BlockSpec((tr, f), lambda j, te: (j, 0)),
                      pl.BlockSpec((None, f, d), lambda j, te: (te[j], 0, 0)),
                      pl.BlockSpec((tr, 1), lambda j, te: (j, 0))],
            out_specs=pl.BlockSpec((tr, d), lambda j, te: (j, 0))),
        out_shape=jax.ShapeDtypeStruct((n_rows, d), BF16),
        compiler_params=_params("parallel"),
        name="moe_down",
    )(plan["tile_expert"], act, wd, plan["gate_rows"])

    vi, vc, vf = plan["combine"]
    return pl.pallas_call(
        _moe_combine_kernel,
        grid_spec=pltpu.PrefetchScalarGridSpec(
            num_scalar_prefetch=3, grid=(vi.shape[0],),
            in_specs=[pl.BlockSpec((tt, TOP_K), lambda v, vi, vc, vf: (vi[v], 0)),
                      pl.BlockSpec((tr, d), lambda v, vi, vc, vf: (vc[v], 0)),
                      pl.BlockSpec((tt, d), lambda v, vi, vc, vf: (vi[v], 0))],
            out_specs=pl.BlockSpec((tt, d), lambda v, vi, vc, vf: (vi[v], 0))),
        out_shape=jax.ShapeDtypeStruct((m, d), F32),
        compiler_params=_params("arbitrary"),
        name="moe_combine",
    )(vi, vc, vf, plan["dest"], y, x)


def _cumsum_matrix(tk):
    r = np.arange(2 * tk)[:, None]
    c = np.arange(2 * tk)[None, :]
    return jnp.asarray((r > c).astype(np.float32), dtype=BF16)


def _sb_sweep_step(qs, k2s, v2s, before, u, acc, carry):
    tq = qs[0].shape[0]
    tk = k2s[0].shape[0] // 2
    z = _rows([_dot_nt(q, k2) for q, k2 in zip(qs, k2s)])
    neg_l = jnp.maximum(z, 0.0) + jnp.log2(1.0 + jnp.exp2(-jnp.abs(z)))
    log_b = z - neg_l
    if before is not None:
        neg_l = jnp.where(before, neg_l, 0.0)
    neg_lb = neg_l.astype(BF16)
    after = _dot(neg_lb, u)
    w = jnp.exp2(log_b - (after + jnp.concatenate([carry, carry], axis=1)))
    if before is not None:
        w = jnp.where(before, w, 0.0)
    w = w.astype(BF16)
    acc = acc + _rows([_dot(w[n * tq:(n + 1) * tq], v2) for n, v2 in enumerate(v2s)])
    total = after[:, :1] + neg_lb[:, :1].astype(F32)
    return acc, carry + total


def _sb_prompt_kernel(q_ref, k_ref, v_ref, u_ref, o_ref, kb_ref, vb_ref, *, tq, tk):
    i = pl.program_id(1)

    @pl.when(i == 0)
    def _():
        kb_ref[...] = k_ref[...].astype(BF16)
        vb_ref[...] = v_ref[...].astype(BF16)

    q = (q_ref[...] * (SCALE * LOG2E)).astype(BF16)
    u = u_ref[...]
    steps_per_tile = tq // (2 * tk)
    state = (jnp.zeros((tq, HEAD_DIM), F32), jnp.zeros((tq, tk), F32))

    def step(off, r0, before, state):
        new = _sb_sweep_step([q[r0:]], [kb_ref[pl.ds(off, 2 * tk), :]], [vb_ref[pl.ds(off, 2 * tk), :]],
                             before, u, *[a[r0:] for a in state])
        return tuple(jnp.concatenate([a[:r0], b], axis=0) if r0 else b for a, b in zip(state, new))

    for r0 in reversed(range(0, tq, 2 * tk)):
        before = (lax.broadcasted_iota(jnp.int32, (tq - r0, 2 * tk), 1)
                  < lax.broadcasted_iota(jnp.int32, (tq - r0, 2 * tk), 0))
        state = step(pl.multiple_of(i * tq + r0, 2 * tk), r0, before, state)
    n_steps = i * steps_per_tile
    unroll = max(u for u in (1, 2, 4) if steps_per_tile % u == 0)

    def body(s, st):
        for k in range(unroll):
            st = step(pl.multiple_of((n_steps - 1 - unroll * s - k) * 2 * tk, 2 * tk), 0, None, st)
        return st

    acc, _ = lax.fori_loop(0, n_steps // unroll, body, state)
    o_ref[...] = acc.astype(o_ref.dtype)


def sb_prompt(p, s, n_heads, q_col, k_col, v_col):
    tk = HEAD_DIM
    tq = _tile(s, 1024, 2 * tk)
    return pl.pallas_call(
        functools.partial(_sb_prompt_kernel, tq=tq, tk=tk),
        grid=(n_heads, s // tq),
        in_specs=[pl.BlockSpec((tq, HEAD_DIM), lambda h, i: (i, q_col + h)),
                  pl.BlockSpec((s, HEAD_DIM), lambda h, i: (0, k_col + h)),
                  pl.BlockSpec((s, HEAD_DIM), lambda h, i: (0, v_col + h)),
                  pl.BlockSpec((2 * tk, 2 * tk), lambda h, i: (0, 0))],
        out_specs=pl.BlockSpec((tq, HEAD_DIM), lambda h, i: (i, h)),
        out_shape=jax.ShapeDtypeStruct((p.shape[0], n_heads * HEAD_DIM), BF16),
        scratch_shapes=[pltpu.VMEM((s, HEAD_DIM), BF16), pltpu.VMEM((s, HEAD_DIM), BF16)],
        compiler_params=_params("parallel", "arbitrary"),
        name="sb_prompt",
    )(p, p, p, _cumsum_matrix(tk))


def _sb_sample_kernel(q_ref, kn_ref, vn_ref, kc_hbm, vc_hbm, u_ref, _into_ref, o_ref,
                      acc_ref, carry_ref, kf_ref, vf_ref, sem, *, t, tk, ck, n_heads):
    b, c = pl.program_id(0), pl.program_id(1)
    n_c = pl.num_programs(1)
    step = b * n_c + c
    slot = step % 2

    def chunk_copies(step, slot):
        first = (n_c - 1 - step % n_c) * ck
        copies = []
        for h in range(n_heads):
            for hbm, buf in ((kc_hbm, kf_ref), (vc_hbm, vf_ref)):
                copies.append(pltpu.make_async_copy(hbm.at[step // n_c, pl.ds(first, ck), h, :],
                                                    buf.at[slot, h], sem.at[slot]))
        return copies

    @pl.when(step == 0)
    def _():
        for cp in chunk_copies(step, slot):
            cp.start()

    @pl.when(step + 1 < pl.num_programs(0) * n_c)
    def _():
        for cp in chunk_copies(step + 1, 1 - slot):
            cp.start()

    heads = [slice(h * HEAD_DIM, (h + 1) * HEAD_DIM) for h in range(n_heads)]
    qs = [(q_ref[:, sl] * (SCALE * LOG2E)).astype(BF16) for sl in heads]
    u = u_ref[...]

    @pl.when(c == 0)
    def _():
        reps = 2 * tk // t
        k2s = [jnp.concatenate([kn_ref[:, sl].astype(BF16)] * reps, axis=0) for sl in heads]
        v2s = [jnp.concatenate([vn_ref[:, sl].astype(BF16)] * reps, axis=0) for sl in heads]
        before = (lax.broadcasted_iota(jnp.int32, (t, 2 * tk), 1) < lax.broadcasted_iota(jnp.int32, (t, 2 * tk), 0))
        acc, carry = _sb_sweep_step(qs, k2s, v2s, _rows([before] * n_heads), u,
                                    jnp.zeros(acc_ref.shape, F32), jnp.zeros(carry_ref.shape, F32))
        acc_ref[...] = acc
        carry_ref[...] = carry

    for cp in chunk_copies(step, slot):
        cp.wait()
    n_steps = ck // (2 * tk)

    def body(s, state):
        off = pl.multiple_of((n_steps - 1 - s) * 2 * tk, 2 * tk)
        k2s = [kf_ref[slot, h, pl.ds(off, 2 * tk), :].astype(BF16) for h in range(n_heads)]
        v2s = [vf_ref[slot, h, pl.ds(off, 2 * tk), :].astype(BF16) for h in range(n_heads)]
        return _sb_sweep_step(qs, k2s, v2s, None, u, *state)

    acc, carry = lax.fori_loop(0, n_steps, body, (acc_ref[...], carry_ref[...]), unroll=2)
    acc_ref[...] = acc
    carry_ref[...] = carry

    @pl.when(c == pl.num_programs(1) - 1)
    def _():
        for h, sl in enumerate(heads):
            o_ref[:, sl] = acc_ref[h * t:(h + 1) * t, :].astype(o_ref.dtype)


def sb_sample(p, cache_k, cache_v, into, row0, nb, t, q_col, k_col, v_col):
    _, past, n_heads, _ = cache_k.shape
    tk = HEAD_DIM
    w = n_heads * HEAD_DIM
    ck = _tile(past, 1024, 2 * tk)
    n_c = past // ck
    assert (2 * tk) % t == 0 and row0 % t == 0
    r0 = row0 // t
    return pl.pallas_call(
        functools.partial(_sb_sample_kernel, t=t, tk=tk, ck=ck, n_heads=n_heads),
        grid=(nb, n_c),
        in_specs=[pl.BlockSpec((t, w), lambda b, c: (r0 + b, q_col)),
                  pl.BlockSpec((t, w), lambda b, c: (r0 + b, k_col)),
                  pl.BlockSpec((t, w), lambda b, c: (r0 + b, v_col)),
                  pl.BlockSpec(memory_space=pl.ANY), pl.BlockSpec(memory_space=pl.ANY),
                  pl.BlockSpec((2 * tk, 2 * tk), lambda b, c: (0, 0)),
                  pl.BlockSpec(memory_space=pl.ANY)],
        out_specs=pl.BlockSpec((t, w), lambda b, c: (r0 + b, 0)),
        out_shape=jax.ShapeDtypeStruct(into.shape, into.dtype),
        input_output_aliases={6: 0},
        scratch_shapes=[pltpu.VMEM((n_heads * t, HEAD_DIM), F32), pltpu.VMEM((n_heads * t, tk), F32),
                        pltpu.VMEM((2, n_heads, ck, HEAD_DIM), F32), pltpu.VMEM((2, n_heads, ck, HEAD_DIM), F32),
                        pltpu.SemaphoreType.DMA((2,))],
        compiler_params=_params("arbitrary", "arbitrary"),
        name="sb_sample",
    )(p, p, p, cache_k.astype(F32), cache_v.astype(F32), _cumsum_matrix(tk), into)


def _retention_log_decay(n_heads):
    return [float(np.log1p(-np.float32(2.0 ** (-5.0 - h)))) for h in range(n_heads)]


def _retention_kernel(q_ref, k_ref, v_ref, g_ref, cos_ref, sin_ref, gn_ref, s0_ref, *rest, t, n_chunks, log_g):
    o_ref, sout_ref, state_ref, dec_ref = rest[-4:]
    c = pl.program_id(1)

    @pl.when(c == 0)
    def _():
        state_ref[...] = s0_ref[...]
        rel = (lax.broadcasted_iota(jnp.int32, (t, t), 0) - lax.broadcasted_iota(jnp.int32, (t, t), 1)).astype(F32)
        for h, lg in enumerate(log_g):
            dec_ref[h] = jnp.where(rel >= 0, jnp.exp(lg * jnp.maximum(rel, 0.0)), 0.0)

    cos, sin = cos_ref[...], sin_ref[...]
    idx = lax.broadcasted_iota(jnp.int32, (t, HEAD_DIM), 0).astype(F32)
    for h, lg in enumerate(log_g):
        sl = slice(h * HEAD_DIM, (h + 1) * HEAD_DIM)
        qh, kh = q_ref[:, sl], k_ref[:, sl]
        qh = qh * cos + pltpu.roll(qh, HEAD_DIM // 2, 1) * sin
        kh = (kh * cos + pltpu.roll(kh, HEAD_DIM // 2, 1) * sin) * SCALE
        qb, vb = qh.astype(BF16), v_ref[:, sl].astype(BF16)
        scores = _dot_nt(qb, kh.astype(BF16)) * dec_ref[h]
        state = state_ref[h]
        o = _dot(scores.astype(BF16), vb) + _dot(qb, state.astype(BF16)) * jnp.exp((idx + 1.0) * lg)
        k_dec = (kh * jnp.exp((t - 1.0 - idx) * lg)).astype(BF16)
        state_ref[h] = float(np.exp(np.float32(t * lg))) * state + _dot_tn(k_dec, vb)
        o = o * lax.rsqrt(jnp.mean(o * o, axis=-1, keepdims=True) + RMS_EPS) * gn_ref[:, sl]
        o_ref[:, sl] = (o * _silu(g_ref[:, sl])).astype(o_ref.dtype)

    @pl.when(c == n_chunks - 1)
    def _():
        sout_ref[...] = state_ref[...]


def retention(p, cos2, sin2, ret_norm_g, state0, into, row0, nb, seq, t, n_heads, q_col, k_col, v_col, g_col):
    w = n_heads * HEAD_DIM
    n_chunks = seq // t
    assert row0 % t == 0 and seq % t == 0
    r0 = row0 // t

    def rows(col):
        return pl.BlockSpec((t, w), lambda b, c: (r0 + b * n_chunks + c, col))

    args = [p, p, p, p, cos2, sin2, ret_norm_g.reshape(1, w).astype(F32), state0]
    in_specs = [rows(q_col), rows(k_col), rows(v_col), rows(g_col),
                pl.BlockSpec((t, HEAD_DIM), lambda b, c: (c, 0)),
                pl.BlockSpec((t, HEAD_DIM), lambda b, c: (c, 0)),
                pl.BlockSpec((1, w), lambda b, c: (0, 0)),
                pl.BlockSpec((None, n_heads, HEAD_DIM, HEAD_DIM), lambda b, c: (b, 0, 0, 0))]
    aliases = {}
    if into is not None:
        aliases = {len(args): 0}
        args.append(into)
        in_specs.append(pl.BlockSpec(memory_space=pl.ANY))
    return pl.pallas_call(
        functools.partial(_retention_kernel, t=t, n_chunks=n_chunks, log_g=_retention_log_decay(n_heads)),
        grid=(nb, n_chunks),
        in_specs=in_specs,
        out_specs=[rows(0),
                   pl.BlockSpec((None, n_heads, HEAD_DIM, HEAD_DIM), lambda b, c: (b, 0, 0, 0))],
        out_shape=[jax.ShapeDtypeStruct((p.shape[0], w), BF16),
                   jax.ShapeDtypeStruct((nb, n_heads, HEAD_DIM, HEAD_DIM), F32)],
        input_output_aliases=aliases,
        scratch_shapes=[pltpu.VMEM((n_heads, HEAD_DIM, HEAD_DIM), F32), pltpu.VMEM((n_heads, t, t), F32)],
        compiler_params=_params("parallel", "arbitrary"),
        name="retention",
    )(*args)


def _rope_tables(pos):
    half = HEAD_DIM // 2
    inv = ROPE_BASE ** (-jnp.arange(half, dtype=F32) / half)
    ang = pos.astype(F32)[:, None] * inv[None, :]
    cos, sin = jnp.cos(ang), jnp.sin(ang)
    return jnp.concatenate([cos, cos], axis=1), jnp.concatenate([-sin, sin], axis=1)


def _band_bias_table(rel_bias, tq):
    n_heads = rel_bias.shape[0]
    w = BAND_WINDOW + tq
    n = tq + w - 1
    dist = BAND_WINDOW + tq - 1 - np.arange(n)
    vec = rel_bias.astype(F32)[:, np.clip(dist, -MAX_REL, MAX_REL) + MAX_REL]
    vec = jnp.roll(vec, -(tq - 1), axis=1)
    bias = jnp.tile(vec, (1, tq))[:, :tq * (n - 1)].reshape(n_heads, tq, n - 1)[:, :, :w]
    t = np.arange(tq)[:, None]
    j = np.arange(w)[None, :]
    lo = (t // CHUNK) * CHUNK
    in_band = (j >= lo) & (j < lo + BAND_WINDOW + CHUNK)
    return jnp.where(jnp.asarray(in_band)[None], bias, NEG_INF)


def _softmax_pv(scores, values):
    m = scores[0].max(axis=1, keepdims=True)
    for s in scores[1:]:
        m = jnp.maximum(m, s.max(axis=1, keepdims=True))
    ps = [jnp.exp(s - m) for s in scores]
    denom = ps[0].sum(axis=1, keepdims=True)
    for p in ps[1:]:
        denom = denom + p.sum(axis=1, keepdims=True)
    o = _dot(ps[0].astype(BF16), values[0])
    for p, v in zip(ps[1:], values[1:]):
        o = o + _dot(p.astype(BF16), v)
    return o * (1.0 / denom)


BAND_HEADS_PER_STEP = 4


def _band_prompt_kernel(*refs, tq, n_kb, hb):
    q_ref = refs[0]
    k_refs, v_refs = refs[1:1 + n_kb], refs[1 + n_kb:1 + 2 * n_kb]
    b_ref, o_ref = refs[1 + 2 * n_kb], refs[2 + 2 * n_kb]
    i = pl.program_id(1)
    for h in range(hb):
        sl = slice(h * HEAD_DIM, (h + 1) * HEAD_DIM)
        q = (q_ref[:, sl] * SCALE).astype(BF16)
        scores, values = [], []
        for kb in range(n_kb):
            s = _dot_nt(q, k_refs[kb][:, sl].astype(BF16)) + b_ref[h, :, kb * tq:(kb + 1) * tq]
            scores.append(s + jnp.where(i - (n_kb - 1) + kb >= 0, 0.0, NEG_INF))
            values.append(v_refs[kb][:, sl].astype(BF16))
        o_ref[:, sl] = _softmax_pv(scores, values).astype(o_ref.dtype)


def band_prompt(p, rel_bias, s, n_heads, q_col, k_col, v_col):
    tq = _tile(s, 256)
    hb = BAND_HEADS_PER_STEP
    w = hb * HEAD_DIM
    assert BAND_WINDOW % tq == 0 and tq % CHUNK == 0 and n_heads % hb == 0
    assert q_col % hb == 0 and k_col % hb == 0 and v_col % hb == 0
    n_kb = BAND_WINDOW // tq + 1

    def kv_specs(col):
        return [pl.BlockSpec((tq, w), lambda h, i, kb=kb: (jnp.maximum(i - (n_kb - 1) + kb, 0), col // hb + h))
                for kb in range(n_kb)]

    return pl.pallas_call(
        functools.partial(_band_prompt_kernel, tq=tq, n_kb=n_kb, hb=hb),
        grid=(n_heads // hb, s // tq),
        in_specs=([pl.BlockSpec((tq, w), lambda h, i: (i, q_col // hb + h))] + kv_specs(k_col) + kv_specs(v_col)
                  + [pl.BlockSpec((hb, tq, BAND_WINDOW + tq), lambda h, i: (h, 0, 0))]),
        out_specs=pl.BlockSpec((tq, w), lambda h, i: (i, h)),
        out_shape=jax.ShapeDtypeStruct((p.shape[0], n_heads * HEAD_DIM), BF16),
        compiler_params=_params("parallel", "parallel"),
        name="band_prompt",
    )(*([p] * (1 + 2 * n_kb)), _band_bias_table(rel_bias, tq))


def _band_sample_kernel(q_ref, kn_ref, vn_ref, kc_ref, vc_ref, b_ref, _into_ref, o_ref, ko_ref, vo_ref, *,
                        band_past, t, n_heads):
    keep = (band_past - t) * n_heads
    for h in range(n_heads):
        sl = slice(h * HEAD_DIM, (h + 1) * HEAD_DIM)
        q = (q_ref[:, sl] * SCALE).astype(BF16)
        kn, vn = kn_ref[:, sl], vn_ref[:, sl]
        kc = kc_ref[pl.ds(h, band_past, stride=n_heads), :].astype(BF16)
        vc = vc_ref[pl.ds(h, band_past, stride=n_heads), :].astype(BF16)
        s_c = _dot_nt(q, kc) + b_ref[h, :, :band_past]
        s_n = _dot_nt(q, kn.astype(BF16)) + b_ref[h, :, band_past:]
        o_ref[:, sl] = _softmax_pv([s_c, s_n], [vc, vn.astype(BF16)]).astype(o_ref.dtype)
        ko_ref[pl.ds(keep + h, t, stride=n_heads), :] = kn
        vo_ref[pl.ds(keep + h, t, stride=n_heads), :] = vn
    ko_ref[:keep] = kc_ref[t * n_heads:]
    vo_ref[:keep] = vc_ref[t * n_heads:]


def band_sample(p, cache_k, cache_v, rel_bias, into, row0, nb, t, q_col, k_col, v_col):
    _, band_past, n_heads, _ = cache_k.shape
    w = n_heads * HEAD_DIM
    assert t == CHUNK and band_past == BAND_WINDOW and row0 % t == 0
    r0 = row0 // t
    cache_spec = pl.BlockSpec((None, band_past * n_heads, HEAD_DIM), lambda b: (b, 0, 0))
    cache_shape = jax.ShapeDtypeStruct((nb, band_past * n_heads, HEAD_DIM), F32)
    cache_k = cache_k.reshape(cache_shape.shape)
    cache_v = cache_v.reshape(cache_shape.shape)
    o, k_out, v_out = pl.pallas_call(
        functools.partial(_band_sample_kernel, band_past=band_past, t=t, n_heads=n_heads),
        grid=(nb,),
        in_specs=[pl.BlockSpec((t, w), lambda b: (r0 + b, q_col)),
                  pl.BlockSpec((t, w), lambda b: (r0 + b, k_col)),
                  pl.BlockSpec((t, w), lambda b: (r0 + b, v_col)),
                  cache_spec, cache_spec,
                  pl.BlockSpec((n_heads, t, band_past + t), lambda b: (0, 0, 0)),
                  pl.BlockSpec(memory_space=pl.ANY)],
        out_specs=[pl.BlockSpec((t, w), lambda b: (r0 + b, 0)), cache_spec, cache_spec],
        out_shape=[jax.ShapeDtypeStruct(into.shape, into.dtype), cache_shape, cache_shape],
        input_output_aliases={6: 0},
        compiler_params=_params("parallel"),
        name="band_sample",
    )(p, p, p, cache_k.astype(F32), cache_v.astype(F32), _band_bias_table(rel_bias, t), into)
    out_4d = (nb, band_past, n_heads, HEAD_DIM)
    return o, k_out.reshape(out_4d), v_out.reshape(out_4d)


def _cross_attn_kernel(q_ref, mk_ref, mv_ref, *rest, n_heads, n_mem):
    o_ref = rest[-1]
    for h in range(n_heads):
        sl = slice(h * HEAD_DIM, (h + 1) * HEAD_DIM)
        q = (q_ref[:, sl] * SCALE).astype(BF16)
        mk = mk_ref[pl.ds(h, n_mem, stride=n_heads), :].astype(BF16)
        mv = mv_ref[pl.ds(h, n_mem, stride=n_heads), :].astype(BF16)
        o_ref[:, sl] = _softmax_pv([_dot_nt(q, mk)], [mv]).astype(o_ref.dtype)


def cross_attn(q, mk, mv, into, row0, nb, seq):
    _, n_mem, n_heads, _ = mk.shape
    w = n_heads * HEAD_DIM
    tq = _tile(seq, 512)
    n_t = seq // tq
    assert row0 % tq == 0
    r0 = row0 // tq
    mem_spec = pl.BlockSpec((None, n_mem * n_heads, HEAD_DIM), lambda b, i: (b, 0, 0))
    args = [q, mk.reshape(nb, n_mem * n_heads, HEAD_DIM), mv.reshape(nb, n_mem * n_heads, HEAD_DIM)]
    in_specs = [pl.BlockSpec((tq, w), lambda b, i: (r0 + b * n_t + i, 0)), mem_spec, mem_spec]
    aliases = {}
    if into is not None:
        aliases = {len(args): 0}
        args.append(into)
        in_specs.append(pl.BlockSpec(memory_space=pl.ANY))
    return pl.pallas_call(
        functools.partial(_cross_attn_kernel, n_heads=n_heads, n_mem=n_mem),
        grid=(nb, n_t),
        in_specs=in_specs,
        out_specs=pl.BlockSpec((tq, w), lambda b, i: (r0 + b * n_t + i, 0)),
        out_shape=jax.ShapeDtypeStruct((q.shape[0], w), BF16),
        input_output_aliases=aliases,
        compiler_params=_params("parallel", "parallel"),
        name="cross_attn",
    )(*args)


def kernel(x_prompt, x_sample, cache_sb_k, cache_sb_v, state_ret, cache_band_k, cache_band_v, cache_mem_k, cache_mem_v, mem_prompt, w_in_ab, w_out_ab, ret_norm_g, w_qkv_band, w_out_band, rel_bias_band, norm_g, mem_norm_g, w_xq, w_xk, w_xv, w_xo, ffn_w_gate, ffn_w_up, ffn_w_down, moe_router, moe_w_gate, moe_w_up, moe_w_down, final_norm_g):
    bp, s, d = x_prompt.shape
    nb, t, _ = x_sample.shape
    assert bp == 1
    past = cache_sb_k.shape[1]
    band_past = cache_band_k.shape[1]
    h_sb = cache_sb_k.shape[2]
    h_ret = state_ret.shape[1]
    h_band = cache_band_k.shape[2]
    d_sb, d_ret = h_sb * HEAD_DIM, h_ret * HEAD_DIM
    n_mem = mem_prompt.shape[1]
    d_x = w_xq.shape[2]
    depth = w_xq.shape[0]
    d_ff = ffn_w_gate.shape[1]
    ms = nb * t
    assert d_sb == d_ret and 3 * d_sb + 4 * d_ret == w_in_ab.shape[1]

    x = jnp.concatenate([x_prompt.reshape(s, d), x_sample.reshape(ms, d)], axis=0)
    mem = mem_prompt.reshape(n_mem, d)
    bf = lambda a: a.astype(BF16)

    cos_p, sin_p = _rope_tables(jnp.arange(s, dtype=jnp.int32))
    cos_s, sin_s = _rope_tables(past + jnp.arange(t, dtype=jnp.int32))

    mem_k_list, mem_v_list = [], []
    for l in range(depth):
        mem_n = rmsnorm(mem, mem_norm_g[l], BF16)
        mk_p = matmul([(mem_n, 0, 0)], bf(w_xk[l]), d)
        mv_p = matmul([(mem_n, 0, 0)], bf(w_xv[l]), d)
        mem_k_list.append(mk_p.reshape(1, n_mem, H_X, HEAD_DIM))
        mem_v_list.append(mv_p.reshape(1, n_mem, H_X, HEAD_DIM))

        if l % 2 == 0:
            p = norm_linear(x, norm_g[l, 0], [w_in_ab], F32, tn_pref=1024)
            o_sb = sb_prompt(p, s, h_sb, 0, h_sb, 2 * h_sb)
            o_sb = sb_sample(p, cache_sb_k, cache_sb_v, o_sb, s, nb, t, 0, 1, 2)
            t_ret = _tile(s, 256, CHUNK)
            o_r, ret_state_prompt = retention(p, cos_p, sin_p, ret_norm_g, jnp.zeros((1,) + state_ret.shape[1:], F32),
                                              None, 0, 1, s, t_ret, h_ret, 3, 4, 5, 6)
            o_r, ret_state_sample = retention(p, cos_s, sin_s, ret_norm_g, state_ret.astype(F32),
                                              o_r, s, nb, t, t, h_ret, 3, 4, 5, 6)
            x = matmul([(o_sb, 0, 0), (o_r, 0, d_sb)], bf(w_out_ab), d_sb, residual=x)
            sb_k_prompt = p[:s, d_sb:2 * d_sb].reshape(1, s, h_sb, HEAD_DIM)
            sb_v_prompt = p[:s, 2 * d_sb:3 * d_sb].reshape(1, s, h_sb, HEAD_DIM)
            sb_k_sample = p[s:, d_sb:2 * d_sb].reshape(nb, t, h_sb, HEAD_DIM)
            sb_v_sample = p[s:, 2 * d_sb:3 * d_sb].reshape(nb, t, h_sb, HEAD_DIM)
        else:
            d_band = h_band * HEAD_DIM
            p = norm_linear(x, norm_g[l, 0], [w_qkv_band], F32, tn_pref=1024)
            o = band_prompt(p, rel_bias_band, s, h_band, 0, h_band, 2 * h_band)
            o, band_k_sample, band_v_sample = band_sample(p, cache_band_k, cache_band_v, rel_bias_band, o,
                                                          s, nb, t, 0, 1, 2)
            x = matmul([(o, 0, 0)], bf(w_out_band), d_band, residual=x)
            band_k_prompt = p[s - band_past:s, d_band:2 * d_band].reshape(1, band_past, h_band, HEAD_DIM)
            band_v_prompt = p[s - band_past:s, 2 * d_band:].reshape(1, band_past, h_band, HEAD_DIM)

        q = norm_linear(x, norm_g[l, 1], [w_xq[l]], F32, tn_pref=1024)
        o = cross_attn(q, mem_k_list[-1], mem_v_list[-1], None, 0, 1, s)
        o = cross_attn(q, cache_mem_k[l], cache_mem_v[l], o, s, nb, t)
        x = matmul([(o, 0, 0)], bf(w_xo[l]), d_x, residual=x)

        if l % 2 == 0:
            act = norm_linear(x, norm_g[l, 2], [ffn_w_gate, ffn_w_up], BF16, tn_pref=512)
            x = matmul([(act, 0, 0)], bf(ffn_w_down), d_ff, residual=x, tk_pref=d_ff)
        else:
            h = rmsnorm(x, norm_g[l, 2], BF16)
            gates, idx = moe_route(x, norm_g[l, 2], moe_router)
            x = moe_top2(x, h, idx, gates, bf(moe_w_gate), bf(moe_w_up), bf(moe_w_down))

    y_prompt = rmsnorm(x, final_norm_g, F32, 0, s).reshape(1, s, d)
    y_sample = rmsnorm(x, final_norm_g, F32, s, ms).reshape(nb, t, d)
    mem_k_prompt = jnp.stack(mem_k_list, axis=0)
    mem_v_prompt = jnp.stack(mem_v_list, axis=0)
    return (y_prompt, y_sample, sb_k_prompt, sb_v_prompt, sb_k_sample, sb_v_sample,
            ret_state_prompt, ret_state_sample, band_k_prompt, band_v_prompt,
            band_k_sample, band_v_sample, mem_k_prompt, mem_v_prompt)
```

```python
import functools

import numpy as np
import jax
import jax.numpy as jnp
from jax import lax
from jax.experimental import pallas as pl
from jax.experimental.pallas import tpu as pltpu

F32 = jnp.float32
BF16 = jnp.bfloat16

HEAD_DIM = 128
CHUNK = 64
N_BAND_CHUNKS = 8
BAND_WINDOW = N_BAND_CHUNKS * CHUNK
MAX_REL = 128
H_X = 4
TOP_K = 2
RMS_EPS = 1e-6
ROPE_BASE = 10000.0
NEG_INF = -1e30
SCALE = HEAD_DIM ** -0.5
LOG2E = 1.4426950408889634

VMEM_LIMIT_BYTES = 56 * 1024 * 1024


def _params(*sem):
    return pltpu.CompilerParams(dimension_semantics=sem, vmem_limit_bytes=VMEM_LIMIT_BYTES)


def _tile(n, pref, mult=8):
    t = min(n, pref)
    while t > mult and (n % t or t % mult):
        t -= mult
    assert n % t == 0, (n, pref)
    return t


def _dot(a, b):
    return jnp.dot(a, b, preferred_element_type=F32)


def _dot_nt(a, b):
    return lax.dot_general(a, b, (((1,), (1,)), ((), ())), preferred_element_type=F32)


def _dot_tn(a, b):
    return lax.dot_general(a, b, (((0,), (0,)), ((), ())), preferred_element_type=F32)


def _silu(a):
    return a * (1.0 / (1.0 + jnp.exp(-a)))


def _rows(parts):
    return parts[0] if len(parts) == 1 else jnp.concatenate(parts, axis=0)


def _rmsnorm_kernel(x_ref, g_ref, o_ref):
    x = x_ref[...]
    y = x * lax.rsqrt(jnp.mean(x * x, axis=-1, keepdims=True) + RMS_EPS) * g_ref[...]
    o_ref[...] = y.astype(o_ref.dtype)


def rmsnorm(x, g, out_dtype, row0=0, rows=None):
    d = x.shape[1]
    m = x.shape[0] - row0 if rows is None else rows
    tm = _tile(m, 512)
    assert row0 % tm == 0
    return pl.pallas_call(
        _rmsnorm_kernel,
        grid=(m // tm,),
        in_specs=[pl.BlockSpec((tm, d), lambda i: (row0 // tm + i, 0)),
                  pl.BlockSpec((1, d), lambda i: (0, 0))],
        out_specs=pl.BlockSpec((tm, d), lambda i: (i, 0)),
        out_shape=jax.ShapeDtypeStruct((m, d), out_dtype),
        compiler_params=_params("parallel"),
        name="rmsnorm",
    )(x, g.reshape(1, d).astype(F32))


def _mm_kernel(*refs, n_parts, has_res, nk):
    xs, ws = refs[:n_parts], refs[n_parts:2 * n_parts]
    pos = 2 * n_parts
    res_ref = refs[pos] if has_res else None
    pos += int(has_res)
    o_ref = refs[pos]
    part = _dot(xs[0][...], ws[0][...])
    for x_ref, w_ref in zip(xs[1:], ws[1:]):
        part = part + _dot(x_ref[...], w_ref[...])
    if nk == 1:
        if has_res:
            part = res_ref[...] + part
        o_ref[...] = part.astype(o_ref.dtype)
        return
    acc_ref = refs[pos + 1]
    k = pl.program_id(2)

    @pl.when(k == 0)
    def _():
        acc_ref[...] = part

    @pl.when(k > 0)
    def _():
        acc_ref[...] += part

    @pl.when(k == nk - 1)
    def _():
        out = acc_ref[...]
        if has_res:
            out = res_ref[...] + out
        o_ref[...] = out.astype(o_ref.dtype)


def matmul(parts, w, k_part, *, col_off=0, n_cols=None, out_dtype=F32, residual=None,
           tm_pref=1024, tn_pref=512, tk_pref=2048):
    m = parts[0][0].shape[0]
    n_cols = w.shape[1] - col_off if n_cols is None else n_cols
    tm = _tile(m, tm_pref)
    tn = _tile(n_cols, tn_pref, 128)
    tk = _tile(k_part, tk_pref, 128)
    nk = k_part // tk
    assert col_off % tn == 0
    in_specs, args = [], []
    for x, xo, _ in parts:
        assert xo % tk == 0
        in_specs.append(pl.BlockSpec((tm, tk), lambda i, j, k, xo=xo: (i, xo // tk + k)))
        args.append(x)
    for _, _, wo in parts:
        assert wo % tk == 0
        in_specs.append(pl.BlockSpec((tk, tn), lambda i, j, k, wo=wo: (wo // tk + k, col_off // tn + j)))
        args.append(w)
    if residual is not None:
        in_specs.append(pl.BlockSpec((tm, tn), lambda i, j, k: (i, j)))
        args.append(residual)
    return pl.pallas_call(
        functools.partial(_mm_kernel, n_parts=len(parts), has_res=residual is not None, nk=nk),
        grid=(m // tm, n_cols // tn, nk),
        in_specs=in_specs,
        out_specs=pl.BlockSpec((tm, tn), lambda i, j, k: (i, j)),
        out_shape=jax.ShapeDtypeStruct((m, n_cols), out_dtype),
        scratch_shapes=[pltpu.VMEM((tm, tn), F32)] if nk > 1 else [],
        compiler_params=_params("parallel", "parallel", "arbitrary"),
        name="matmul",
    )(*args)


def _norm_linear_kernel(x_ref, g_ref, *rest, n_w):
    w_refs, o_ref, wb_refs = rest[:n_w], rest[n_w], rest[n_w + 1:]

    @pl.when(pl.program_id(1) == 0)
    def _():
        for w_ref, wb_ref in zip(w_refs, wb_refs):
            wb_ref[...] = w_ref[...].astype(BF16)

    x = x_ref[...]
    h = (x * lax.rsqrt(jnp.mean(x * x, axis=-1, keepdims=True) + RMS_EPS) * g_ref[...]).astype(BF16)
    if n_w == 1:
        out = _dot(h, wb_refs[0][...])
    else:
        out = _silu(_dot(h, wb_refs[0][...])) * _dot(h, wb_refs[1][...])
    o_ref[...] = out.astype(o_ref.dtype)


def norm_linear(x, g, ws, out_dtype, *, tn_pref):
    m, d = x.shape
    n = ws[0].shape[1]
    tm = _tile(m, 1024)
    tn = _tile(n, tn_pref, 128)
    return pl.pallas_call(
        functools.partial(_norm_linear_kernel, n_w=len(ws)),
        grid=(n // tn, m // tm),
        in_specs=([pl.BlockSpec((tm, d), lambda j, i: (i, 0)), pl.BlockSpec((1, d), lambda j, i: (0, 0))]
                  + [pl.BlockSpec((d, tn), lambda j, i: (0, j))] * len(ws)),
        out_specs=pl.BlockSpec((tm, tn), lambda j, i: (i, j)),
        out_shape=jax.ShapeDtypeStruct((m, n), out_dtype),
        scratch_shapes=[pltpu.VMEM((d, tn), BF16)] * len(ws),
        compiler_params=_params("parallel", "arbitrary"),
        name="norm_linear",
    )(x, g.reshape(1, d).astype(F32), *[w.astype(F32) for w in ws])


def _router_kernel(x_ref, g_ref, r_ref, h_ref, o_ref, *, n_experts):
    x = x_ref[...]
    hn = x * lax.rsqrt(jnp.mean(x * x, axis=-1, keepdims=True) + RMS_EPS) * g_ref[...]
    h_ref[...] = hn.astype(h_ref.dtype)
    logits = jnp.dot(hn, r_ref[...], preferred_element_type=F32, precision=lax.Precision.HIGHEST)
    lane = lax.broadcasted_iota(jnp.int32, logits.shape, 1)
    n_lanes = logits.shape[1]
    lg = jnp.where(lane < n_experts, logits, -jnp.inf)
    m1 = jnp.max(lg, axis=1, keepdims=True)
    i1 = jnp.min(jnp.where(lg == m1, lane, n_lanes), axis=1, keepdims=True)
    lg2 = jnp.where(lane == i1, -jnp.inf, lg)
    m2 = jnp.max(lg2, axis=1, keepdims=True)
    i2 = jnp.min(jnp.where(lg2 == m2, lane, n_lanes), axis=1, keepdims=True)
    e2 = jnp.exp(m2 - m1)
    inv = 1.0 / (1.0 + e2)
    o_ref[...] = (jnp.where(lane == 0, inv, 0.0) + jnp.where(lane == 1, e2 * inv, 0.0)
                  + jnp.where(lane == 2, i1.astype(F32), 0.0) + jnp.where(lane == 3, i2.astype(F32), 0.0))


def moe_route(x, g, router):
    m, d = x.shape
    n_e = router.shape[1]
    tm = _tile(m, 512)
    r_pad = jnp.zeros((d, 128), F32).at[:, :n_e].set(router.astype(F32))
    h, out = pl.pallas_call(
        functools.partial(_router_kernel, n_experts=n_e),
        grid=(m // tm,),
        in_specs=[pl.BlockSpec((tm, d), lambda i: (i, 0)),
                  pl.BlockSpec((1, d), lambda i: (0, 0)),
                  pl.BlockSpec((d, 128), lambda i: (0, 0))],
        out_specs=[pl.BlockSpec((tm, d), lambda i: (i, 0)), pl.BlockSpec((tm, 128), lambda i: (i, 0))],
        out_shape=[jax.ShapeDtypeStruct((m, d), BF16), jax.ShapeDtypeStruct((m, 128), F32)],
        compiler_params=_params("parallel"),
        name="moe_router",
    )(x, g.reshape(1, d).astype(F32), r_pad)
    return h, out[:, :TOP_K], out[:, TOP_K:2 * TOP_K].astype(jnp.int32)


MOE_ROW_TILE = 256
MOE_GATHER_CHUNK = 512
MOE_COMBINE_TILE = 512


def _i32(a):
    return a.astype(jnp.int32)


def _count_le(ends, v):
    return jnp.sum(_i32(ends[None, :] <= v[:, None]), axis=1, dtype=jnp.int32)


def _moe_plan(idx, gates, n_e, tr, tc, tt):
    m = idx.shape[0]
    assert (TOP_K * m) % tr == 0 and m % tc == 0 and m % tt == 0
    n_tiles = TOP_K * m // tr + n_e
    n_rows = n_tiles * tr
    routed = jnp.zeros((m, n_e), jnp.int32)
    for k in range(TOP_K):
        routed = routed + _i32(idx[:, k:k + 1] == jnp.arange(n_e, dtype=jnp.int32)[None, :])
    csum = jnp.cumsum(routed, axis=0, dtype=jnp.int32)
    rank = csum - routed
    tiles_e = (csum[-1] + tr - 1) // tr
    tile_end = jnp.cumsum(tiles_e, dtype=jnp.int32)
    row_start = (tile_end - tiles_e) * tr
    dest = row_start[idx] + jnp.take_along_axis(rank, idx, axis=1)
    gate_rows = jnp.zeros((n_rows,), F32).at[dest.reshape(-1)].set(gates.reshape(-1), unique_indices=True)
    tile_ids = jnp.arange(n_tiles, dtype=jnp.int32)
    tile_valid = tile_ids < tile_end[-1]
    tile_expert = jnp.minimum(_count_le(tile_end, tile_ids), n_e - 1)

    def visits(first_chunk, n_visits, n_max, owner_of):
        end = jnp.cumsum(n_visits, dtype=jnp.int32)
        start = end - n_visits
        v = jnp.arange(n_max, dtype=jnp.int32)
        live = v < end[-1]
        slot = jnp.minimum(_count_le(end, v), n_visits.shape[0] - 1)
        owner = owner_of(slot)
        chunk = first_chunk[slot] + v - start[slot]
        last = jnp.maximum(end[-1] - 1, 0)
        owner = jnp.where(live, owner, owner[last])
        chunk = jnp.where(live, chunk, chunk[last])
        first = live & ((v == 0) | (owner != jnp.roll(owner, 1)))
        return _i32(owner), _i32(chunk), _i32(first) + 2 * _i32(live)

    rank0 = (tile_ids - (tile_end - tiles_e)[tile_expert]) * tr
    rank1 = jnp.minimum(rank0 + tr, csum[-1][tile_expert]) - 1
    csum_e = csum.T[tile_expert]
    first_tok = jnp.sum(_i32(csum_e <= rank0[:, None]), axis=1, dtype=jnp.int32)
    last_tok = jnp.sum(_i32(csum_e <= rank1[:, None]), axis=1, dtype=jnp.int32)
    c_first = jnp.where(tile_valid, first_tok, 0) // tc
    c_last = jnp.where(tile_valid, last_tok, 0) // tc
    g_plan = visits(c_first, c_last - c_first + 1, n_tiles + n_e * (m // tc - 1), lambda s: s)
    n_tt = m // tt
    before = jnp.concatenate([jnp.zeros((1, n_e), jnp.int32), csum[tt - 1::tt]], axis=0)
    lo = row_start[None, :] + before[:-1]
    hi = row_start[None, :] + before[1:]
    n_vis = jnp.where(hi > lo, (hi - 1) // tr - lo // tr + 1, 0)
    c_plan = visits((lo // tr).reshape(-1), n_vis.reshape(-1), n_tt * n_e + n_tiles - 1, lambda s: s // n_e)
    return dict(n_tiles=n_tiles, dest=dest, gate_rows=gate_rows.reshape(n_rows, 1),
                tile_expert=tile_expert, tile_valid=_i32(tile_valid), gather=g_plan, combine=c_plan)


def _moe_select(dest_ref, row0, n_rows):
    row = row0 + lax.broadcasted_iota(jnp.int32, (dest_ref.shape[0], n_rows), 1)
    hit = row == dest_ref[:, 0:1]
    for k in range(1, TOP_K):
        hit = hit | (row == dest_ref[:, k:k + 1])
    return jnp.where(hit, 1.0, 0.0).astype(BF16)


def _moe_gather_kernel(vt_ref, vc_ref, vf_ref, dest_ref, h_ref, o_ref):
    v = pl.program_id(0)
    flags = vf_ref[v]
    tr = o_ref.shape[0]

    def gathered():
        sel = _moe_select(dest_ref, vt_ref[v] * tr, tr)
        return _dot_tn(sel, h_ref[...]).astype(o_ref.dtype)

    @pl.when(flags == 3)
    def _():
        o_ref[...] = gathered()

    @pl.when(flags == 2)
    def _():
        o_ref[...] += gathered()


def _moe_up_kernel(te_ref, tv_ref, x_ref, wg_ref, wu_ref, o_ref):
    j = pl.program_id(1)

    @pl.when(tv_ref[j] != 0)
    def _():
        x = x_ref[...]
        o_ref[...] = (_silu(_dot(x, wg_ref[...])) * _dot(x, wu_ref[...])).astype(o_ref.dtype)

    @pl.when(tv_ref[j] == 0)
    def _():
        o_ref[...] = jnp.zeros_like(o_ref)


def _moe_down_kernel(te_ref, tv_ref, a_ref, wd_ref, g_ref, o_ref):
    j = pl.program_id(0)

    @pl.when(tv_ref[j] != 0)
    def _():
        o_ref[...] = (_dot(a_ref[...], wd_ref[...]) * g_ref[...]).astype(o_ref.dtype)

    @pl.when(tv_ref[j] == 0)
    def _():
        o_ref[...] = jnp.zeros_like(o_ref)


def _moe_combine_kernel(vi_ref, vc_ref, vf_ref, dest_ref, y_ref, x_ref, o_ref):
    v = pl.program_id(0)
    flags = vf_ref[v]
    tr = y_ref.shape[0]

    @pl.when(flags % 2 == 1)
    def _():
        o_ref[...] = x_ref[...]

    @pl.when(flags >= 2)
    def _():
        sel = _moe_select(dest_ref, vc_ref[v] * tr, tr)
        o_ref[...] += _dot(sel, y_ref[...])


def moe_top2(x, h, idx, gates, wg, wu, wd):
    m, d = h.shape
    n_e, _, f = wg.shape
    tr, tc, tt = MOE_ROW_TILE, _tile(m, MOE_GATHER_CHUNK, 16), _tile(m, MOE_COMBINE_TILE)
    plan =_moe_plan(idx, gates, n_e, tr, tc, tt)
    n_tiles = plan["n_tiles"]
    n_rows = n_tiles * tr

    vt, vc, vf = plan["gather"]
    xs = pl.pallas_call(
        _moe_gather_kernel,
        grid_spec=pltpu.PrefetchScalarGridSpec(
            num_scalar_prefetch=3, grid=(vt.shape[0],),
            in_specs=[pl.BlockSpec((tc, TOP_K), lambda v, vt, vc, vf: (vc[v], 0)),
                      pl.BlockSpec((tc, d), lambda v, vt, vc, vf: (vc[v], 0))],
            out_specs=pl.BlockSpec((tr, d), lambda v, vt, vc, vf: (vt[v], 0))),
        out_shape=jax.ShapeDtypeStruct((n_rows, d), BF16),
        compiler_params=_params("arbitrary"),
        name="moe_gather",
    )(vt, vc, vf, plan["dest"], h)

    tf = _tile(f, 1408, 128)
    act = pl.pallas_call(
        _moe_up_kernel,
        grid_spec=pltpu.PrefetchScalarGridSpec(
            num_scalar_prefetch=2, grid=(f // tf, n_tiles),
            in_specs=[pl.BlockSpec((tr, d), lambda c, j, te, tv: (j, 0)),
                      pl.BlockSpec((None, d, tf), lambda c, j, te, tv: (te[j], 0, c)),
                      pl.BlockSpec((None, d, tf), lambda c, j, te, tv: (te[j], 0, c))],
            out_specs=pl.BlockSpec((tr, tf), lambda c, j, te, tv: (j, c))),
        out_shape=jax.ShapeDtypeStruct((n_rows, f), BF16),
        compiler_params=_params("parallel", "parallel"),
        name="moe_up",
    )(plan["tile_expert"], plan["tile_valid"], xs, wg, wu)

    y = pl.pallas_call(
        _moe_down_kernel,
        grid_spec=pltpu.PrefetchScalarGridSpec(
            num_scalar_prefetch=2, grid=(n_tiles,),
            in_specs=[pl.BlockSpec((tr, f), lambda j, te, tv: (j, 0)),
                      pl.BlockSpec((None, f, d), lambda j, te, tv: (te[j], 0, 0)),
                      pl.BlockSpec((tr, 1), lambda j, te, tv: (j, 0))],
            out_specs=pl.BlockSpec((tr, d), lambda j, te, tv: (j, 0))),
        out_shape=jax.ShapeDtypeStruct((n_rows, d), BF16),
        compiler_params=_params("parallel"),
        name="moe_down",
    )(plan["tile_expert"], plan["tile_valid"], act, wd, plan["gate_rows"])

    vi, vc, vf = plan["combine"]
    return pl.pallas_call(
        _moe_combine_kernel,
        grid_spec=pltpu.PrefetchScalarGridSpec(
            num_scalar_prefetch=3, grid=(vi.shape[0],),
            in_specs=[pl.BlockSpec((tt, TOP_K), lambda v, vi, vc, vf: (vi[v], 0)),
                      pl.BlockSpec((tr, d), lambda v, vi, vc, vf: (vc[v], 0)),
                      pl.BlockSpec((tt, d), lambda v, vi, vc, vf: (vi[v], 0))],
            out_specs=pl.BlockSpec((tt, d), lambda v, vi, vc, vf: (vi[v], 0))),
        out_shape=jax.ShapeDtypeStruct((m, d), F32),
        compiler_params=_params("arbitrary"),
        name="moe_combine",
    )(vi, vc, vf, plan["dest"], y, x)


def _cumsum_matrix(tk):
    r = np.arange(2 * tk)[:, None]
    c = np.arange(2 * tk)[None, :]
    return jnp.asarray((r > c).astype(np.float32), dtype=BF16)


def _sb_sweep_step(qs, k2s, v2s, before, u, acc, carry):
    tq = qs[0].shape[0]
    tk = k2s[0].shape[0] // 2
    z = _rows([_dot_nt(q, k2) for q, k2 in zip(qs, k2s)])
    neg_l = jnp.maximum(z, 0.0) + jnp.log2(1.0 + jnp.exp2(-jnp.abs(z)))
    log_b = z - neg_l
    if before is not None:
        neg_l = jnp.where(before, neg_l, 0.0)
    neg_lb = neg_l.astype(BF16)
    after = _dot(neg_lb, u)
    w = jnp.exp2(log_b - (after + jnp.concatenate([carry, carry], axis=1)))
    if before is not None:
        w = jnp.where(before, w, 0.0)
    w = w.astype(BF16)
    acc = acc + _rows([_dot(w[n * tq:(n + 1) * tq], v2) for n, v2 in enumerate(v2s)])
    total = after[:, :1] + neg_lb[:, :1].astype(F32)
    return acc, carry + total


def _sb_prompt_kernel(q_ref, k_ref, v_ref, u_ref, o_ref, kb_ref, vb_ref, *, tq, tk):
    i = pl.program_id(1)

    @pl.when(i == 0)
    def _():
        kb_ref[...] = k_ref[...].astype(BF16)
        vb_ref[...] = v_ref[...].astype(BF16)

    q = (q_ref[...] * (SCALE * LOG2E)).astype(BF16)
    u = u_ref[...]
    steps_per_tile = tq // (2 * tk)
    state = (jnp.zeros((tq, HEAD_DIM), F32), jnp.zeros((tq, tk), F32))

    def step(off, r0, before, state):
        new = _sb_sweep_step([q[r0:]], [kb_ref[pl.ds(off, 2 * tk), :]], [vb_ref[pl.ds(off, 2 * tk), :]],
                             before, u, *[a[r0:] for a in state])
        return tuple(jnp.concatenate([a[:r0], b], axis=0) if r0 else b for a, b in zip(state, new))

    for r0 in reversed(range(0, tq, 2 * tk)):
        before = (lax.broadcasted_iota(jnp.int32, (tq - r0, 2 * tk), 1)
                  < lax.broadcasted_iota(jnp.int32, (tq - r0, 2 * tk), 0))
        state = step(pl.multiple_of(i * tq + r0, 2 * tk), r0, before, state)
    n_steps = i * steps_per_tile
    unroll = max(u for u in (1, 2, 4) if steps_per_tile % u == 0)

    def body(s, st):
        for k in range(unroll):
            st = step(pl.multiple_of((n_steps - 1 - unroll * s - k) * 2 * tk, 2 * tk), 0, None, st)
        return st

    acc, _ = lax.fori_loop(0, n_steps // unroll, body, state)
    o_ref[...] = acc.astype(o_ref.dtype)


def sb_prompt(p, s, n_heads, q_col, k_col, v_col):
    tk = HEAD_DIM
    tq = _tile(s, 1024, 2 * tk)
    return pl.pallas_call(
        functools.partial(_sb_prompt_kernel, tq=tq, tk=tk),
        grid=(n_heads, s // tq),
        in_specs=[pl.BlockSpec((tq, HEAD_DIM), lambda h, i: (i, q_col + h)),
                  pl.BlockSpec((s, HEAD_DIM), lambda h, i: (0, k_col + h)),
                  pl.BlockSpec((s, HEAD_DIM), lambda h, i: (0, v_col + h)),
                  pl.BlockSpec((2 * tk, 2 * tk), lambda h, i: (0, 0))],
        out_specs=pl.BlockSpec((tq, HEAD_DIM), lambda h, i: (i, h)),
        out_shape=jax.ShapeDtypeStruct((p.shape[0], n_heads * HEAD_DIM), BF16),
        scratch_shapes=[pltpu.VMEM((s, HEAD_DIM), BF16), pltpu.VMEM((s, HEAD_DIM), BF16)],
        compiler_params=_params("parallel", "arbitrary"),
        name="sb_prompt",
    )(p, p, p, _cumsum_matrix(tk))


def _sb_sample_kernel(q_ref, kn_ref, vn_ref, kc_hbm, vc_hbm, u_ref, _into_ref, o_ref,
                      acc_ref, carry_ref, kf_ref, vf_ref, sem, *, t, tk, ck, n_heads):
    b, c = pl.program_id(0), pl.program_id(1)
    n_c = pl.num_programs(1)
    step = b * n_c + c
    slot = step % 2

    def chunk_copies(step, slot):
        first = (n_c - 1 - step % n_c) * ck
        copies = []
        for h in range(n_heads):
            for hbm, buf in ((kc_hbm, kf_ref), (vc_hbm, vf_ref)):
                copies.append(pltpu.make_async_copy(hbm.at[step // n_c, pl.ds(first, ck), h, :],
                                                    buf.at[slot, h], sem.at[slot]))
        return copies

    @pl.when(step == 0)
    def _():
        for cp in chunk_copies(step, slot):
            cp.start()

    @pl.when(step + 1 < pl.num_programs(0) * n_c)
    def _():
        for cp in chunk_copies(step + 1, 1 - slot):
            cp.start()

    heads = [slice(h * HEAD_DIM, (h + 1) * HEAD_DIM) for h in range(n_heads)]
    qs = [(q_ref[:, sl] * (SCALE * LOG2E)).astype(BF16) for sl in heads]
    u = u_ref[...]

    @pl.when(c == 0)
    def _():
        reps = 2 * tk // t
        k2s = [jnp.concatenate([kn_ref[:, sl].astype(BF16)] * reps, axis=0) for sl in heads]
        v2s = [jnp.concatenate([vn_ref[:, sl].astype(BF16)] * reps, axis=0) for sl in heads]
        before = (lax.broadcasted_iota(jnp.int32, (t, 2 * tk), 1) < lax.broadcasted_iota(jnp.int32, (t, 2 * tk), 0))
        acc, carry = _sb_sweep_step(qs, k2s, v2s, _rows([before] * n_heads), u,
                                    jnp.zeros(acc_ref.shape, F32), jnp.zeros(carry_ref.shape, F32))
        acc_ref[...] = acc
        carry_ref[...] = carry

    for cp in chunk_copies(step, slot):
        cp.wait()
    n_steps = ck // (2 * tk)

    def body(s, state):
        off = pl.multiple_of((n_steps - 1 - s) * 2 * tk, 2 * tk)
        k2s = [kf_ref[slot, h, pl.ds(off, 2 * tk), :].astype(BF16) for h in range(n_heads)]
        v2s = [vf_ref[slot, h, pl.ds(off, 2 * tk), :].astype(BF16) for h in range(n_heads)]
        return _sb_sweep_step(qs, k2s, v2s, None, u, *state)

    acc, carry = lax.fori_loop(0, n_steps, body, (acc_ref[...], carry_ref[...]), unroll=2)
    acc_ref[...] = acc
    carry_ref[...] = carry

    @pl.when(c == pl.num_programs(1) - 1)
    def _():
        for h, sl in enumerate(heads):
            o_ref[:, sl] = acc_ref[h * t:(h + 1) * t, :].astype(o_ref.dtype)


def sb_sample(p, cache_k, cache_v, into, row0, nb, t, q_col, k_col, v_col):
    _, past, n_heads, _ = cache_k.shape
    tk = HEAD_DIM
    w = n_heads * HEAD_DIM
    ck = _tile(past, 1024, 2 * tk)
    n_c = past // ck
    assert (2 * tk) % t == 0 and row0 % t == 0
    r0 = row0 // t
    return pl.pallas_call(
        functools.partial(_sb_sample_kernel, t=t, tk=tk, ck=ck, n_heads=n_heads),
        grid=(nb, n_c),
        in_specs=[pl.BlockSpec((t, w), lambda b, c: (r0 + b, q_col)),
                  pl.BlockSpec((t, w), lambda b, c: (r0 + b, k_col)),
                  pl.BlockSpec((t, w), lambda b, c: (r0 + b, v_col)),
                  pl.BlockSpec(memory_space=pl.ANY), pl.BlockSpec(memory_space=pl.ANY),
                  pl.BlockSpec((2 * tk, 2 * tk), lambda b, c: (0, 0)),
                  pl.BlockSpec(memory_space=pl.ANY)],
        out_specs=pl.BlockSpec((t, w), lambda b, c: (r0 + b, 0)),
        out_shape=jax.ShapeDtypeStruct(into.shape, into.dtype),
        input_output_aliases={6: 0},
        scratch_shapes=[pltpu.VMEM((n_heads * t, HEAD_DIM), F32), pltpu.VMEM((n_heads * t, tk), F32),
                        pltpu.VMEM((2, n_heads, ck, HEAD_DIM), F32), pltpu.VMEM((2, n_heads, ck, HEAD_DIM), F32),
                        pltpu.SemaphoreType.DMA((2,))],
        compiler_params=_params("arbitrary", "arbitrary"),
        name="sb_sample",
    )(p, p, p, cache_k.astype(F32), cache_v.astype(F32), _cumsum_matrix(tk), into)


def _retention_log_decay(n_heads):
    return [float(np.log1p(-np.float32(2.0 ** (-5.0 - h)))) for h in range(n_heads)]


def _retention_kernel(q_ref, k_ref, v_ref, g_ref, cos_ref, sin_ref, gn_ref, s0_ref, *rest, t, n_chunks, log_g):
    o_ref, sout_ref, state_ref, dec_ref = rest[-4:]
    c = pl.program_id(1)

    @pl.when(c == 0)
    def _():
        state_ref[...] = s0_ref[...]
        rel = (lax.broadcasted_iota(jnp.int32, (t, t), 0) - lax.broadcasted_iota(jnp.int32, (t, t), 1)).astype(F32)
        for h, lg in enumerate(log_g):
            dec_ref[h] = jnp.where(rel >= 0, jnp.exp(lg * jnp.maximum(rel, 0.0)), 0.0)

    cos, sin = cos_ref[...], sin_ref[...]
    idx = lax.broadcasted_iota(jnp.int32, (t, HEAD_DIM), 0).astype(F32)
    for h, lg in enumerate(log_g):
        sl = slice(h * HEAD_DIM, (h + 1) * HEAD_DIM)
        qh, kh = q_ref[:, sl], k_ref[:, sl]
        qh = qh * cos + pltpu.roll(qh, HEAD_DIM // 2, 1) * sin
        kh = (kh * cos + pltpu.roll(kh, HEAD_DIM // 2, 1) * sin) * SCALE
        qb, vb = qh.astype(BF16), v_ref[:, sl].astype(BF16)
        scores = _dot_nt(qb, kh.astype(BF16)) * dec_ref[h]
        state = state_ref[h]
        o = _dot(scores.astype(BF16), vb) + _dot(qb, state.astype(BF16)) * jnp.exp((idx + 1.0) * lg)
        k_dec = (kh * jnp.exp((t - 1.0 - idx) * lg)).astype(BF16)
        state_ref[h] = float(np.exp(np.float32(t * lg))) * state + _dot_tn(k_dec, vb)
        o = o * lax.rsqrt(jnp.mean(o * o, axis=-1, keepdims=True) + RMS_EPS) * gn_ref[:, sl]
        o_ref[:, sl] = (o * _silu(g_ref[:, sl])).astype(o_ref.dtype)

    @pl.when(c == n_chunks - 1)
    def _():
        sout_ref[...] = state_ref[...]


def retention(p, cos2, sin2, ret_norm_g, state0, into, row0, nb, seq, t, n_heads, q_col, k_col, v_col, g_col):
    w = n_heads * HEAD_DIM
    n_chunks = seq // t
    assert row0 % t == 0 and seq % t == 0
    r0 = row0 // t

    def rows(col):
        return pl.BlockSpec((t, w), lambda b, c: (r0 + b * n_chunks + c, col))

    args = [p, p, p, p, cos2, sin2, ret_norm_g.reshape(1, w).astype(F32), state0]
    in_specs = [rows(q_col), rows(k_col), rows(v_col), rows(g_col),
                pl.BlockSpec((t, HEAD_DIM), lambda b, c: (c, 0)),
                pl.BlockSpec((t, HEAD_DIM), lambda b, c: (c, 0)),
                pl.BlockSpec((1, w), lambda b, c: (0, 0)),
                pl.BlockSpec((None, n_heads, HEAD_DIM, HEAD_DIM), lambda b, c: (b, 0, 0, 0))]
    aliases = {}
    if into is not None:
        aliases = {len(args): 0}
        args.append(into)
        in_specs.append(pl.BlockSpec(memory_space=pl.ANY))
    return pl.pallas_call(
        functools.partial(_retention_kernel, t=t, n_chunks=n_chunks, log_g=_retention_log_decay(n_heads)),
        grid=(nb, n_chunks),
        in_specs=in_specs,
        out_specs=[rows(0),
                   pl.BlockSpec((None, n_heads, HEAD_DIM, HEAD_DIM), lambda b, c: (b, 0, 0, 0))],
        out_shape=[jax.ShapeDtypeStruct((p.shape[0], w), BF16),
                   jax.ShapeDtypeStruct((nb, n_heads, HEAD_DIM, HEAD_DIM), F32)],
        input_output_aliases=aliases,
        scratch_shapes=[pltpu.VMEM((n_heads, HEAD_DIM, HEAD_DIM), F32), pltpu.VMEM((n_heads, t, t), F32)],
        compiler_params=_params("parallel", "arbitrary"),
        name="retention",
    )(*args)


def _rope_tables(pos):
    half = HEAD_DIM // 2
    inv = ROPE_BASE ** (-jnp.arange(half, dtype=F32) / half)
    ang = pos.astype(F32)[:, None] * inv[None, :]
    cos, sin = jnp.cos(ang), jnp.sin(ang)
    return jnp.concatenate([cos, cos], axis=1), jnp.concatenate([-sin, sin], axis=1)


def _band_bias_table(rel_bias, tq):
    n_heads = rel_bias.shape[0]
    w = BAND_WINDOW + tq
    n = tq + w - 1
    dist = BAND_WINDOW + tq - 1 - np.arange(n)
    vec = rel_bias.astype(F32)[:, np.clip(dist, -MAX_REL, MAX_REL) + MAX_REL]
    vec = jnp.roll(vec, -(tq - 1), axis=1)
    bias = jnp.tile(vec, (1, tq))[:, :tq * (n - 1)].reshape(n_heads, tq, n - 1)[:, :, :w]
    t = np.arange(tq)[:, None]
    j = np.arange(w)[None, :]
    lo = (t // CHUNK) * CHUNK
    in_band = (j >= lo) & (j < lo + BAND_WINDOW + CHUNK)
    return jnp.where(jnp.asarray(in_band)[None], bias, NEG_INF)


def _softmax_pv(scores, values):
    m = scores[0].max(axis=1, keepdims=True)
    for s in scores[1:]:
        m = jnp.maximum(m, s.max(axis=1, keepdims=True))
    ps = [jnp.exp(s - m) for s in scores]
    denom = ps[0].sum(axis=1, keepdims=True)
    for p in ps[1:]:
        denom = denom + p.sum(axis=1, keepdims=True)
    o = _dot(ps[0].astype(BF16), values[0])
    for p, v in zip(ps[1:], values[1:]):
        o = o + _dot(p.astype(BF16), v)
    return o * (1.0 / denom)


BAND_HEADS_PER_STEP = 8


def _band_prompt_kernel(*refs, tq, n_kb, hb):
    q_ref = refs[0]
    k_refs, v_refs = refs[1:1 + n_kb], refs[1 + n_kb:1 + 2 * n_kb]
    b_ref, o_ref = refs[1 + 2 * n_kb], refs[2 + 2 * n_kb]
    i = pl.program_id(1)
    for h in range(hb):
        sl = slice(h * HEAD_DIM, (h + 1) * HEAD_DIM)
        q = (q_ref[:, sl] * SCALE).astype(BF16)
        scores, values = [], []
        for kb in range(n_kb):
            s = _dot_nt(q, k_refs[kb][:, sl].astype(BF16)) + b_ref[h, :, kb * tq:(kb + 1) * tq]
            scores.append(s + jnp.where(i - (n_kb - 1) + kb >= 0, 0.0, NEG_INF))
            values.append(v_refs[kb][:, sl].astype(BF16))
        o_ref[:, sl] = _softmax_pv(scores, values).astype(o_ref.dtype)


def band_prompt(p, rel_bias, s, n_heads, q_col, k_col, v_col):
    tq = _tile(s, 256)
    hb = BAND_HEADS_PER_STEP
    w = hb * HEAD_DIM
    assert BAND_WINDOW % tq == 0 and tq % CHUNK == 0 and n_heads % hb == 0
    assert q_col % hb == 0 and k_col % hb == 0 and v_col % hb == 0
    n_kb = BAND_WINDOW // tq + 1

    def kv_specs(col):
        return [pl.BlockSpec((tq, w), lambda h, i, kb=kb: (jnp.maximum(i - (n_kb - 1) + kb, 0), col // hb + h))
                for kb in range(n_kb)]

    return pl.pallas_call(
        functools.partial(_band_prompt_kernel, tq=tq, n_kb=n_kb, hb=hb),
        grid=(n_heads // hb, s // tq),
        in_specs=([pl.BlockSpec((tq, w), lambda h, i: (i, q_col // hb + h))] + kv_specs(k_col) + kv_specs(v_col)
                  + [pl.BlockSpec((hb, tq, BAND_WINDOW + tq), lambda h, i: (h, 0, 0))]),
        out_specs=pl.BlockSpec((tq, w), lambda h, i: (i, h)),
        out_shape=jax.ShapeDtypeStruct((p.shape[0], n_heads * HEAD_DIM), BF16),
        compiler_params=_params("parallel", "parallel"),
        name="band_prompt",
    )(*([p] * (1 + 2 * n_kb)), _band_bias_table(rel_bias, tq))


def _band_sample_kernel(q_ref, kn_ref, vn_ref, kc_ref, vc_ref, b_ref, _into_ref, o_ref, ko_ref, vo_ref, *,
                        band_past, t, n_heads):
    keep = (band_past - t) * n_heads
    for h in range(n_heads):
        sl = slice(h * HEAD_DIM, (h + 1) * HEAD_DIM)
        q = (q_ref[:, sl] * SCALE).astype(BF16)
        kn, vn = kn_ref[:, sl], vn_ref[:, sl]
        kc = kc_ref[pl.ds(h, band_past, stride=n_heads), :].astype(BF16)
        vc = vc_ref[pl.ds(h, band_past, stride=n_heads), :].astype(BF16)
        s_c = _dot_nt(q, kc) + b_ref[h, :, :band_past]
        s_n = _dot_nt(q, kn.astype(BF16)) + b_ref[h, :, band_past:]
        o_ref[:, sl] = _softmax_pv([s_c, s_n], [vc, vn.astype(BF16)]).astype(o_ref.dtype)
        ko_ref[pl.ds(keep + h, t, stride=n_heads), :] = kn
        vo_ref[pl.ds(keep + h, t, stride=n_heads), :] = vn
    ko_ref[:keep] = kc_ref[t * n_heads:]
    vo_ref[:keep] = vc_ref[t * n_heads:]


def band_sample(p, cache_k, cache_v, rel_bias, into, row0, nb, t, q_col, k_col, v_col):
    _, band_past, n_heads, _ = cache_k.shape
    w = n_heads * HEAD_DIM
    assert t == CHUNK and band_past == BAND_WINDOW and row0 % t == 0
    r0 = row0 // t
    cache_spec = pl.BlockSpec((None, band_past * n_heads, HEAD_DIM), lambda b: (b, 0, 0))
    cache_shape = jax.ShapeDtypeStruct((nb, band_past * n_heads, HEAD_DIM), F32)
    cache_k = cache_k.reshape(cache_shape.shape)
    cache_v = cache_v.reshape(cache_shape.shape)
    o, k_out, v_out = pl.pallas_call(
        functools.partial(_band_sample_kernel, band_past=band_past, t=t, n_heads=n_heads),
        grid=(nb,),
        in_specs=[pl.BlockSpec((t, w), lambda b: (r0 + b, q_col)),
                  pl.BlockSpec((t, w), lambda b: (r0 + b, k_col)),
                  pl.BlockSpec((t, w), lambda b: (r0 + b, v_col)),
                  cache_spec, cache_spec,
                  pl.BlockSpec((n_heads, t, band_past + t), lambda b: (0, 0, 0)),
                  pl.BlockSpec(memory_space=pl.ANY)],
        out_specs=[pl.BlockSpec((t, w), lambda b: (r0 + b, 0)), cache_spec, cache_spec],
        out_shape=[jax.ShapeDtypeStruct(into.shape, into.dtype), cache_shape, cache_shape],
        input_output_aliases={6: 0},
        compiler_params=_params("parallel"),
        name="band_sample",
    )(p, p, p, cache_k.astype(F32), cache_v.astype(F32), _band_bias_table(rel_bias, t), into)
    out_4d = (nb, band_past, n_heads, HEAD_DIM)
    return o, k_out.reshape(out_4d), v_out.reshape(out_4d)


def _cross_attn_kernel(q_ref, mk_ref, mv_ref, *rest, n_heads, n_mem):
    o_ref = rest[-1]
    for h in range(n_heads):
        sl = slice(h * HEAD_DIM, (h + 1) * HEAD_DIM)
        q = (q_ref[:, sl] * SCALE).astype(BF16)
        mk = mk_ref[pl.ds(h, n_mem, stride=n_heads), :].astype(BF16)
        mv = mv_ref[pl.ds(h, n_mem, stride=n_heads), :].astype(BF16)
        o_ref[:, sl] = _softmax_pv([_dot_nt(q, mk)], [mv]).astype(o_ref.dtype)


def cross_attn(q, mk, mv, into, row0, nb, seq):
    _, n_mem, n_heads, _ = mk.shape
    w = n_heads * HEAD_DIM
    tq = _tile(seq, 512)
    n_t = seq // tq
    assert row0 % tq == 0
    r0 = row0 // tq
    mem_spec = pl.BlockSpec((None, n_mem * n_heads, HEAD_DIM), lambda b, i: (b, 0, 0))
    args = [q, mk.reshape(nb, n_mem * n_heads, HEAD_DIM), mv.reshape(nb, n_mem * n_heads, HEAD_DIM)]
    in_specs = [pl.BlockSpec((tq, w), lambda b, i: (r0 + b * n_t + i, 0)), mem_spec, mem_spec]
    aliases = {}
    if into is not None:
        aliases = {len(args): 0}
        args.append(into)
        in_specs.append(pl.BlockSpec(memory_space=pl.ANY))
    return pl.pallas_call(
        functools.partial(_cross_attn_kernel, n_heads=n_heads, n_mem=n_mem),
        grid=(nb, n_t),
        in_specs=in_specs,
        out_specs=pl.BlockSpec((tq, w), lambda b, i: (r0 + b * n_t + i, 0)),
        out_shape=jax.ShapeDtypeStruct((q.shape[0], w), BF16),
        input_output_aliases=aliases,
        compiler_params=_params("parallel", "parallel"),
        name="cross_attn",
    )(*args)


def kernel(x_prompt, x_sample, cache_sb_k, cache_sb_v, state_ret, cache_band_k, cache_band_v, cache_mem_k, cache_mem_v, mem_prompt, w_in_ab, w_out_ab, ret_norm_g, w_qkv_band, w_out_band, rel_bias_band, norm_g, mem_norm_g, w_xq, w_xk, w_xv, w_xo, ffn_w_gate, ffn_w_up, ffn_w_down, moe_router, moe_w_gate, moe_w_up, moe_w_down, final_norm_g):
    bp, s, d = x_prompt.shape
    nb, t, _ = x_sample.shape
    assert bp == 1
    past = cache_sb_k.shape[1]
    band_past = cache_band_k.shape[1]
    h_sb = cache_sb_k.shape[2]
    h_ret = state_ret.shape[1]
    h_band = cache_band_k.shape[2]
    d_sb, d_ret = h_sb * HEAD_DIM, h_ret * HEAD_DIM
    n_mem = mem_prompt.shape[1]
    d_x = w_xq.shape[2]
    depth = w_xq.shape[0]
    d_ff = ffn_w_gate.shape[1]
    ms = nb * t
    assert d_sb == d_ret and 3 * d_sb + 4 * d_ret == w_in_ab.shape[1]

    x = jnp.concatenate([x_prompt.reshape(s, d), x_sample.reshape(ms, d)], axis=0)
    mem = mem_prompt.reshape(n_mem, d)
    bf = lambda a: a.astype(BF16)

    cos_p, sin_p = _rope_tables(jnp.arange(s, dtype=jnp.int32))
    cos_s, sin_s = _rope_tables(past + jnp.arange(t, dtype=jnp.int32))

    mem_k_list, mem_v_list = [], []
    for l in range(depth):
        mem_n = rmsnorm(mem, mem_norm_g[l], BF16)
        mk_p = matmul([(mem_n, 0, 0)], bf(w_xk[l]), d)
        mv_p = matmul([(mem_n, 0, 0)], bf(w_xv[l]), d)
        mem_k_list.append(mk_p.reshape(1, n_mem, H_X, HEAD_DIM))
        mem_v_list.append(mv_p.reshape(1, n_mem, H_X, HEAD_DIM))

        if l % 2 == 0:
            p = norm_linear(x, norm_g[l, 0], [w_in_ab], F32, tn_pref=1024)
            o_sb = sb_prompt(p, s, h_sb, 0, h_sb, 2 * h_sb)
            o_sb = sb_sample(p, cache_sb_k, cache_sb_v, o_sb, s, nb, t, 0, 1, 2)
            t_ret = _tile(s, 256, CHUNK)
            o_r, ret_state_prompt = retention(p, cos_p, sin_p, ret_norm_g, jnp.zeros((1,) + state_ret.shape[1:], F32),
                                              None, 0, 1, s, t_ret, h_ret, 3, 4, 5, 6)
            o_r, ret_state_sample = retention(p, cos_s, sin_s, ret_norm_g, state_ret.astype(F32),
                                              o_r, s, nb, t, t, h_ret, 3, 4, 5, 6)
            x = matmul([(o_sb, 0, 0), (o_r, 0, d_sb)], bf(w_out_ab), d_sb, residual=x)
            sb_k_prompt = p[:s, d_sb:2 * d_sb].reshape(1, s, h_sb, HEAD_DIM)
            sb_v_prompt = p[:s, 2 * d_sb:3 * d_sb].reshape(1, s, h_sb, HEAD_DIM)
            sb_k_sample = p[s:, d_sb:2 * d_sb].reshape(nb, t, h_sb, HEAD_DIM)
            sb_v_sample = p[s:, 2 * d_sb:3 * d_sb].reshape(nb, t, h_sb, HEAD_DIM)
        else:
            d_band = h_band * HEAD_DIM
            p = norm_linear(x, norm_g[l, 0], [w_qkv_band], F32, tn_pref=1024)
            o = band_prompt(p, rel_bias_band, s, h_band, 0, h_band, 2 * h_band)
            o, band_k_sample, band_v_sample = band_sample(p, cache_band_k, cache_band_v, rel_bias_band, o,
                                                          s, nb, t, 0, 1, 2)
            x = matmul([(o, 0, 0)], bf(w_out_band), d_band, residual=x)
            band_k_prompt = p[s - band_past:s, d_band:2 * d_band].reshape(1, band_past, h_band, HEAD_DIM)
            band_v_prompt = p[s - band_past:s, 2 * d_band:].reshape(1, band_past, h_band, HEAD_DIM)

        q = norm_linear(x, norm_g[l, 1], [w_xq[l]], F32, tn_pref=1024)
        o = cross_attn(q, mem_k_list[-1], mem_v_list[-1], None, 0, 1, s)
        o = cross_attn(q, cache_mem_k[l], cache_mem_v[l], o, s, nb, t)
        x = matmul([(o, 0, 0)], bf(w_xo[l]), d_x, residual=x)

        if l % 2 == 0:
            act = norm_linear(x, norm_g[l, 2], [ffn_w_gate, ffn_w_up], BF16, tn_pref=512)
            x = matmul([(act, 0, 0)], bf(ffn_w_down), d_ff, residual=x, tk_pref=d_ff)
        else:
            h, gates, idx = moe_route(x, norm_g[l, 2], moe_router)
            x = moe_top2(x, h, idx, gates, bf(moe_w_gate), bf(moe_w_up), bf(moe_w_down))

    y_prompt = rmsnorm(x, final_norm_g, F32, 0, s).reshape(1, s, d)
    y_sample = rmsnorm(x, final_norm_g, F32, s, ms).reshape(nb, t, d)
    mem_k_prompt = jnp.stack(mem_k_list, axis=0)
    mem_v_prompt = jnp.stack(mem_v_list, axis=0)
    return (y_prompt, y_sample, sb_k_prompt, sb_v_prompt, sb_k_sample, sb_v_sample,
            ret_state_prompt, ret_state_sample, band_k_prompt, band_v_prompt,
            band_k_sample, band_v_sample, mem_k_prompt, mem_v_prompt)
```

```python
import functools

import numpy as np
import jax
import jax.numpy as jnp
from jax import lax
from jax.experimental import pallas as pl
from jax.experimental.pallas import tpu as pltpu

F32 = jnp.float32
BF16 = jnp.bfloat16

HEAD_DIM = 128
CHUNK = 64
N_BAND_CHUNKS = 8
BAND_WINDOW = N_BAND_CHUNKS * CHUNK
MAX_REL = 128
H_X = 4
TOP_K = 2
RMS_EPS = 1e-6
ROPE_BASE = 10000.0
NEG_INF = -1e30
SCALE = HEAD_DIM ** -0.5
LOG2E = 1.4426950408889634

VMEM_LIMIT_BYTES = 56 * 1024 * 1024


def _params(*sem):
    return pltpu.CompilerParams(dimension_semantics=sem, vmem_limit_bytes=VMEM_LIMIT_BYTES)


def _tile(n, pref, mult=8):
    t = min(n, pref)
    while t > mult and (n % t or t % mult):
        t -= mult
    assert n % t == 0, (n, pref)
    return t


def _dot(a, b):
    return jnp.dot(a, b, preferred_element_type=F32)


def _dot_nt(a, b):
    return lax.dot_general(a, b, (((1,), (1,)), ((), ())), preferred_element_type=F32)


def _dot_tn(a, b):
    return lax.dot_general(a, b, (((0,), (0,)), ((), ())), preferred_element_type=F32)


def _silu(a):
    return a * (1.0 / (1.0 + jnp.exp(-a)))


def _rows(parts):
    return parts[0] if len(parts) == 1 else jnp.concatenate(parts, axis=0)


def _rmsnorm_kernel(x_ref, g_ref, o_ref):
    x = x_ref[...]
    y = x * lax.rsqrt(jnp.mean(x * x, axis=-1, keepdims=True) + RMS_EPS) * g_ref[...]
    o_ref[...] = y.astype(o_ref.dtype)


def rmsnorm(x, g, out_dtype, row0=0, rows=None):
    d = x.shape[1]
    m = x.shape[0] - row0 if rows is None else rows
    tm = _tile(m, 512)
    assert row0 % tm == 0
    return pl.pallas_call(
        _rmsnorm_kernel,
        grid=(m // tm,),
        in_specs=[pl.BlockSpec((tm, d), lambda i: (row0 // tm + i, 0)),
                  pl.BlockSpec((1, d), lambda i: (0, 0))],
        out_specs=pl.BlockSpec((tm, d), lambda i: (i, 0)),
        out_shape=jax.ShapeDtypeStruct((m, d), out_dtype),
        compiler_params=_params("parallel"),
        name="rmsnorm",
    )(x, g.reshape(1, d).astype(F32))


def _mm_kernel(*refs, n_parts, has_res, nk):
    xs, ws = refs[:n_parts], refs[n_parts:2 * n_parts]
    pos = 2 * n_parts
    res_ref = refs[pos] if has_res else None
    pos += int(has_res)
    o_ref = refs[pos]
    part = _dot(xs[0][...], ws[0][...])
    for x_ref, w_ref in zip(xs[1:], ws[1:]):
        part = part + _dot(x_ref[...], w_ref[...])
    if nk == 1:
        if has_res:
            part = res_ref[...] + part
        o_ref[...] = part.astype(o_ref.dtype)
        return
    acc_ref = refs[pos + 1]
    k = pl.program_id(2)

    @pl.when(k == 0)
    def _():
        acc_ref[...] = part

    @pl.when(k > 0)
    def _():
        acc_ref[...] += part

    @pl.when(k == nk - 1)
    def _():
        out = acc_ref[...]
        if has_res:
            out = res_ref[...] + out
        o_ref[...] = out.astype(o_ref.dtype)


def matmul(parts, w, k_part, *, col_off=0, n_cols=None, out_dtype=F32, residual=None,
           tm_pref=1024, tn_pref=512, tk_pref=2048):
    m = parts[0][0].shape[0]
    n_cols = w.shape[1] - col_off if n_cols is None else n_cols
    tm = _tile(m, tm_pref)
    tn = _tile(n_cols, tn_pref, 128)
    tk = _tile(k_part, tk_pref, 128)
    nk = k_part // tk
    assert col_off % tn == 0
    in_specs, args = [], []
    for x, xo, _ in parts:
        assert xo % tk == 0
        in_specs.append(pl.BlockSpec((tm, tk), lambda i, j, k, xo=xo: (i, xo // tk + k)))
        args.append(x)
    for _, _, wo in parts:
        assert wo % tk == 0
        in_specs.append(pl.BlockSpec((tk, tn), lambda i, j, k, wo=wo: (wo // tk + k, col_off // tn + j)))
        args.append(w)
    if residual is not None:
        in_specs.append(pl.BlockSpec((tm, tn), lambda i, j, k: (i, j)))
        args.append(residual)
    return pl.pallas_call(
        functools.partial(_mm_kernel, n_parts=len(parts), has_res=residual is not None, nk=nk),
        grid=(m // tm, n_cols // tn, nk),
        in_specs=in_specs,
        out_specs=pl.BlockSpec((tm, tn), lambda i, j, k: (i, j)),
        out_shape=jax.ShapeDtypeStruct((m, n_cols), out_dtype),
        scratch_shapes=[pltpu.VMEM((tm, tn), F32)] if nk > 1 else [],
        compiler_params=_params("parallel", "parallel", "arbitrary"),
        name="matmul",
    )(*args)


def _norm_linear_kernel(x_ref, g_ref, *rest, n_w):
    w_refs, o_ref, wb_refs = rest[:n_w], rest[n_w], rest[n_w + 1:]

    @pl.when(pl.program_id(1) == 0)
    def _():
        for w_ref, wb_ref in zip(w_refs, wb_refs):
            wb_ref[...] = w_ref[...].astype(BF16)

    x = x_ref[...]
    h = (x * lax.rsqrt(jnp.mean(x * x, axis=-1, keepdims=True) + RMS_EPS) * g_ref[...]).astype(BF16)
    if n_w == 1:
        out = _dot(h, wb_refs[0][...])
    else:
        out = _silu(_dot(h, wb_refs[0][...])) * _dot(h, wb_refs[1][...])
    o_ref[...] = out.astype(o_ref.dtype)


def norm_linear(x, g, ws, out_dtype, *, tn_pref):
    m, d = x.shape
    n = ws[0].shape[1]
    tm = _tile(m, 1024)
    tn = _tile(n, tn_pref, 128)
    return pl.pallas_call(
        functools.partial(_norm_linear_kernel, n_w=len(ws)),
        grid=(n // tn, m // tm),
        in_specs=([pl.BlockSpec((tm, d), lambda j, i: (i, 0)), pl.BlockSpec((1, d), lambda j, i: (0, 0))]
                  + [pl.BlockSpec((d, tn), lambda j, i: (0, j))] * len(ws)),
        out_specs=pl.BlockSpec((tm, tn), lambda j, i: (i, j)),
        out_shape=jax.ShapeDtypeStruct((m, n), out_dtype),
        scratch_shapes=[pltpu.VMEM((d, tn), BF16)] * len(ws),
        compiler_params=_params("parallel", "arbitrary"),
        name="norm_linear",
    )(x, g.reshape(1, d).astype(F32), *[w.astype(F32) for w in ws])


def _router_kernel(x_ref, g_ref, r_ref, h_ref, o_ref, *, n_experts):
    x = x_ref[...]
    hn = x * lax.rsqrt(jnp.mean(x * x, axis=-1, keepdims=True) + RMS_EPS) * g_ref[...]
    h_ref[...] = hn.astype(h_ref.dtype)
    logits = jnp.dot(hn, r_ref[...], preferred_element_type=F32, precision=lax.Precision.HIGHEST)
    lane = lax.broadcasted_iota(jnp.int32, logits.shape, 1)
    n_lanes = logits.shape[1]
    lg = jnp.where(lane < n_experts, logits, -jnp.inf)
    m1 = jnp.max(lg, axis=1, keepdims=True)
    i1 = jnp.min(jnp.where(lg == m1, lane, n_lanes), axis=1, keepdims=True)
    lg2 = jnp.where(lane == i1, -jnp.inf, lg)
    m2 = jnp.max(lg2, axis=1, keepdims=True)
    i2 = jnp.min(jnp.where(lg2 == m2, lane, n_lanes), axis=1, keepdims=True)
    e2 = jnp.exp(m2 - m1)
    inv = 1.0 / (1.0 + e2)
    o_ref[...] = (jnp.where(lane == 0, inv, 0.0) + jnp.where(lane == 1, e2 * inv, 0.0)
                  + jnp.where(lane == 2, i1.astype(F32), 0.0) + jnp.where(lane == 3, i2.astype(F32), 0.0))


def moe_route(x, g, router):
    m, d = x.shape
    n_e = router.shape[1]
    tm = _tile(m, 512)
    r_pad = jnp.zeros((d, 128), F32).at[:, :n_e].set(router.astype(F32))
    h, out = pl.pallas_call(
        functools.partial(_router_kernel, n_experts=n_e),
        grid=(m // tm,),
        in_specs=[pl.BlockSpec((tm, d), lambda i: (i, 0)),
                  pl.BlockSpec((1, d), lambda i: (0, 0)),
                  pl.BlockSpec((d, 128), lambda i: (0, 0))],
        out_specs=[pl.BlockSpec((tm, d), lambda i: (i, 0)), pl.BlockSpec((tm, 128), lambda i: (i, 0))],
        out_shape=[jax.ShapeDtypeStruct((m, d), BF16), jax.ShapeDtypeStruct((m, 128), F32)],
        compiler_params=_params("parallel"),
        name="moe_router",
    )(x, g.reshape(1, d).astype(F32), r_pad)
    return h, out[:, :TOP_K], out[:, TOP_K:2 * TOP_K].astype(jnp.int32)


MOE_ROW_TILE = 256
MOE_GATHER_CHUNK = 512
MOE_COMBINE_TILE = 512


def _i32(a):
    return a.astype(jnp.int32)


def _count_le(ends, v):
    return jnp.sum(_i32(ends[None, :] <= v[:, None]), axis=1, dtype=jnp.int32)


def _moe_plan(idx, gates, n_e, tr, tc, tt):
    m = idx.shape[0]
    assert (TOP_K * m) % tr == 0 and m % tc == 0 and m % tt == 0
    n_tiles = TOP_K * m // tr + n_e
    n_rows = n_tiles * tr
    routed = jnp.zeros((m, n_e), jnp.int32)
    for k in range(TOP_K):
        routed = routed + _i32(idx[:, k:k + 1] == jnp.arange(n_e, dtype=jnp.int32)[None, :])
    csum = jnp.cumsum(routed, axis=0, dtype=jnp.int32)
    rank = csum - routed
    tiles_e = (csum[-1] + tr - 1) // tr
    tile_end = jnp.cumsum(tiles_e, dtype=jnp.int32)
    row_start = (tile_end - tiles_e) * tr
    dest = row_start[idx] + jnp.take_along_axis(rank, idx, axis=1)
    gate_rows = jnp.zeros((n_rows,), F32).at[dest.reshape(-1)].set(gates.reshape(-1), unique_indices=True)
    tile_ids = jnp.arange(n_tiles, dtype=jnp.int32)
    tile_valid = tile_ids < tile_end[-1]
    tile_expert = jnp.minimum(_count_le(tile_end, tile_ids), n_e - 1)

    def visits(first_chunk, n_visits, n_max, owner_of):
        end = jnp.cumsum(n_visits, dtype=jnp.int32)
        start = end - n_visits
        v = jnp.arange(n_max, dtype=jnp.int32)
        live = v < end[-1]
        slot = jnp.minimum(_count_le(end, v), n_visits.shape[0] - 1)
        owner = owner_of(slot)
        chunk = first_chunk[slot] + v - start[slot]
        last = jnp.maximum(end[-1] - 1, 0)
        owner = jnp.where(live, owner, owner[last])
        chunk = jnp.where(live, chunk, chunk[last])
        first = live & ((v == 0) | (owner != jnp.roll(owner, 1)))
        return _i32(owner), _i32(chunk), _i32(first) + 2 * _i32(live)

    rank0 = (tile_ids - (tile_end - tiles_e)[tile_expert]) * tr
    rank1 = jnp.minimum(rank0 + tr, csum[-1][tile_expert]) - 1
    csum_e = csum.T[tile_expert]
    first_tok = jnp.sum(_i32(csum_e <= rank0[:, None]), axis=1, dtype=jnp.int32)
    last_tok = jnp.sum(_i32(csum_e <= rank1[:, None]), axis=1, dtype=jnp.int32)
    c_first = jnp.where(tile_valid, first_tok, 0) // tc
    c_last = jnp.where(tile_valid, last_tok, 0) // tc
    g_plan = visits(c_first, c_last - c_first + 1, n_tiles + n_e * (m // tc - 1), lambda s: s)
    n_tt = m // tt
    before = jnp.concatenate([jnp.zeros((1, n_e), jnp.int32), csum[tt - 1::tt]], axis=0)
    lo = row_start[None, :] + before[:-1]
    hi = row_start[None, :] + before[1:]
    n_vis = jnp.where(hi > lo, (hi - 1) // tr - lo // tr + 1, 0)
    c_plan = visits((lo // tr).reshape(-1), n_vis.reshape(-1), n_tt * n_e + n_tiles - 1, lambda s: s // n_e)
    return dict(n_tiles=n_tiles, dest=dest, gate_rows=gate_rows.reshape(n_rows, 1),
                tile_expert=tile_expert, tile_valid=_i32(tile_valid), gather=g_plan, combine=c_plan)


def _moe_select(dest_ref, row0, n_rows):
    row = row0 + lax.broadcasted_iota(jnp.int32, (dest_ref.shape[0], n_rows), 1)
    hit = row == dest_ref[:, 0:1]
    for k in range(1, TOP_K):
        hit = hit | (row == dest_ref[:, k:k + 1])
    return jnp.where(hit, 1.0, 0.0).astype(BF16)


def _moe_gather_kernel(vt_ref, vc_ref, vf_ref, dest_ref, h_ref, o_ref):
    v = pl.program_id(0)
    flags = vf_ref[v]
    tr = o_ref.shape[0]

    def gathered():
        sel = _moe_select(dest_ref, vt_ref[v] * tr, tr)
        return _dot_tn(sel, h_ref[...]).astype(o_ref.dtype)

    @pl.when(flags == 3)
    def _():
        o_ref[...] = gathered()

    @pl.when(flags == 2)
    def _():
        o_ref[...] += gathered()


def _moe_up_kernel(te_ref, tv_ref, x_ref, wg_ref, wu_ref, o_ref):
    j = pl.program_id(1)

    @pl.when(tv_ref[j] != 0)
    def _():
        x = x_ref[...]
        o_ref[...] = (_silu(_dot(x, wg_ref[...])) * _dot(x, wu_ref[...])).astype(o_ref.dtype)

    @pl.when(tv_ref[j] == 0)
    def _():
        o_ref[...] = jnp.zeros_like(o_ref)


def _moe_down_kernel(te_ref, tv_ref, a_ref, wd_ref, g_ref, o_ref):
    j = pl.program_id(0)

    @pl.when(tv_ref[j] != 0)
    def _():
        o_ref[...] = (_dot(a_ref[...], wd_ref[...]) * g_ref[...]).astype(o_ref.dtype)

    @pl.when(tv_ref[j] == 0)
    def _():
        o_ref[...] = jnp.zeros_like(o_ref)


def _moe_combine_kernel(vi_ref, vc_ref, vf_ref, dest_ref, y_ref, x_ref, o_ref):
    v = pl.program_id(0)
    flags = vf_ref[v]
    tr = y_ref.shape[0]

    @pl.when(flags % 2 == 1)
    def _():
        o_ref[...] = x_ref[...]

    @pl.when(flags >= 2)
    def _():
        sel = _moe_select(dest_ref, vc_ref[v] * tr, tr)
        o_ref[...] += _dot(sel, y_ref[...])


def moe_top2(x, h, idx, gates, wg, wu, wd):
    m, d = h.shape
    n_e, _, f = wg.shape
    tr, tc, tt = MOE_ROW_TILE, _tile(m, MOE_GATHER_CHUNK, 16), _tile(m, MOE_COMBINE_TILE)
    plan =_moe_plan(idx, gates, n_e, tr, tc, tt)
    n_tiles = plan["n_tiles"]
    n_rows = n_tiles * tr

    vt, vc, vf = plan["gather"]
    xs = pl.pallas_call(
        _moe_gather_kernel,
        grid_spec=pltpu.PrefetchScalarGridSpec(
            num_scalar_prefetch=3, grid=(vt.shape[0],),
            in_specs=[pl.BlockSpec((tc, TOP_K), lambda v, vt, vc, vf: (vc[v], 0)),
                      pl.BlockSpec((tc, d), lambda v, vt, vc, vf: (vc[v], 0))],
            out_specs=pl.BlockSpec((tr, d), lambda v, vt, vc, vf: (vt[v], 0))),
        out_shape=jax.ShapeDtypeStruct((n_rows, d), BF16),
        compiler_params=_params("arbitrary"),
        name="moe_gather",
    )(vt, vc, vf, plan["dest"], h)

    tf = _tile(f, 1408, 128)
    act = pl.pallas_call(
        _moe_up_kernel,
        grid_spec=pltpu.PrefetchScalarGridSpec(
            num_scalar_prefetch=2, grid=(f // tf, n_tiles),
            in_specs=[pl.BlockSpec((tr, d), lambda c, j, te, tv: (j, 0)),
                      pl.BlockSpec((None, d, tf), lambda c, j, te, tv: (te[j], 0, c)),
                      pl.BlockSpec((None, d, tf), lambda c, j, te, tv: (te[j], 0, c))],
            out_specs=pl.BlockSpec((tr, tf), lambda c, j, te, tv: (j, c))),
        out_shape=jax.ShapeDtypeStruct((n_rows, f), BF16),
        compiler_params=_params("parallel", "parallel"),
        name="moe_up",
    )(plan["tile_expert"], plan["tile_valid"], xs, wg, wu)

    y = pl.pallas_call(
        _moe_down_kernel,
        grid_spec=pltpu.PrefetchScalarGridSpec(
            num_scalar_prefetch=2, grid=(n_tiles,),
            in_specs=[pl.BlockSpec((tr, f), lambda j, te, tv: (j, 0)),
                      pl.BlockSpec((None, f, d), lambda j, te, tv: (te[j], 0, 0)),
                      pl.BlockSpec((tr, 1), lambda j, te, tv: (j, 0))],
            out_specs=pl.BlockSpec((tr, d), lambda j, te, tv: (j, 0))),
        out_shape=jax.ShapeDtypeStruct((n_rows, d), BF16),
        compiler_params=_params("parallel"),
        name="moe_down",
    )(plan["tile_expert"], plan["tile_valid"], act, wd, plan["gate_rows"])

    vi, vc, vf = plan["combine"]
    return pl.pallas_call(
        _moe_combine_kernel,
        grid_spec=pltpu.PrefetchScalarGridSpec(
            num_scalar_prefetch=3, grid=(vi.shape[0],),
            in_specs=[pl.BlockSpec((tt, TOP_K), lambda v, vi, vc, vf: (vi[v], 0)),
                      pl.BlockSpec((tr, d), lambda v, vi, vc, vf: (vc[v], 0)),
                      pl.BlockSpec((tt, d), lambda v, vi, vc, vf: (vi[v], 0))],
            out_specs=pl.BlockSpec((tt, d), lambda v, vi, vc, vf: (vi[v], 0))),
        out_shape=jax.ShapeDtypeStruct((m, d), F32),
        compiler_params=_params("arbitrary"),
        name="moe_combine",
    )(vi, vc, vf, plan["dest"], y, x)


def _cumsum_matrix(tk):
    r = np.arange(2 * tk)[:, None]
    c = np.arange(2 * tk)[None, :]
    return jnp.asarray((r > c).astype(np.float32), dtype=BF16)


def _sb_sweep_step(qs, k2s, v2s, before, u, acc, carry):
    tq = qs[0].shape[0]
    tk = k2s[0].shape[0] // 2
    z = _rows([_dot_nt(q, k2) for q, k2 in zip(qs, k2s)])
    neg_l = jnp.maximum(z, 0.0) + jnp.log2(1.0 + jnp.exp2(-jnp.abs(z)))
    log_b = z - neg_l
    if before is not None:
        neg_l = jnp.where(before, neg_l, 0.0)
    neg_lb = neg_l.astype(BF16)
    after = _dot(neg_lb, u)
    w = jnp.exp2(log_b - (after + jnp.concatenate([carry, carry], axis=1)))
    if before is not None:
        w = jnp.where(before, w, 0.0)
    w = w.astype(BF16)
    acc = acc + _rows([_dot(w[n * tq:(n + 1) * tq], v2) for n, v2 in enumerate(v2s)])
    total = after[:, :1] + neg_lb[:, :1].astype(F32)
    return acc, carry + total


def _sb_prompt_kernel(q_ref, k_ref, v_ref, u_ref, o_ref, kb_ref, vb_ref, *, tq, tk):
    i = pl.program_id(1)

    @pl.when(i == 0)
    def _():
        kb_ref[...] = k_ref[...].astype(BF16)
        vb_ref[...] = v_ref[...].astype(BF16)

    q = (q_ref[...] * (SCALE * LOG2E)).astype(BF16)
    u = u_ref[...]
    steps_per_tile = tq // (2 * tk)
    state = (jnp.zeros((tq, HEAD_DIM), F32), jnp.zeros((tq, tk), F32))

    def step(off, r0, before, state):
        new = _sb_sweep_step([q[r0:]], [kb_ref[pl.ds(off, 2 * tk), :]], [vb_ref[pl.ds(off, 2 * tk), :]],
                             before, u, *[a[r0:] for a in state])
        return tuple(jnp.concatenate([a[:r0], b], axis=0) if r0 else b for a, b in zip(state, new))

    for r0 in reversed(range(0, tq, 2 * tk)):
        before = (lax.broadcasted_iota(jnp.int32, (tq - r0, 2 * tk), 1)
                  < lax.broadcasted_iota(jnp.int32, (tq - r0, 2 * tk), 0))
        state = step(pl.multiple_of(i * tq + r0, 2 * tk), r0, before, state)
    n_steps = i * steps_per_tile
    unroll = max(u for u in (1, 2, 4) if steps_per_tile % u == 0)

    def body(s, st):
        for k in range(unroll):
            st = step(pl.multiple_of((n_steps - 1 - unroll * s - k) * 2 * tk, 2 * tk), 0, None, st)
        return st

    acc, _ = lax.fori_loop(0, n_steps // unroll, body, state)
    o_ref[...] = acc.astype(o_ref.dtype)


def sb_prompt(p, s, n_heads, q_col, k_col, v_col):
    tk = HEAD_DIM
    tq = _tile(s, 2048, 2 * tk)
    return pl.pallas_call(
        functools.partial(_sb_prompt_kernel, tq=tq, tk=tk),
        grid=(n_heads, s // tq),
        in_specs=[pl.BlockSpec((tq, HEAD_DIM), lambda h, i: (i, q_col + h)),
                  pl.BlockSpec((s, HEAD_DIM), lambda h, i: (0, k_col + h)),
                  pl.BlockSpec((s, HEAD_DIM), lambda h, i: (0, v_col + h)),
                  pl.BlockSpec((2 * tk, 2 * tk), lambda h, i: (0, 0))],
        out_specs=pl.BlockSpec((tq, HEAD_DIM), lambda h, i: (i, h)),
        out_shape=jax.ShapeDtypeStruct((p.shape[0], n_heads * HEAD_DIM), BF16),
        scratch_shapes=[pltpu.VMEM((s, HEAD_DIM), BF16), pltpu.VMEM((s, HEAD_DIM), BF16)],
        compiler_params=_params("parallel", "arbitrary"),
        name="sb_prompt",
    )(p, p, p, _cumsum_matrix(tk))


def _sb_sample_kernel(q_ref, kn_ref, vn_ref, kc_hbm, vc_hbm, u_ref, _into_ref, o_ref,
                      acc_ref, carry_ref, kf_ref, vf_ref, sem, *, t, tk, ck, n_heads):
    b, c = pl.program_id(0), pl.program_id(1)
    n_c = pl.num_programs(1)
    step = b * n_c + c
    slot = step % 2

    def chunk_copies(step, slot):
        first = (n_c - 1 - step % n_c) * ck
        copies = []
        for h in range(n_heads):
            for hbm, buf in ((kc_hbm, kf_ref), (vc_hbm, vf_ref)):
                copies.append(pltpu.make_async_copy(hbm.at[step // n_c, pl.ds(first, ck), h, :],
                                                    buf.at[slot, h], sem.at[slot]))
        return copies

    @pl.when(step == 0)
    def _():
        for cp in chunk_copies(step, slot):
            cp.start()

    @pl.when(step + 1 < pl.num_programs(0) * n_c)
    def _():
        for cp in chunk_copies(step + 1, 1 - slot):
            cp.start()

    heads = [slice(h * HEAD_DIM, (h + 1) * HEAD_DIM) for h in range(n_heads)]
    qs = [(q_ref[:, sl] * (SCALE * LOG2E)).astype(BF16) for sl in heads]
    u = u_ref[...]

    @pl.when(c == 0)
    def _():
        reps = 2 * tk // t
        k2s = [jnp.concatenate([kn_ref[:, sl].astype(BF16)] * reps, axis=0) for sl in heads]
        v2s = [jnp.concatenate([vn_ref[:, sl].astype(BF16)] * reps, axis=0) for sl in heads]
        before = (lax.broadcasted_iota(jnp.int32, (t, 2 * tk), 1) < lax.broadcasted_iota(jnp.int32, (t, 2 * tk), 0))
        acc, carry = _sb_sweep_step(qs, k2s, v2s, _rows([before] * n_heads), u,
                                    jnp.zeros(acc_ref.shape, F32), jnp.zeros(carry_ref.shape, F32))
        acc_ref[...] = acc
        carry_ref[...] = carry

    for cp in chunk_copies(step, slot):
        cp.wait()
    n_steps = ck // (2 * tk)

    def body(s, state):
        off = pl.multiple_of((n_steps - 1 - s) * 2 * tk, 2 * tk)
        k2s = [kf_ref[slot, h, pl.ds(off, 2 * tk), :].astype(BF16) for h in range(n_heads)]
        v2s = [vf_ref[slot, h, pl.ds(off, 2 * tk), :].astype(BF16) for h in range(n_heads)]
        return _sb_sweep_step(qs, k2s, v2s, None, u, *state)

    acc, carry = lax.fori_loop(0, n_steps, body, (acc_ref[...], carry_ref[...]), unroll=2)
    acc_ref[...] = acc
    carry_ref[...] = carry

    @pl.when(c == pl.num_programs(1) - 1)
    def _():
        for h, sl in enumerate(heads):
            o_ref[:, sl] = acc_ref[h * t:(h + 1) * t, :].astype(o_ref.dtype)


def sb_sample(p, cache_k, cache_v, into, row0, nb, t, q_col, k_col, v_col):
    _, past, n_heads, _ = cache_k.shape
    tk = HEAD_DIM
    w = n_heads * HEAD_DIM
    ck = _tile(past, 1024, 2 * tk)
    n_c = past // ck
    assert (2 * tk) % t == 0 and row0 % t == 0
    r0 = row0 // t
    return pl.pallas_call(
        functools.partial(_sb_sample_kernel, t=t, tk=tk, ck=ck, n_heads=n_heads),
        grid=(nb, n_c),
        in_specs=[pl.BlockSpec((t, w), lambda b, c: (r0 + b, q_col)),
                  pl.BlockSpec((t, w), lambda b, c: (r0 + b, k_col)),
                  pl.BlockSpec((t, w), lambda b, c: (r0 + b, v_col)),
                  pl.BlockSpec(memory_space=pl.ANY), pl.BlockSpec(memory_space=pl.ANY),
                  pl.BlockSpec((2 * tk, 2 * tk), lambda b, c: (0, 0)),
                  pl.BlockSpec(memory_space=pl.ANY)],
        out_specs=pl.BlockSpec((t, w), lambda b, c: (r0 + b, 0)),
        out_shape=jax.ShapeDtypeStruct(into.shape, into.dtype),
        input_output_aliases={6: 0},
        scratch_shapes=[pltpu.VMEM((n_heads * t, HEAD_DIM), F32), pltpu.VMEM((n_heads * t, tk), F32),
                        pltpu.VMEM((2, n_heads, ck, HEAD_DIM), F32), pltpu.VMEM((2, n_heads, ck, HEAD_DIM), F32),
                        pltpu.SemaphoreType.DMA((2,))],
        compiler_params=_params("arbitrary", "arbitrary"),
        name="sb_sample",
    )(p, p, p, cache_k.astype(F32), cache_v.astype(F32), _cumsum_matrix(tk), into)


def _retention_log_decay(n_heads):
    return [float(np.log1p(-np.float32(2.0 ** (-5.0 - h)))) for h in range(n_heads)]


def _retention_kernel(q_ref, k_ref, v_ref, g_ref, cos_ref, sin_ref, gn_ref, s0_ref, *rest, t, n_chunks, log_g):
    o_ref, sout_ref, state_ref, dec_ref = rest[-4:]
    c = pl.program_id(1)

    @pl.when(c == 0)
    def _():
        state_ref[...] = s0_ref[...]
        rel = (lax.broadcasted_iota(jnp.int32, (t, t), 0) - lax.broadcasted_iota(jnp.int32, (t, t), 1)).astype(F32)
        for h, lg in enumerate(log_g):
            dec_ref[h] = jnp.where(rel >= 0, jnp.exp(lg * jnp.maximum(rel, 0.0)), 0.0)

    cos, sin = cos_ref[...], sin_ref[...]
    idx = lax.broadcasted_iota(jnp.int32, (t, HEAD_DIM), 0).astype(F32)
    for h, lg in enumerate(log_g):
        sl = slice(h * HEAD_DIM, (h + 1) * HEAD_DIM)
        qh, kh = q_ref[:, sl], k_ref[:, sl]
        qh = qh * cos + pltpu.roll(qh, HEAD_DIM // 2, 1) * sin
        kh = (kh * cos + pltpu.roll(kh, HEAD_DIM // 2, 1) * sin) * SCALE
        qb, vb = qh.astype(BF16), v_ref[:, sl].astype(BF16)
        scores = _dot_nt(qb, kh.astype(BF16)) * dec_ref[h]
        state = state_ref[h]
        o = _dot(scores.astype(BF16), vb) + _dot(qb, state.astype(BF16)) * jnp.exp((idx + 1.0) * lg)
        k_dec = (kh * jnp.exp((t - 1.0 - idx) * lg)).astype(BF16)
        state_ref[h] = float(np.exp(np.float32(t * lg))) * state + _dot_tn(k_dec, vb)
        o = o * lax.rsqrt(jnp.mean(o * o, axis=-1, keepdims=True) + RMS_EPS) * gn_ref[:, sl]
        o_ref[:, sl] = (o * _silu(g_ref[:, sl])).astype(o_ref.dtype)

    @pl.when(c == n_chunks - 1)
    def _():
        sout_ref[...] = state_ref[...]


def retention(p, cos2, sin2, ret_norm_g, state0, into, row0, nb, seq, t, n_heads, q_col, k_col, v_col, g_col):
    w = n_heads * HEAD_DIM
    n_chunks = seq // t
    assert row0 % t == 0 and seq % t == 0
    r0 = row0 // t

    def rows(col):
        return pl.BlockSpec((t, w), lambda b, c: (r0 + b * n_chunks + c, col))

    args = [p, p, p, p, cos2, sin2, ret_norm_g.reshape(1, w).astype(F32), state0]
    in_specs = [rows(q_col), rows(k_col), rows(v_col), rows(g_col),
                pl.BlockSpec((t, HEAD_DIM), lambda b, c: (c, 0)),
                pl.BlockSpec((t, HEAD_DIM), lambda b, c: (c, 0)),
                pl.BlockSpec((1, w), lambda b, c: (0, 0)),
                pl.BlockSpec((None, n_heads, HEAD_DIM, HEAD_DIM), lambda b, c: (b, 0, 0, 0))]
    aliases = {}
    if into is not None:
        aliases = {len(args): 0}
        args.append(into)
        in_specs.append(pl.BlockSpec(memory_space=pl.ANY))
    return pl.pallas_call(
        functools.partial(_retention_kernel, t=t, n_chunks=n_chunks, log_g=_retention_log_decay(n_heads)),
        grid=(nb, n_chunks),
        in_specs=in_specs,
        out_specs=[rows(0),
                   pl.BlockSpec((None, n_heads, HEAD_DIM, HEAD_DIM), lambda b, c: (b, 0, 0, 0))],
        out_shape=[jax.ShapeDtypeStruct((p.shape[0], w), BF16),
                   jax.ShapeDtypeStruct((nb, n_heads, HEAD_DIM, HEAD_DIM), F32)],
        input_output_aliases=aliases,
        scratch_shapes=[pltpu.VMEM((n_heads, HEAD_DIM, HEAD_DIM), F32), pltpu.VMEM((n_heads, t, t), F32)],
        compiler_params=_params("parallel", "arbitrary"),
        name="retention",
    )(*args)


def _rope_tables(pos):
    half = HEAD_DIM // 2
    inv = ROPE_BASE ** (-jnp.arange(half, dtype=F32) / half)
    ang = pos.astype(F32)[:, None] * inv[None, :]
    cos, sin = jnp.cos(ang), jnp.sin(ang)
    return jnp.concatenate([cos, cos], axis=1), jnp.concatenate([-sin, sin], axis=1)


def _band_bias_table(rel_bias, tq):
    n_heads = rel_bias.shape[0]
    w = BAND_WINDOW + CHUNK
    n = CHUNK + w - 1
    dist = BAND_WINDOW + CHUNK - 1 - np.arange(n)
    vec = rel_bias.astype(F32)[:, np.clip(dist, -MAX_REL, MAX_REL) + MAX_REL]
    vec = jnp.roll(vec, -(CHUNK - 1), axis=1)
    chunk_bias = jnp.tile(vec, (1, CHUNK))[:, :CHUNK * (n - 1)].reshape(n_heads, CHUNK, n - 1)[:, :, :w]
    return jnp.concatenate(
        [jnp.pad(chunk_bias, ((0, 0), (0, 0), (c * CHUNK, tq - (c + 1) * CHUNK)), constant_values=NEG_INF)
         for c in range(tq // CHUNK)], axis=1)


def _softmax_pv(scores, values):
    m = scores[0].max(axis=1, keepdims=True)
    for s in scores[1:]:
        m = jnp.maximum(m, s.max(axis=1, keepdims=True))
    ps = [jnp.exp(s - m) for s in scores]
    denom = ps[0].sum(axis=1, keepdims=True)
    for p in ps[1:]:
        denom = denom + p.sum(axis=1, keepdims=True)
    o = _dot(ps[0].astype(BF16), values[0])
    for p, v in zip(ps[1:], values[1:]):
        o = o + _dot(p.astype(BF16), v)
    return o * (1.0 / denom)


BAND_HEADS_PER_STEP = 8


def _band_prompt_kernel(*refs, tq, n_kb, hb):
    q_ref = refs[0]
    k_refs, v_refs = refs[1:1 + n_kb], refs[1 + n_kb:1 + 2 * n_kb]
    b_ref, o_ref = refs[1 + 2 * n_kb], refs[2 + 2 * n_kb]
    i = pl.program_id(1)
    for h in range(hb):
        sl = slice(h * HEAD_DIM, (h + 1) * HEAD_DIM)
        q = (q_ref[:, sl] * SCALE).astype(BF16)
        scores, values = [], []
        for kb in range(n_kb):
            s = _dot_nt(q, k_refs[kb][:, sl].astype(BF16)) + b_ref[h, :, kb * tq:(kb + 1) * tq]
            scores.append(s + jnp.where(i - (n_kb - 1) + kb >= 0, 0.0, NEG_INF))
            values.append(v_refs[kb][:, sl].astype(BF16))
        o_ref[:, sl] = _softmax_pv(scores, values).astype(o_ref.dtype)


def band_prompt(p, rel_bias, s, n_heads, q_col, k_col, v_col):
    tq = _tile(s, 256)
    hb = BAND_HEADS_PER_STEP
    w = hb * HEAD_DIM
    assert BAND_WINDOW % tq == 0 and tq % CHUNK == 0 and n_heads % hb == 0
    assert q_col % hb == 0 and k_col % hb == 0 and v_col % hb == 0
    n_kb = BAND_WINDOW // tq + 1

    def kv_specs(col):
        return [pl.BlockSpec((tq, w), lambda h, i, kb=kb: (jnp.maximum(i - (n_kb - 1) + kb, 0), col // hb + h))
                for kb in range(n_kb)]

    return pl.pallas_call(
        functools.partial(_band_prompt_kernel, tq=tq, n_kb=n_kb, hb=hb),
        grid=(n_heads // hb, s // tq),
        in_specs=([pl.BlockSpec((tq, w), lambda h, i: (i, q_col // hb + h))] + kv_specs(k_col) + kv_specs(v_col)
                  + [pl.BlockSpec((hb, tq, BAND_WINDOW + tq), lambda h, i: (h, 0, 0))]),
        out_specs=pl.BlockSpec((tq, w), lambda h, i: (i, h)),
        out_shape=jax.ShapeDtypeStruct((p.shape[0], n_heads * HEAD_DIM), BF16),
        compiler_params=_params("parallel", "parallel"),
        name="band_prompt",
    )(*([p] * (1 + 2 * n_kb)), _band_bias_table(rel_bias, tq))


def _band_sample_kernel(q_ref, kn_ref, vn_ref, kc_ref, vc_ref, b_ref, _into_ref, o_ref, ko_ref, vo_ref, *,
                        band_past, t, n_heads):
    keep = (band_past - t) * n_heads
    for h in range(n_heads):
        sl = slice(h * HEAD_DIM, (h + 1) * HEAD_DIM)
        q = (q_ref[:, sl] * SCALE).astype(BF16)
        kn, vn = kn_ref[:, sl], vn_ref[:, sl]
        kc = kc_ref[pl.ds(h, band_past, stride=n_heads), :].astype(BF16)
        vc = vc_ref[pl.ds(h, band_past, stride=n_heads), :].astype(BF16)
        s_c = _dot_nt(q, kc) + b_ref[h, :, :band_past]
        s_n = _dot_nt(q, kn.astype(BF16)) + b_ref[h, :, band_past:]
        o_ref[:, sl] = _softmax_pv([s_c, s_n], [vc, vn.astype(BF16)]).astype(o_ref.dtype)
        ko_ref[pl.ds(keep + h, t, stride=n_heads), :] = kn
        vo_ref[pl.ds(keep + h, t, stride=n_heads), :] = vn
    ko_ref[:keep] = kc_ref[t * n_heads:]
    vo_ref[:keep] = vc_ref[t * n_heads:]


def band_sample(p, cache_k, cache_v, rel_bias, into, row0, nb, t, q_col, k_col, v_col):
    _, band_past, n_heads, _ = cache_k.shape
    w = n_heads * HEAD_DIM
    assert t == CHUNK and band_past == BAND_WINDOW and row0 % t == 0
    r0 = row0 // t
    cache_spec = pl.BlockSpec((None, band_past * n_heads, HEAD_DIM), lambda b: (b, 0, 0))
    cache_shape = jax.ShapeDtypeStruct((nb, band_past * n_heads, HEAD_DIM), F32)
    cache_k = cache_k.reshape(cache_shape.shape)
    cache_v = cache_v.reshape(cache_shape.shape)
    o, k_out, v_out = pl.pallas_call(
        functools.partial(_band_sample_kernel, band_past=band_past, t=t, n_heads=n_heads),
        grid=(nb,),
        in_specs=[pl.BlockSpec((t, w), lambda b: (r0 + b, q_col)),
                  pl.BlockSpec((t, w), lambda b: (r0 + b, k_col)),
                  pl.BlockSpec((t, w), lambda b: (r0 + b, v_col)),
                  cache_spec, cache_spec,
                  pl.BlockSpec((n_heads, t, band_past + t), lambda b: (0, 0, 0)),
                  pl.BlockSpec(memory_space=pl.ANY)],
        out_specs=[pl.BlockSpec((t, w), lambda b: (r0 + b, 0)), cache_spec, cache_spec],
        out_shape=[jax.ShapeDtypeStruct(into.shape, into.dtype), cache_shape, cache_shape],
        input_output_aliases={6: 0},
        compiler_params=_params("parallel"),
        name="band_sample",
    )(p, p, p, cache_k.astype(F32), cache_v.astype(F32), _band_bias_table(rel_bias, t), into)
    out_4d = (nb, band_past, n_heads, HEAD_DIM)
    return o, k_out.reshape(out_4d), v_out.reshape(out_4d)


def _cross_attn_kernel(q_ref, mk_ref, mv_ref, *rest, n_heads, n_mem):
    o_ref = rest[-1]
    for h in range(n_heads):
        sl = slice(h * HEAD_DIM, (h + 1) * HEAD_DIM)
        q = (q_ref[:, sl] * SCALE).astype(BF16)
        mk = mk_ref[pl.ds(h, n_mem, stride=n_heads), :].astype(BF16)
        mv = mv_ref[pl.ds(h, n_mem, stride=n_heads), :].astype(BF16)
        o_ref[:, sl] = _softmax_pv([_dot_nt(q, mk)], [mv]).astype(o_ref.dtype)


def cross_attn(q, mk, mv, into, row0, nb, seq):
    _, n_mem, n_heads, _ = mk.shape
    w = n_heads * HEAD_DIM
    tq = _tile(seq, 512)
    n_t = seq // tq
    assert row0 % tq == 0
    r0 = row0 // tq
    mem_spec = pl.BlockSpec((None, n_mem * n_heads, HEAD_DIM), lambda b, i: (b, 0, 0))
    args = [q, mk.reshape(nb, n_mem * n_heads, HEAD_DIM), mv.reshape(nb, n_mem * n_heads, HEAD_DIM)]
    in_specs = [pl.BlockSpec((tq, w), lambda b, i: (r0 + b * n_t + i, 0)), mem_spec, mem_spec]
    aliases = {}
    if into is not None:
        aliases = {len(args): 0}
        args.append(into)
        in_specs.append(pl.BlockSpec(memory_space=pl.ANY))
    return pl.pallas_call(
        functools.partial(_cross_attn_kernel, n_heads=n_heads, n_mem=n_mem),
        grid=(nb, n_t),
        in_specs=in_specs,
        out_specs=pl.BlockSpec((tq, w), lambda b, i: (r0 + b * n_t + i, 0)),
        out_shape=jax.ShapeDtypeStruct((q.shape[0], w), BF16),
        input_output_aliases=aliases,
        compiler_params=_params("parallel", "parallel"),
        name="cross_attn",
    )(*args)


def kernel(x_prompt, x_sample, cache_sb_k, cache_sb_v, state_ret, cache_band_k, cache_band_v, cache_mem_k, cache_mem_v, mem_prompt, w_in_ab, w_out_ab, ret_norm_g, w_qkv_band, w_out_band, rel_bias_band, norm_g, mem_norm_g, w_xq, w_xk, w_xv, w_xo, ffn_w_gate, ffn_w_up, ffn_w_down, moe_router, moe_w_gate, moe_w_up, moe_w_down, final_norm_g):
    bp, s, d = x_prompt.shape
    nb, t, _ = x_sample.shape
    assert bp == 1
    past = cache_sb_k.shape[1]
    band_past = cache_band_k.shape[1]
    h_sb = cache_sb_k.shape[2]
    h_ret = state_ret.shape[1]
    h_band = cache_band_k.shape[2]
    d_sb, d_ret = h_sb * HEAD_DIM, h_ret * HEAD_DIM
    n_mem = mem_prompt.shape[1]
    d_x = w_xq.shape[2]
    depth = w_xq.shape[0]
    d_ff = ffn_w_gate.shape[1]
    ms = nb * t
    assert d_sb == d_ret and 3 * d_sb + 4 * d_ret == w_in_ab.shape[1]

    x = jnp.concatenate([x_prompt.reshape(s, d), x_sample.reshape(ms, d)], axis=0)
    mem = mem_prompt.reshape(n_mem, d)
    bf = lambda a: a.astype(BF16)

    cos_p, sin_p = _rope_tables(jnp.arange(s, dtype=jnp.int32))
    cos_s, sin_s = _rope_tables(past + jnp.arange(t, dtype=jnp.int32))

    mem_k_list, mem_v_list = [], []
    for l in range(depth):
        mem_n = rmsnorm(mem, mem_norm_g[l], BF16)
        mk_p = matmul([(mem_n, 0, 0)], bf(w_xk[l]), d)
        mv_p = matmul([(mem_n, 0, 0)], bf(w_xv[l]), d)
        mem_k_list.append(mk_p.reshape(1, n_mem, H_X, HEAD_DIM))
        mem_v_list.append(mv_p.reshape(1, n_mem, H_X, HEAD_DIM))

        if l % 2 == 0:
            p = norm_linear(x, norm_g[l, 0], [w_in_ab], F32, tn_pref=1024)
            o_sb = sb_prompt(p, s, h_sb, 0, h_sb, 2 * h_sb)
            o_sb = sb_sample(p, cache_sb_k, cache_sb_v, o_sb, s, nb, t, 0, 1, 2)
            t_ret = _tile(s, 256, CHUNK)
            o_r, ret_state_prompt = retention(p, cos_p, sin_p, ret_norm_g, jnp.zeros((1,) + state_ret.shape[1:], F32),
                                              None, 0, 1, s, t_ret, h_ret, 3, 4, 5, 6)
            o_r, ret_state_sample = retention(p, cos_s, sin_s, ret_norm_g, state_ret.astype(F32),
                                              o_r, s, nb, t, t, h_ret, 3, 4, 5, 6)
            x = matmul([(o_sb, 0, 0), (o_r, 0, d_sb)], bf(w_out_ab), d_sb, residual=x)
            sb_k_prompt = p[:s, d_sb:2 * d_sb].reshape(1, s, h_sb, HEAD_DIM)
            sb_v_prompt = p[:s, 2 * d_sb:3 * d_sb].reshape(1, s, h_sb, HEAD_DIM)
            sb_k_sample = p[s:, d_sb:2 * d_sb].reshape(nb, t, h_sb, HEAD_DIM)
            sb_v_sample = p[s:, 2 * d_sb:3 * d_sb].reshape(nb, t, h_sb, HEAD_DIM)
        else:
            d_band = h_band * HEAD_DIM
            p = norm_linear(x, norm_g[l, 0], [w_qkv_band], F32, tn_pref=1024)
            o = band_prompt(p, rel_bias_band, s, h_band, 0, h_band, 2 * h_band)
            o, band_k_sample, band_v_sample = band_sample(p, cache_band_k, cache_band_v, rel_bias_band, o,
                                                          s, nb, t, 0, 1, 2)
            x = matmul([(o, 0, 0)], bf(w_out_band), d_band, residual=x)
            band_k_prompt = p[s - band_past:s, d_band:2 * d_band].reshape(1, band_past, h_band, HEAD_DIM)
            band_v_prompt = p[s - band_past:s, 2 * d_band:].reshape(1, band_past, h_band, HEAD_DIM)

        q = norm_linear(x, norm_g[l, 1], [w_xq[l]], F32, tn_pref=1024)
        o = cross_attn(q, mem_k_list[-1], mem_v_list[-1], None, 0, 1, s)
        o = cross_attn(q, cache_mem_k[l], cache_mem_v[l], o, s, nb, t)
        x = matmul([(o, 0, 0)], bf(w_xo[l]), d_x, residual=x)

        if l % 2 == 0:
            act = norm_linear(x, norm_g[l, 2], [ffn_w_gate, ffn_w_up], BF16, tn_pref=512)
            x = matmul([(act, 0, 0)], bf(ffn_w_down), d_ff, residual=x, tk_pref=d_ff)
        else:
            h, gates, idx = moe_route(x, norm_g[l, 2], moe_router)
            x = moe_top2(x, h, idx, gates, bf(moe_w_gate), bf(moe_w_up), bf(moe_w_down))

    y_prompt = rmsnorm(x, final_norm_g, F32, 0, s).reshape(1, s, d)
    y_sample = rmsnorm(x, final_norm_g, F32, s, ms).reshape(nb, t, d)
    mem_k_prompt = jnp.stack(mem_k_list, axis=0)
    mem_v_prompt = jnp.stack(mem_v_list, axis=0)
    return (y_prompt, y_sample, sb_k_prompt, sb_v_prompt, sb_k_sample, sb_v_sample,
            ret_state_prompt, ret_state_sample, band_k_prompt, band_v_prompt,
            band_k_sample, band_v_sample, mem_k_prompt, mem_v_prompt)
```

```python
import functools

import numpy as np
import jax
import jax.numpy as jnp
from jax import lax
from jax.experimental import pallas as pl
from jax.experimental.pallas import tpu as pltpu

F32 = jnp.float32
BF16 = jnp.bfloat16

HEAD_DIM = 128
CHUNK = 64
N_BAND_CHUNKS = 8
BAND_WINDOW = N_BAND_CHUNKS * CHUNK
MAX_REL = 128
H_X = 4
TOP_K = 2
RMS_EPS = 1e-6
ROPE_BASE = 10000.0
NEG_INF = -1e30
SCALE = HEAD_DIM ** -0.5
LOG2E = 1.4426950408889634

VMEM_LIMIT_BYTES = 56 * 1024 * 1024


def _params(*sem):
    return pltpu.CompilerParams(dimension_semantics=sem, vmem_limit_bytes=VMEM_LIMIT_BYTES)


def _tile(n, pref, mult=8):
    t = min(n, pref)
    while t > mult and (n % t or t % mult):
        t -= mult
    assert n % t == 0, (n, pref)
    return t


def _dot(a, b):
    return jnp.dot(a, b, preferred_element_type=F32)


def _dot_nt(a, b):
    return lax.dot_general(a, b, (((1,), (1,)), ((), ())), preferred_element_type=F32)


def _dot_tn(a, b):
    return lax.dot_general(a, b, (((0,), (0,)), ((), ())), preferred_element_type=F32)


def _silu(a):
    return a * (1.0 / (1.0 + jnp.exp(-a)))


def _rows(parts):
    return parts[0] if len(parts) == 1 else jnp.concatenate(parts, axis=0)


def _rmsnorm_kernel(x_ref, g_ref, o_ref):
    x = x_ref[...]
    y = x * lax.rsqrt(jnp.mean(x * x, axis=-1, keepdims=True) + RMS_EPS) * g_ref[...]
    o_ref[...] = y.astype(o_ref.dtype)


def rmsnorm(x, g, out_dtype, row0=0, rows=None):
    d = x.shape[1]
    m = x.shape[0] - row0 if rows is None else rows
    tm = _tile(m, 512)
    assert row0 % tm == 0
    return pl.pallas_call(
        _rmsnorm_kernel,
        grid=(m // tm,),
        in_specs=[pl.BlockSpec((tm, d), lambda i: (row0 // tm + i, 0)),
                  pl.BlockSpec((1, d), lambda i: (0, 0))],
        out_specs=pl.BlockSpec((tm, d), lambda i: (i, 0)),
        out_shape=jax.ShapeDtypeStruct((m, d), out_dtype),
        compiler_params=_params("parallel"),
        name="rmsnorm",
    )(x, g.reshape(1, d).astype(F32))


def _mm_kernel(*refs, n_parts, has_res, nk):
    xs, ws = refs[:n_parts], refs[n_parts:2 * n_parts]
    pos = 2 * n_parts
    res_ref = refs[pos] if has_res else None
    pos += int(has_res)
    o_ref = refs[pos]
    part = _dot(xs[0][...], ws[0][...])
    for x_ref, w_ref in zip(xs[1:], ws[1:]):
        part = part + _dot(x_ref[...], w_ref[...])
    if nk == 1:
        if has_res:
            part = res_ref[...] + part
        o_ref[...] = part.astype(o_ref.dtype)
        return
    acc_ref = refs[pos + 1]
    k = pl.program_id(2)

    @pl.when(k == 0)
    def _():
        acc_ref[...] = part

    @pl.when(k > 0)
    def _():
        acc_ref[...] += part

    @pl.when(k == nk - 1)
    def _():
        out = acc_ref[...]
        if has_res:
            out = res_ref[...] + out
        o_ref[...] = out.astype(o_ref.dtype)


def matmul(parts, w, k_part, *, col_off=0, n_cols=None, out_dtype=F32, residual=None,
           tm_pref=1024, tn_pref=512, tk_pref=2048):
    m = parts[0][0].shape[0]
    n_cols = w.shape[1] - col_off if n_cols is None else n_cols
    tm = _tile(m, tm_pref)
    tn = _tile(n_cols, tn_pref, 128)
    tk = _tile(k_part, tk_pref, 128)
    nk = k_part // tk
    assert col_off % tn == 0
    in_specs, args = [], []
    for x, xo, _ in parts:
        assert xo % tk == 0
        in_specs.append(pl.BlockSpec((tm, tk), lambda i, j, k, xo=xo: (i, xo // tk + k)))
        args.append(x)
    for _, _, wo in parts:
        assert wo % tk == 0
        in_specs.append(pl.BlockSpec((tk, tn), lambda i, j, k, wo=wo: (wo // tk + k, col_off // tn + j)))
        args.append(w)
    if residual is not None:
        in_specs.append(pl.BlockSpec((tm, tn), lambda i, j, k: (i, j)))
        args.append(residual)
    return pl.pallas_call(
        functools.partial(_mm_kernel, n_parts=len(parts), has_res=residual is not None, nk=nk),
        grid=(m // tm, n_cols // tn, nk),
        in_specs=in_specs,
        out_specs=pl.BlockSpec((tm, tn), lambda i, j, k: (i, j)),
        out_shape=jax.ShapeDtypeStruct((m, n_cols), out_dtype),
        scratch_shapes=[pltpu.VMEM((tm, tn), F32)] if nk > 1 else [],
        compiler_params=_params("parallel", "parallel", "arbitrary"),
        name="matmul",
    )(*args)


def _norm_linear_kernel(x_ref, g_ref, *rest, n_w):
    w_refs, o_ref, wb_refs = rest[:n_w], rest[n_w], rest[n_w + 1:]

    @pl.when(pl.program_id(1) == 0)
    def _():
        for w_ref, wb_ref in zip(w_refs, wb_refs):
            wb_ref[...] = w_ref[...].astype(BF16)

    x = x_ref[...]
    h = (x * lax.rsqrt(jnp.mean(x * x, axis=-1, keepdims=True) + RMS_EPS) * g_ref[...]).astype(BF16)
    if n_w == 1:
        out = _dot(h, wb_refs[0][...])
    else:
        out = _silu(_dot(h, wb_refs[0][...])) * _dot(h, wb_refs[1][...])
    o_ref[...] = out.astype(o_ref.dtype)


def norm_linear(x, g, ws, out_dtype, *, tn_pref):
    m, d = x.shape
    n = ws[0].shape[1]
    tm = _tile(m, 1024)
    tn = _tile(n, tn_pref, 128)
    return pl.pallas_call(
        functools.partial(_norm_linear_kernel, n_w=len(ws)),
        grid=(n // tn, m // tm),
        in_specs=([pl.BlockSpec((tm, d), lambda j, i: (i, 0)), pl.BlockSpec((1, d), lambda j, i: (0, 0))]
                  + [pl.BlockSpec((d, tn), lambda j, i: (0, j))] * len(ws)),
        out_specs=pl.BlockSpec((tm, tn), lambda j, i: (i, j)),
        out_shape=jax.ShapeDtypeStruct((m, n), out_dtype),
        scratch_shapes=[pltpu.VMEM((d, tn), BF16)] * len(ws),
        compiler_params=_params("parallel", "arbitrary"),
        name="norm_linear",
    )(x, g.reshape(1, d).astype(F32), *[w.astype(F32) for w in ws])


def _router_kernel(x_ref, g_ref, r_ref, h_ref, o_ref, *, n_experts):
    x = x_ref[...]
    hn = x * lax.rsqrt(jnp.mean(x * x, axis=-1, keepdims=True) + RMS_EPS) * g_ref[...]
    h_ref[...] = hn.astype(h_ref.dtype)
    logits = jnp.dot(hn, r_ref[...], preferred_element_type=F32, precision=lax.Precision.HIGHEST)
    lane = lax.broadcasted_iota(jnp.int32, logits.shape, 1)
    n_lanes = logits.shape[1]
    lg = jnp.where(lane < n_experts, logits, -jnp.inf)
    m1 = jnp.max(lg, axis=1, keepdims=True)
    i1 = jnp.min(jnp.where(lg == m1, lane, n_lanes), axis=1, keepdims=True)
    lg2 = jnp.where(lane == i1, -jnp.inf, lg)
    m2 = jnp.max(lg2, axis=1, keepdims=True)
    i2 = jnp.min(jnp.where(lg2 == m2, lane, n_lanes), axis=1, keepdims=True)
    e2 = jnp.exp(m2 - m1)
    inv = 1.0 / (1.0 + e2)
    o_ref[...] = (jnp.where(lane == 0, inv, 0.0) + jnp.where(lane == 1, e2 * inv, 0.0)
                  + jnp.where(lane == 2, i1.astype(F32), 0.0) + jnp.where(lane == 3, i2.astype(F32), 0.0))


def moe_route(x, g, router):
    m, d = x.shape
    n_e = router.shape[1]
    tm = _tile(m, 512)
    r_pad = jnp.zeros((d, 128), F32).at[:, :n_e].set(router.astype(F32))
    h, out = pl.pallas_call(
        functools.partial(_router_kernel, n_experts=n_e),
        grid=(m // tm,),
        in_specs=[pl.BlockSpec((tm, d), lambda i: (i, 0)),
                  pl.BlockSpec((1, d), lambda i: (0, 0)),
                  pl.BlockSpec((d, 128), lambda i: (0, 0))],
        out_specs=[pl.BlockSpec((tm, d), lambda i: (i, 0)), pl.BlockSpec((tm, 128), lambda i: (i, 0))],
        out_shape=[jax.ShapeDtypeStruct((m, d), BF16), jax.ShapeDtypeStruct((m, 128), F32)],
        compiler_params=_params("parallel"),
        name="moe_router",
    )(x, g.reshape(1, d).astype(F32), r_pad)
    return h, out[:, :TOP_K], out[:, TOP_K:2 * TOP_K].astype(jnp.int32)


MOE_ROW_TILE = 256
MOE_GATHER_CHUNK = 512
MOE_COMBINE_TILE = 1024


def _i32(a):
    return a.astype(jnp.int32)


def _count_le(ends, v):
    return jnp.sum(_i32(ends[None, :] <= v[:, None]), axis=1, dtype=jnp.int32)


def _moe_plan(idx, gates, n_e, tr, tc, tt):
    m = idx.shape[0]
    assert (TOP_K * m) % tr == 0 and m % tc == 0 and m % tt == 0
    n_tiles = TOP_K * m // tr + n_e
    n_rows = n_tiles * tr
    routed = jnp.zeros((m, n_e), jnp.int32)
    for k in range(TOP_K):
        routed = routed + _i32(idx[:, k:k + 1] == jnp.arange(n_e, dtype=jnp.int32)[None, :])
    csum = jnp.cumsum(routed, axis=0, dtype=jnp.int32)
    rank = csum - routed
    tiles_e = (csum[-1] + tr - 1) // tr
    tile_end = jnp.cumsum(tiles_e, dtype=jnp.int32)
    row_start = (tile_end - tiles_e) * tr
    dest = row_start[idx] + jnp.take_along_axis(rank, idx, axis=1)
    gate_rows = jnp.zeros((n_rows,), F32).at[dest.reshape(-1)].set(gates.reshape(-1), unique_indices=True)
    tile_ids = jnp.arange(n_tiles, dtype=jnp.int32)
    tile_valid = tile_ids < tile_end[-1]
    tile_expert = jnp.minimum(_count_le(tile_end, tile_ids), n_e - 1)

    def visits(first_chunk, n_visits, n_max, owner_of):
        end = jnp.cumsum(n_visits, dtype=jnp.int32)
        start = end - n_visits
        v = jnp.arange(n_max, dtype=jnp.int32)
        live = v < end[-1]
        slot = jnp.minimum(_count_le(end, v), n_visits.shape[0] - 1)
        owner = owner_of(slot)
        chunk = first_chunk[slot] + v - start[slot]
        last = jnp.maximum(end[-1] - 1, 0)
        owner = jnp.where(live, owner, owner[last])
        chunk = jnp.where(live, chunk, chunk[last])
        first = live & ((v == 0) | (owner != jnp.roll(owner, 1)))
        return _i32(owner), _i32(chunk), _i32(first) + 2 * _i32(live)

    rank0 = (tile_ids - (tile_end - tiles_e)[tile_expert]) * tr
    rank1 = jnp.minimum(rank0 + tr, csum[-1][tile_expert]) - 1
    csum_e = csum.T[tile_expert]
    first_tok = jnp.sum(_i32(csum_e <= rank0[:, None]), axis=1, dtype=jnp.int32)
    last_tok = jnp.sum(_i32(csum_e <= rank1[:, None]), axis=1, dtype=jnp.int32)
    c_first = jnp.where(tile_valid, first_tok, 0) // tc
    c_last = jnp.where(tile_valid, last_tok, 0) // tc
    g_plan = visits(c_first, c_last - c_first + 1, n_tiles + n_e * (m // tc - 1), lambda s: s)
    n_tt = m // tt
    before = jnp.concatenate([jnp.zeros((1, n_e), jnp.int32), csum[tt - 1::tt]], axis=0)
    lo = row_start[None, :] + before[:-1]
    hi = row_start[None, :] + before[1:]
    n_vis = jnp.where(hi > lo, (hi - 1) // tr - lo // tr + 1, 0)
    c_plan = visits((lo // tr).reshape(-1), n_vis.reshape(-1), n_tt * n_e + n_tiles - 1, lambda s: s // n_e)
    return dict(n_tiles=n_tiles, dest=dest, gate_rows=gate_rows.reshape(n_rows, 1),
                tile_expert=tile_expert, tile_valid=_i32(tile_valid), gather=g_plan, combine=c_plan)


def _moe_select(dest_ref, row0, n_rows):
    row = row0 + lax.broadcasted_iota(jnp.int32, (dest_ref.shape[0], n_rows), 1)
    hit = row == dest_ref[:, 0:1]
    for k in range(1, TOP_K):
        hit = hit | (row == dest_ref[:, k:k + 1])
    return jnp.where(hit, 1.0, 0.0).astype(BF16)


def _moe_gather_kernel(vt_ref, vc_ref, vf_ref, dest_ref, h_ref, o_ref):
    v = pl.program_id(0)
    flags = vf_ref[v]
    tr = o_ref.shape[0]

    def gathered():
        sel = _moe_select(dest_ref, vt_ref[v] * tr, tr)
        return _dot_tn(sel, h_ref[...]).astype(o_ref.dtype)

    @pl.when(flags == 3)
    def _():
        o_ref[...] = gathered()

    @pl.when(flags == 2)
    def _():
        o_ref[...] += gathered()


def _moe_up_kernel(te_ref, tv_ref, x_ref, wg_ref, wu_ref, o_ref):
    j = pl.program_id(1)

    @pl.when(tv_ref[j] != 0)
    def _():
        x = x_ref[...]
        o_ref[...] = (_silu(_dot(x, wg_ref[...])) * _dot(x, wu_ref[...])).astype(o_ref.dtype)

    @pl.when(tv_ref[j] == 0)
    def _():
        o_ref[...] = jnp.zeros_like(o_ref)


def _moe_down_kernel(te_ref, tv_ref, a_ref, wd_ref, g_ref, o_ref):
    j = pl.program_id(0)

    @pl.when(tv_ref[j] != 0)
    def _():
        o_ref[...] = (_dot(a_ref[...], wd_ref[...]) * g_ref[...]).astype(o_ref.dtype)

    @pl.when(tv_ref[j] == 0)
    def _():
        o_ref[...] = jnp.zeros_like(o_ref)


def _moe_combine_kernel(vi_ref, vc_ref, vf_ref, dest_ref, y_ref, x_ref, o_ref):
    v = pl.program_id(0)
    flags = vf_ref[v]
    tr = y_ref.shape[0]

    @pl.when(flags % 2 == 1)
    def _():
        o_ref[...] = x_ref[...]

    @pl.when(flags >= 2)
    def _():
        sel = _moe_select(dest_ref, vc_ref[v] * tr, tr)
        o_ref[...] += _dot(sel, y_ref[...])


def moe_top2(x, h, idx, gates, wg, wu, wd):
    m, d = h.shape
    n_e, _, f = wg.shape
    tr, tc, tt = MOE_ROW_TILE, _tile(m, MOE_GATHER_CHUNK, 16), _tile(m, MOE_COMBINE_TILE)
    plan =_moe_plan(idx, gates, n_e, tr, tc, tt)
    n_tiles = plan["n_tiles"]
    n_rows = n_tiles * tr

    vt, vc, vf = plan["gather"]
    xs = pl.pallas_call(
        _moe_gather_kernel,
        grid_spec=pltpu.PrefetchScalarGridSpec(
            num_scalar_prefetch=3, grid=(vt.shape[0],),
            in_specs=[pl.BlockSpec((tc, TOP_K), lambda v, vt, vc, vf: (vc[v], 0)),
                      pl.BlockSpec((tc, d), lambda v, vt, vc, vf: (vc[v], 0))],
            out_specs=pl.BlockSpec((tr, d), lambda v, vt, vc, vf: (vt[v], 0))),
        out_shape=jax.ShapeDtypeStruct((n_rows, d), BF16),
        compiler_params=_params("arbitrary"),
        name="moe_gather",
    )(vt, vc, vf, plan["dest"], h)

    tf = _tile(f, 1408, 128)
    act = pl.pallas_call(
        _moe_up_kernel,
        grid_spec=pltpu.PrefetchScalarGridSpec(
            num_scalar_prefetch=2, grid=(f // tf, n_tiles),
            in_specs=[pl.BlockSpec((tr, d), lambda c, j, te, tv: (j, 0)),
                      pl.BlockSpec((None, d, tf), lambda c, j, te, tv: (te[j], 0, c)),
                      pl.BlockSpec((None, d, tf), lambda c, j, te, tv: (te[j], 0, c))],
            out_specs=pl.BlockSpec((tr, tf), lambda c, j, te, tv: (j, c))),
        out_shape=jax.ShapeDtypeStruct((n_rows, f), BF16),
        compiler_params=_params("parallel", "parallel"),
        name="moe_up",
    )(plan["tile_expert"], plan["tile_valid"], xs, wg, wu)

    y = pl.pallas_call(
        _moe_down_kernel,
        grid_spec=pltpu.PrefetchScalarGridSpec(
            num_scalar_prefetch=2, grid=(n_tiles,),
            in_specs=[pl.BlockSpec((tr, f), lambda j, te, tv: (j, 0)),
                      pl.BlockSpec((None, f, d), lambda j, te, tv: (te[j], 0, 0)),
                      pl.BlockSpec((tr, 1), lambda j, te, tv: (j, 0))],
            out_specs=pl.BlockSpec((tr, d), lambda j, te, tv: (j, 0))),
        out_shape=jax.ShapeDtypeStruct((n_rows, d), BF16),
        compiler_params=_params("parallel"),
        name="moe_down",
    )(plan["tile_expert"], plan["tile_valid"], act, wd, plan["gate_rows"])

    vi, vc, vf = plan["combine"]
    return pl.pallas_call(
        _moe_combine_kernel,
        grid_spec=pltpu.PrefetchScalarGridSpec(
            num_scalar_prefetch=3, grid=(vi.shape[0],),
            in_specs=[pl.BlockSpec((tt, TOP_K), lambda v, vi, vc, vf: (vi[v], 0)),
                      pl.BlockSpec((tr, d), lambda v, vi, vc, vf: (vc[v], 0)),
                      pl.BlockSpec((tt, d), lambda v, vi, vc, vf: (vi[v], 0))],
            out_specs=pl.BlockSpec((tt, d), lambda v, vi, vc, vf: (vi[v], 0))),
        out_shape=jax.ShapeDtypeStruct((m, d), F32),
        compiler_params=_params("arbitrary"),
        name="moe_combine",
    )(vi, vc, vf, plan["dest"], y, x)


def _cumsum_matrix(tk):
    r = np.arange(2 * tk)[:, None]
    c = np.arange(2 * tk)[None, :]
    return jnp.asarray((r > c).astype(np.float32), dtype=BF16)


def _sb_sweep_step(qs, k2s, v2s, before, u, acc, carry):
    tq = qs[0].shape[0]
    tk = k2s[0].shape[0] // 2
    z = _rows([_dot_nt(q, k2) for q, k2 in zip(qs, k2s)])
    neg_l = jnp.maximum(z, 0.0) + jnp.log2(1.0 + jnp.exp2(-jnp.abs(z)))
    log_b = z - neg_l
    if before is not None:
        neg_l = jnp.where(before, neg_l, 0.0)
    neg_lb = neg_l.astype(BF16)
    after = _dot(neg_lb, u)
    w = jnp.exp2(log_b - (after + jnp.concatenate([carry, carry], axis=1)))
    if before is not None:
        w = jnp.where(before, w, 0.0)
    w = w.astype(BF16)
    acc = acc + _rows([_dot(w[n * tq:(n + 1) * tq], v2) for n, v2 in enumerate(v2s)])
    total = after[:, :1] + neg_lb[:, :1].astype(F32)
    return acc, carry + total


def _sb_prompt_kernel(q_ref, k_ref, v_ref, u_ref, o_ref, kb_ref, vb_ref, *, tq, tk):
    i = pl.program_id(1)

    @pl.when(i == 0)
    def _():
        kb_ref[...] = k_ref[...].astype(BF16)
        vb_ref[...] = v_ref[...].astype(BF16)

    q = (q_ref[...] * (SCALE * LOG2E)).astype(BF16)
    u = u_ref[...]
    steps_per_tile = tq // (2 * tk)
    state = (jnp.zeros((tq, HEAD_DIM), F32), jnp.zeros((tq, tk), F32))

    def step(off, r0, before, state):
        new = _sb_sweep_step([q[r0:]], [kb_ref[pl.ds(off, 2 * tk), :]], [vb_ref[pl.ds(off, 2 * tk), :]],
                             before, u, *[a[r0:] for a in state])
        return tuple(jnp.concatenate([a[:r0], b], axis=0) if r0 else b for a, b in zip(state, new))

    for r0 in reversed(range(0, tq, 2 * tk)):
        before = (lax.broadcasted_iota(jnp.int32, (tq - r0, 2 * tk), 1)
                  < lax.broadcasted_iota(jnp.int32, (tq - r0, 2 * tk), 0))
        state = step(pl.multiple_of(i * tq + r0, 2 * tk), r0, before, state)
    n_steps = i * steps_per_tile
    unroll = max(u for u in (1, 2, 4) if steps_per_tile % u == 0)

    def body(s, st):
        for k in range(unroll):
            st = step(pl.multiple_of((n_steps - 1 - unroll * s - k) * 2 * tk, 2 * tk), 0, None, st)
        return st

    acc, _ = lax.fori_loop(0, n_steps // unroll, body, state)
    o_ref[...] = acc.astype(o_ref.dtype)


def sb_prompt(p, s, n_heads, q_col, k_col, v_col):
    tk = HEAD_DIM
    tq = _tile(s, 2048, 2 * tk)
    return pl.pallas_call(
        functools.partial(_sb_prompt_kernel, tq=tq, tk=tk),
        grid=(n_heads, s // tq),
        in_specs=[pl.BlockSpec((tq, HEAD_DIM), lambda h, i: (i, q_col + h)),
                  pl.BlockSpec((s, HEAD_DIM), lambda h, i: (0, k_col + h)),
                  pl.BlockSpec((s, HEAD_DIM), lambda h, i: (0, v_col + h)),
                  pl.BlockSpec((2 * tk, 2 * tk), lambda h, i: (0, 0))],
        out_specs=pl.BlockSpec((tq, HEAD_DIM), lambda h, i: (i, h)),
        out_shape=jax.ShapeDtypeStruct((p.shape[0], n_heads * HEAD_DIM), BF16),
        scratch_shapes=[pltpu.VMEM((s, HEAD_DIM), BF16), pltpu.VMEM((s, HEAD_DIM), BF16)],
        compiler_params=_params("parallel", "arbitrary"),
        name="sb_prompt",
    )(p, p, p, _cumsum_matrix(tk))


def _sb_sample_kernel(q_ref, kn_ref, vn_ref, kc_hbm, vc_hbm, u_ref, _into_ref, o_ref,
                      acc_ref, carry_ref, kf_ref, vf_ref, sem, *, t, tk, ck, n_heads):
    b, c = pl.program_id(0), pl.program_id(1)
    n_c = pl.num_programs(1)
    step = b * n_c + c
    slot = step % 2

    def chunk_copies(step, slot):
        first = (n_c - 1 - step % n_c) * ck
        copies = []
        for h in range(n_heads):
            for hbm, buf in ((kc_hbm, kf_ref), (vc_hbm, vf_ref)):
                copies.append(pltpu.make_async_copy(hbm.at[step // n_c, pl.ds(first, ck), h, :],
                                                    buf.at[slot, h], sem.at[slot]))
        return copies

    @pl.when(step == 0)
    def _():
        for cp in chunk_copies(step, slot):
            cp.start()

    @pl.when(step + 1 < pl.num_programs(0) * n_c)
    def _():
        for cp in chunk_copies(step + 1, 1 - slot):
            cp.start()

    heads = [slice(h * HEAD_DIM, (h + 1) * HEAD_DIM) for h in range(n_heads)]
    qs = [(q_ref[:, sl] * (SCALE * LOG2E)).astype(BF16) for sl in heads]
    u = u_ref[...]

    @pl.when(c == 0)
    def _():
        reps = 2 * tk // t
        k2s = [jnp.concatenate([kn_ref[:, sl].astype(BF16)] * reps, axis=0) for sl in heads]
        v2s = [jnp.concatenate([vn_ref[:, sl].astype(BF16)] * reps, axis=0) for sl in heads]
        before = (lax.broadcasted_iota(jnp.int32, (t, 2 * tk), 1) < lax.broadcasted_iota(jnp.int32, (t, 2 * tk), 0))
        acc, carry = _sb_sweep_step(qs, k2s, v2s, _rows([before] * n_heads), u,
                                    jnp.zeros(acc_ref.shape, F32), jnp.zeros(carry_ref.shape, F32))
        acc_ref[...] = acc
        carry_ref[...] = carry

    for cp in chunk_copies(step, slot):
        cp.wait()
    n_steps = ck // (2 * tk)

    def body(s, state):
        off = pl.multiple_of((n_steps - 1 - s) * 2 * tk, 2 * tk)
        k2s = [kf_ref[slot, h, pl.ds(off, 2 * tk), :].astype(BF16) for h in range(n_heads)]
        v2s = [vf_ref[slot, h, pl.ds(off, 2 * tk), :].astype(BF16) for h in range(n_heads)]
        return _sb_sweep_step(qs, k2s, v2s, None, u, *state)

    acc, carry = lax.fori_loop(0, n_steps, body, (acc_ref[...], carry_ref[...]), unroll=2)
    acc_ref[...] = acc
    carry_ref[...] = carry

    @pl.when(c == pl.num_programs(1) - 1)
    def _():
        for h, sl in enumerate(heads):
            o_ref[:, sl] = acc_ref[h * t:(h + 1) * t, :].astype(o_ref.dtype)


def sb_sample(p, cache_k, cache_v, into, row0, nb, t, q_col, k_col, v_col):
    _, past, n_heads, _ = cache_k.shape
    tk = HEAD_DIM
    w = n_heads * HEAD_DIM
    ck = _tile(past, 2048, 2 * tk)
    n_c = past // ck
    assert (2 * tk) % t == 0 and row0 % t == 0
    r0 = row0 // t
    return pl.pallas_call(
        functools.partial(_sb_sample_kernel, t=t, tk=tk, ck=ck, n_heads=n_heads),
        grid=(nb, n_c),
        in_specs=[pl.BlockSpec((t, w), lambda b, c: (r0 + b, q_col)),
                  pl.BlockSpec((t, w), lambda b, c: (r0 + b, k_col)),
                  pl.BlockSpec((t, w), lambda b, c: (r0 + b, v_col)),
                  pl.BlockSpec(memory_space=pl.ANY), pl.BlockSpec(memory_space=pl.ANY),
                  pl.BlockSpec((2 * tk, 2 * tk), lambda b, c: (0, 0)),
                  pl.BlockSpec(memory_space=pl.ANY)],
        out_specs=pl.BlockSpec((t, w), lambda b, c: (r0 + b, 0)),
        out_shape=jax.ShapeDtypeStruct(into.shape, into.dtype),
        input_output_aliases={6: 0},
        scratch_shapes=[pltpu.VMEM((n_heads * t, HEAD_DIM), F32), pltpu.VMEM((n_heads * t, tk), F32),
                        pltpu.VMEM((2, n_heads, ck, HEAD_DIM), F32), pltpu.VMEM((2, n_heads, ck, HEAD_DIM), F32),
                        pltpu.SemaphoreType.DMA((2,))],
        compiler_params=_params("arbitrary", "arbitrary"),
        name="sb_sample",
    )(p, p, p, cache_k.astype(F32), cache_v.astype(F32), _cumsum_matrix(tk), into)


def _retention_log_decay(n_heads):
    return [float(np.log1p(-np.float32(2.0 ** (-5.0 - h)))) for h in range(n_heads)]


def _retention_kernel(q_ref, k_ref, v_ref, g_ref, cos_ref, sin_ref, gn_ref, s0_ref, *rest, t, n_chunks, log_g):
    o_ref, sout_ref, state_ref, dec_ref = rest[-4:]
    c = pl.program_id(1)

    @pl.when(c == 0)
    def _():
        state_ref[...] = s0_ref[...]
        rel = (lax.broadcasted_iota(jnp.int32, (t, t), 0) - lax.broadcasted_iota(jnp.int32, (t, t), 1)).astype(F32)
        for h, lg in enumerate(log_g):
            dec_ref[h] = jnp.where(rel >= 0, jnp.exp(lg * jnp.maximum(rel, 0.0)), 0.0)

    cos, sin = cos_ref[...], sin_ref[...]
    idx = lax.broadcasted_iota(jnp.int32, (t, HEAD_DIM), 0).astype(F32)
    for h, lg in enumerate(log_g):
        sl = slice(h * HEAD_DIM, (h + 1) * HEAD_DIM)
        qh, kh = q_ref[:, sl], k_ref[:, sl]
        qh = qh * cos + pltpu.roll(qh, HEAD_DIM // 2, 1) * sin
        kh = (kh * cos + pltpu.roll(kh, HEAD_DIM // 2, 1) * sin) * SCALE
        qb, vb = qh.astype(BF16), v_ref[:, sl].astype(BF16)
        scores = _dot_nt(qb, kh.astype(BF16)) * dec_ref[h]
        state = state_ref[h]
        o = _dot(scores.astype(BF16), vb) + _dot(qb, state.astype(BF16)) * jnp.exp((idx + 1.0) * lg)
        k_dec = (kh * jnp.exp((t - 1.0 - idx) * lg)).astype(BF16)
        state_ref[h] = float(np.exp(np.float32(t * lg))) * state + _dot_tn(k_dec, vb)
        o = o * lax.rsqrt(jnp.mean(o * o, axis=-1, keepdims=True) + RMS_EPS) * gn_ref[:, sl]
        o_ref[:, sl] = (o * _silu(g_ref[:, sl])).astype(o_ref.dtype)

    @pl.when(c == n_chunks - 1)
    def _():
        sout_ref[...] = state_ref[...]


def retention(p, cos2, sin2, ret_norm_g, state0, into, row0, nb, seq, t, n_heads, q_col, k_col, v_col, g_col):
    w = n_heads * HEAD_DIM
    n_chunks = seq // t
    assert row0 % t == 0 and seq % t == 0
    r0 = row0 // t

    def rows(col):
        return pl.BlockSpec((t, w), lambda b, c: (r0 + b * n_chunks + c, col))

    args = [p, p, p, p, cos2, sin2, ret_norm_g.reshape(1, w).astype(F32), state0]
    in_specs = [rows(q_col), rows(k_col), rows(v_col), rows(g_col),
                pl.BlockSpec((t, HEAD_DIM), lambda b, c: (c, 0)),
                pl.BlockSpec((t, HEAD_DIM), lambda b, c: (c, 0)),
                pl.BlockSpec((1, w), lambda b, c: (0, 0)),
                pl.BlockSpec((None, n_heads, HEAD_DIM, HEAD_DIM), lambda b, c: (b, 0, 0, 0))]
    aliases = {}
    if into is not None:
        aliases = {len(args): 0}
        args.append(into)
        in_specs.append(pl.BlockSpec(memory_space=pl.ANY))
    return pl.pallas_call(
        functools.partial(_retention_kernel, t=t, n_chunks=n_chunks, log_g=_retention_log_decay(n_heads)),
        grid=(nb, n_chunks),
        in_specs=in_specs,
        out_specs=[rows(0),
                   pl.BlockSpec((None, n_heads, HEAD_DIM, HEAD_DIM), lambda b, c: (b, 0, 0, 0))],
        out_shape=[jax.ShapeDtypeStruct((p.shape[0], w), BF16),
                   jax.ShapeDtypeStruct((nb, n_heads, HEAD_DIM, HEAD_DIM), F32)],
        input_output_aliases=aliases,
        scratch_shapes=[pltpu.VMEM((n_heads, HEAD_DIM, HEAD_DIM), F32), pltpu.VMEM((n_heads, t, t), F32)],
        compiler_params=_params("parallel", "arbitrary"),
        name="retention",
    )(*args)


def _rope_tables(pos):
    half = HEAD_DIM // 2
    inv = ROPE_BASE ** (-jnp.arange(half, dtype=F32) / half)
    ang = pos.astype(F32)[:, None] * inv[None, :]
    cos, sin = jnp.cos(ang), jnp.sin(ang)
    return jnp.concatenate([cos, cos], axis=1), jnp.concatenate([-sin, sin], axis=1)


def _band_bias_table(rel_bias, tq):
    n_heads = rel_bias.shape[0]
    w = BAND_WINDOW + CHUNK
    n = CHUNK + w - 1
    dist = BAND_WINDOW + CHUNK - 1 - np.arange(n)
    vec = rel_bias.astype(F32)[:, np.clip(dist, -MAX_REL, MAX_REL) + MAX_REL]
    vec = jnp.roll(vec, -(CHUNK - 1), axis=1)
    chunk_bias = jnp.tile(vec, (1, CHUNK))[:, :CHUNK * (n - 1)].reshape(n_heads, CHUNK, n - 1)[:, :, :w]
    return jnp.concatenate(
        [jnp.pad(chunk_bias, ((0, 0), (0, 0), (c * CHUNK, tq - (c + 1) * CHUNK)), constant_values=NEG_INF)
         for c in range(tq // CHUNK)], axis=1)


def _softmax_pv(scores, values):
    m = scores[0].max(axis=1, keepdims=True)
    for s in scores[1:]:
        m = jnp.maximum(m, s.max(axis=1, keepdims=True))
    ps = [jnp.exp(s - m) for s in scores]
    denom = ps[0].sum(axis=1, keepdims=True)
    for p in ps[1:]:
        denom = denom + p.sum(axis=1, keepdims=True)
    o = _dot(ps[0].astype(BF16), values[0])
    for p, v in zip(ps[1:], values[1:]):
        o = o + _dot(p.astype(BF16), v)
    return o * (1.0 / denom)


BAND_HEADS_PER_STEP = 8


def _band_prompt_kernel(*refs, tq, n_kb, hb):
    q_ref = refs[0]
    k_refs, v_refs = refs[1:1 + n_kb], refs[1 + n_kb:1 + 2 * n_kb]
    b_ref, o_ref = refs[1 + 2 * n_kb], refs[2 + 2 * n_kb]
    i = pl.program_id(1)
    for h in range(hb):
        sl = slice(h * HEAD_DIM, (h + 1) * HEAD_DIM)
        q = (q_ref[:, sl] * SCALE).astype(BF16)
        scores, values = [], []
        for kb in range(n_kb):
            s = _dot_nt(q, k_refs[kb][:, sl].astype(BF16)) + b_ref[h, :, kb * tq:(kb + 1) * tq]
            scores.append(s + jnp.where(i - (n_kb - 1) + kb >= 0, 0.0, NEG_INF))
            values.append(v_refs[kb][:, sl].astype(BF16))
        o_ref[:, sl] = _softmax_pv(scores, values).astype(o_ref.dtype)


def band_prompt(p, rel_bias, s, n_heads, q_col, k_col, v_col):
    tq = _tile(s, 256)
    hb = BAND_HEADS_PER_STEP
    w = hb * HEAD_DIM
    assert BAND_WINDOW % tq == 0 and tq % CHUNK == 0 and n_heads % hb == 0
    assert q_col % hb == 0 and k_col % hb == 0 and v_col % hb == 0
    n_kb = BAND_WINDOW // tq + 1

    def kv_specs(col):
        return [pl.BlockSpec((tq, w), lambda h, i, kb=kb: (jnp.maximum(i - (n_kb - 1) + kb, 0), col // hb + h))
                for kb in range(n_kb)]

    return pl.pallas_call(
        functools.partial(_band_prompt_kernel, tq=tq, n_kb=n_kb, hb=hb),
        grid=(n_heads // hb, s // tq),
        in_specs=([pl.BlockSpec((tq, w), lambda h, i: (i, q_col // hb + h))] + kv_specs(k_col) + kv_specs(v_col)
                  + [pl.BlockSpec((hb, tq, BAND_WINDOW + tq), lambda h, i: (h, 0, 0))]),
        out_specs=pl.BlockSpec((tq, w), lambda h, i: (i, h)),
        out_shape=jax.ShapeDtypeStruct((p.shape[0], n_heads * HEAD_DIM), BF16),
        compiler_params=_params("parallel", "parallel"),
        name="band_prompt",
    )(*([p] * (1 + 2 * n_kb)), _band_bias_table(rel_bias, tq))


def _band_sample_kernel(q_ref, kn_ref, vn_ref, kc_ref, vc_ref, b_ref, _into_ref, o_ref, ko_ref, vo_ref, *,
                        band_past, t, n_heads):
    keep = (band_past - t) * n_heads
    for h in range(n_heads):
        sl = slice(h * HEAD_DIM, (h + 1) * HEAD_DIM)
        q = (q_ref[:, sl] * SCALE).astype(BF16)
        kn, vn = kn_ref[:, sl], vn_ref[:, sl]
        kc = kc_ref[pl.ds(h, band_past, stride=n_heads), :].astype(BF16)
        vc = vc_ref[pl.ds(h, band_past, stride=n_heads), :].astype(BF16)
        s_c = _dot_nt(q, kc) + b_ref[h, :, :band_past]
        s_n = _dot_nt(q, kn.astype(BF16)) + b_ref[h, :, band_past:]
        o_ref[:, sl] = _softmax_pv([s_c, s_n], [vc, vn.astype(BF16)]).astype(o_ref.dtype)
        ko_ref[pl.ds(keep + h, t, stride=n_heads), :] = kn
        vo_ref[pl.ds(keep + h, t, stride=n_heads), :] = vn
    ko_ref[:keep] = kc_ref[t * n_heads:]
    vo_ref[:keep] = vc_ref[t * n_heads:]


def band_sample(p, cache_k, cache_v, rel_bias, into, row0, nb, t, q_col, k_col, v_col):
    _, band_past, n_heads, _ = cache_k.shape
    w = n_heads * HEAD_DIM
    assert t == CHUNK and band_past == BAND_WINDOW and row0 % t == 0
    r0 = row0 // t
    cache_spec = pl.BlockSpec((None, band_past * n_heads, HEAD_DIM), lambda b: (b, 0, 0))
    cache_shape = jax.ShapeDtypeStruct((nb, band_past * n_heads, HEAD_DIM), F32)
    cache_k = cache_k.reshape(cache_shape.shape)
    cache_v = cache_v.reshape(cache_shape.shape)
    o, k_out, v_out = pl.pallas_call(
        functools.partial(_band_sample_kernel, band_past=band_past, t=t, n_heads=n_heads),
        grid=(nb,),
        in_specs=[pl.BlockSpec((t, w), lambda b: (r0 + b, q_col)),
                  pl.BlockSpec((t, w), lambda b: (r0 + b, k_col)),
                  pl.BlockSpec((t, w), lambda b: (r0 + b, v_col)),
                  cache_spec, cache_spec,
                  pl.BlockSpec((n_heads, t, band_past + t), lambda b: (0, 0, 0)),
                  pl.BlockSpec(memory_space=pl.ANY)],
        out_specs=[pl.BlockSpec((t, w), lambda b: (r0 + b, 0)), cache_spec, cache_spec],
        out_shape=[jax.ShapeDtypeStruct(into.shape, into.dtype), cache_shape, cache_shape],
        input_output_aliases={6: 0},
        compiler_params=_params("parallel"),
        name="band_sample",
    )(p, p, p, cache_k.astype(F32), cache_v.astype(F32), _band_bias_table(rel_bias, t), into)
    out_4d = (nb, band_past, n_heads, HEAD_DIM)
    return o, k_out.reshape(out_4d), v_out.reshape(out_4d)


def _cross_attn_kernel(q_ref, mk_ref, mv_ref, *rest, n_heads, n_mem):
    o_ref = rest[-1]
    for h in range(n_heads):
        sl = slice(h * HEAD_DIM, (h + 1) * HEAD_DIM)
        q = (q_ref[:, sl] * SCALE).astype(BF16)
        mk = mk_ref[pl.ds(h, n_mem, stride=n_heads), :].astype(BF16)
        mv = mv_ref[pl.ds(h, n_mem, stride=n_heads), :].astype(BF16)
        o_ref[:, sl] = _softmax_pv([_dot_nt(q, mk)], [mv]).astype(o_ref.dtype)


def cross_attn(q, mk, mv, into, row0, nb, seq):
    _, n_mem, n_heads, _ = mk.shape
    w = n_heads * HEAD_DIM
    tq = _tile(seq, 512)
    n_t = seq // tq
    assert row0 % tq == 0
    r0 = row0 // tq
    mem_spec = pl.BlockSpec((None, n_mem * n_heads, HEAD_DIM), lambda b, i: (b, 0, 0))
    args = [q, mk.reshape(nb, n_mem * n_heads, HEAD_DIM), mv.reshape(nb, n_mem * n_heads, HEAD_DIM)]
    in_specs = [pl.BlockSpec((tq, w), lambda b, i: (r0 + b * n_t + i, 0)), mem_spec, mem_spec]
    aliases = {}
    if into is not None:
        aliases = {len(args): 0}
        args.append(into)
        in_specs.append(pl.BlockSpec(memory_space=pl.ANY))
    return pl.pallas_call(
        functools.partial(_cross_attn_kernel, n_heads=n_heads, n_mem=n_mem),
        grid=(nb, n_t),
        in_specs=in_specs,
        out_specs=pl.BlockSpec((tq, w), lambda b, i: (r0 + b * n_t + i, 0)),
        out_shape=jax.ShapeDtypeStruct((q.shape[0], w), BF16),
        input_output_aliases=aliases,
        compiler_params=_params("parallel", "parallel"),
        name="cross_attn",
    )(*args)


def kernel(x_prompt, x_sample, cache_sb_k, cache_sb_v, state_ret, cache_band_k, cache_band_v, cache_mem_k, cache_mem_v, mem_prompt, w_in_ab, w_out_ab, ret_norm_g, w_qkv_band, w_out_band, rel_bias_band, norm_g, mem_norm_g, w_xq, w_xk, w_xv, w_xo, ffn_w_gate, ffn_w_up, ffn_w_down, moe_router, moe_w_gate, moe_w_up, moe_w_down, final_norm_g):
    bp, s, d = x_prompt.shape
    nb, t, _ = x_sample.shape
    assert bp == 1
    past = cache_sb_k.shape[1]
    band_past = cache_band_k.shape[1]
    h_sb = cache_sb_k.shape[2]
    h_ret = state_ret.shape[1]
    h_band = cache_band_k.shape[2]
    d_sb, d_ret = h_sb * HEAD_DIM, h_ret * HEAD_DIM
    n_mem = mem_prompt.shape[1]
    d_x = w_xq.shape[2]
    depth = w_xq.shape[0]
    d_ff = ffn_w_gate.shape[1]
    ms = nb * t
    assert d_sb == d_ret and 3 * d_sb + 4 * d_ret == w_in_ab.shape[1]

    x = jnp.concatenate([x_prompt.reshape(s, d), x_sample.reshape(ms, d)], axis=0)
    mem = mem_prompt.reshape(n_mem, d)
    bf = lambda a: a.astype(BF16)

    cos_p, sin_p = _rope_tables(jnp.arange(s, dtype=jnp.int32))
    cos_s, sin_s = _rope_tables(past + jnp.arange(t, dtype=jnp.int32))

    mem_k_list, mem_v_list = [], []
    for l in range(depth):
        mem_n = rmsnorm(mem, mem_norm_g[l], BF16)
        mk_p = matmul([(mem_n, 0, 0)], bf(w_xk[l]), d)
        mv_p = matmul([(mem_n, 0, 0)], bf(w_xv[l]), d)
        mem_k_list.append(mk_p.reshape(1, n_mem, H_X, HEAD_DIM))
        mem_v_list.append(mv_p.reshape(1, n_mem, H_X, HEAD_DIM))

        if l % 2 == 0:
            p = norm_linear(x, norm_g[l, 0], [w_in_ab], F32, tn_pref=1024)
            o_sb = sb_prompt(p, s, h_sb, 0, h_sb, 2 * h_sb)
            o_sb = sb_sample(p, cache_sb_k, cache_sb_v, o_sb, s, nb, t, 0, 1, 2)
            t_ret = _tile(s, 256, CHUNK)
            o_r, ret_state_prompt = retention(p, cos_p, sin_p, ret_norm_g, jnp.zeros((1,) + state_ret.shape[1:], F32),
                                              None, 0, 1, s, t_ret, h_ret, 3, 4, 5, 6)
            o_r, ret_state_sample = retention(p, cos_s, sin_s, ret_norm_g, state_ret.astype(F32),
                                              o_r, s, nb, t, t, h_ret, 3, 4, 5, 6)
            x = matmul([(o_sb, 0, 0), (o_r, 0, d_sb)], bf(w_out_ab), d_sb, residual=x)
            sb_k_prompt = p[:s, d_sb:2 * d_sb].reshape(1, s, h_sb, HEAD_DIM)
            sb_v_prompt = p[:s, 2 * d_sb:3 * d_sb].reshape(1, s, h_sb, HEAD_DIM)
            sb_k_sample = p[s:, d_sb:2 * d_sb].reshape(nb, t, h_sb, HEAD_DIM)
            sb_v_sample = p[s:, 2 * d_sb:3 * d_sb].reshape(nb, t, h_sb, HEAD_DIM)
        else:
            d_band = h_band * HEAD_DIM
            p = norm_linear(x, norm_g[l, 0], [w_qkv_band], F32, tn_pref=1024)
            o = band_prompt(p, rel_bias_band, s, h_band, 0, h_band, 2 * h_band)
            o, band_k_sample, band_v_sample = band_sample(p, cache_band_k, cache_band_v, rel_bias_band, o,
                                                          s, nb, t, 0, 1, 2)
            x = matmul([(o, 0, 0)], bf(w_out_band), d_band, residual=x)
            band_k_prompt = p[s - band_past:s, d_band:2 * d_band].reshape(1, band_past, h_band, HEAD_DIM)
            band_v_prompt = p[s - band_past:s, 2 * d_band:].reshape(1, band_past, h_band, HEAD_DIM)

        q = norm_linear(x, norm_g[l, 1], [w_xq[l]], F32, tn_pref=1024)
        o = cross_attn(q, mem_k_list[-1], mem_v_list[-1], None, 0, 1, s)
        o = cross_attn(q, cache_mem_k[l], cache_mem_v[l], o, s, nb, t)
        x = matmul([(o, 0, 0)], bf(w_xo[l]), d_x, residual=x)

        if l % 2 == 0:
            act = norm_linear(x, norm_g[l, 2], [ffn_w_gate, ffn_w_up], BF16, tn_pref=512)
            x = matmul([(act, 0, 0)], bf(ffn_w_down), d_ff, residual=x, tk_pref=d_ff)
        else:
            h, gates, idx = moe_route(x, norm_g[l, 2], moe_router)
            x = moe_top2(x, h, idx, gates, bf(moe_w_gate), bf(moe_w_up), bf(moe_w_down))

    y_prompt = rmsnorm(x, final_norm_g, F32, 0, s).reshape(1, s, d)
    y_sample = rmsnorm(x, final_norm_g, F32, s, ms).reshape(nb, t, d)
    mem_k_prompt = jnp.stack(mem_k_list, axis=0)
    mem_v_prompt = jnp.stack(mem_v_list, axis=0)
    return (y_prompt, y_sample, sb_k_prompt, sb_v_prompt, sb_k_sample, sb_v_sample,
            ret_state_prompt, ret_state_sample, band_k_prompt, band_v_prompt,
            band_k_sample, band_v_sample, mem_k_prompt, mem_v_prompt)
```
